```python
import math
import jax, jax.numpy as jnp
from jax import lax
import numpy as np

D_MODEL = 2048
BATCH = 4
SEQ = 4096
DEPTH = 2
DEC_BATCH = 16
DEC_SEQ = 16
PAST_LEN = 1024

CHUNK = 64
Q_BLOCK = 128
H_A = 8
DK_A = 128
DV_A = 128
H_B = 4
DK_B = 128
DV_B = 256
H_C = 8
DK_C = 128
DV_C = 128
CONV_C = 4
CONV_DIM_C = H_C * (2 * DK_C + DV_C)
H_D = 8
D_HD = 64
DV_D = 2 * D_HD
D_FF = 5632
CONV_FFN = 3
N_HGRN_LAYERS = (DEPTH + 1) // 2
EPS = 1e-6
NEG_BIG = -1e30

SPLIT_AB = (H_A * DK_A, H_A * DK_A, H_A * DV_A, H_A * DV_A,
            H_B * DK_B, H_B * DK_B, H_B * DV_B, H_B, H_B, H_B * DV_B)
SPLIT_CD = (CONV_DIM_C, H_C, H_C, H_C * DV_C,
            H_D * 2 * D_HD, H_D * 2 * D_HD, H_D * DV_D)
IN_AB = sum(SPLIT_AB)
IN_CD = sum(SPLIT_CD)
OUT_AB = H_A * DV_A + H_B * DV_B
OUT_CD = H_C * DV_C + H_D * DV_D

kernel_name = 'hybrid_streaming_encoder_step'


def rms_norm(x, g):
    xf = x.astype(jnp.float32)
    y = xf * lax.rsqrt(jnp.mean(xf * xf, axis=-1, keepdims=True) + EPS)
    return (y * g.astype(jnp.float32)).astype(x.dtype)


def l2norm(x):
    return x * lax.rsqrt(jnp.sum(x * x, axis=-1, keepdims=True) + EPS)


def split_cols(z, sizes):
    idx = np.cumsum(sizes)[:-1].tolist()
    return jnp.split(z, idx, axis=-1)


def to_heads(x, h):
    b, t, _ = x.shape
    return x.reshape(b, t, h, -1).transpose(0, 2, 1, 3)


def merge_heads(o):
    b, h, t, d = o.shape
    return o.transpose(0, 2, 1, 3).reshape(b, t, h * d)


def to_blocks(x, L):
    b, h, t = x.shape[:3]
    return jnp.moveaxis(x.reshape(b, h, t // L, L, *x.shape[3:]), 2, 0)


def from_blocks(x):
    n, b, h, L = x.shape[:4]
    return jnp.moveaxis(x, 0, 2).reshape(b, h, n * L, *x.shape[4:])


def causal_dwconv(x, buf, w):
    W = w.shape[0]
    T = x.shape[1]
    xp = jnp.concatenate([buf.astype(x.dtype), x], axis=1)
    y = sum(xp[:, j:j + T] * w[j] for j in range(W))
    return y, xp[:, T:]


def hgrn2_chunked(q, k, v, logf, S0):
    L = min(CHUNK, q.shape[2])
    mask = jnp.tril(jnp.ones((L, L), dtype=bool))

    def step(S, blk):
        qb, kb, vb, lfb = blk
        b = jnp.cumsum(lfb, axis=2)
        rel = jnp.where(mask[:, :, None], b[:, :, :, None, :] - b[:, :, None, :, :], -jnp.inf)
        scores = jnp.einsum('bhtd,bhsd,bhtsd->bhts', qb, kb, jnp.exp(rel))
        o = (jnp.einsum('bhtd,bhde->bhte', qb * jnp.exp(b), S)
             + jnp.einsum('bhts,bhse->bhte', scores, vb))
        b_last = b[:, :, -1:]
        S_new = (jnp.exp(b_last[:, :, 0])[..., None] * S
                 + jnp.einsum('bhsd,bhse->bhde', kb * jnp.exp(b_last - b), vb))
        return S_new, o

    S, o = lax.scan(step, S0, (to_blocks(q, L), to_blocks(k, L), to_blocks(v, L), to_blocks(logf, L)))
    return from_blocks(o), S


def mlstm_chunked(q, k, v, ig, lf, C0, n0, m0):
    L = min(CHUNK, q.shape[2])
    mask = jnp.tril(jnp.ones((L, L), dtype=bool))

    def step(carry, blk):
        C, n, m = carry
        qb, kb, vb, igb, lfb = blk
        b = jnp.cumsum(lfb, axis=-1)
        D = jnp.where(mask, b[..., :, None] - b[..., None, :] + igb[..., None, :], -jnp.inf)
        inter = b + m[..., None]
        m_t = jnp.maximum(jnp.max(D, axis=-1), inter)
        P = jnp.exp(D - m_t[..., None]) * jnp.einsum('bhtd,bhsd->bhts', qb, kb)
        w_inter = jnp.exp(inter - m_t)
        num = (jnp.einsum('bhts,bhse->bhte', P, vb)
               + w_inter[..., None] * jnp.einsum('bhtd,bhde->bhte', qb, C))
        den = jnp.sum(P, axis=-1) + w_inter * jnp.einsum('bhtd,bhd->bht', qb, n)
        h = num / jnp.maximum(jnp.abs(den), jnp.exp(-m_t))[..., None]
        b_last = b[..., -1]
        wk = b_last[..., None] - b + igb
        m_new = jnp.maximum(b_last + m, jnp.max(wk, axis=-1))
        sc = jnp.exp(wk - m_new[..., None])
        carry_w = jnp.exp(b_last + m - m_new)
        C_new = carry_w[..., None, None] * C + jnp.einsum('bhs,bhsd,bhse->bhde', sc, kb, vb)
        n_new = carry_w[..., None] * n + jnp.einsum('bhs,bhsd->bhd', sc, kb)
        return (C_new, n_new, m_new), h

    (C, n, m), h = lax.scan(step, (C0, n0, m0),
                            (to_blocks(q, L), to_blocks(k, L), to_blocks(v, L),
                             to_blocks(ig, L), to_blocks(lf, L)))
    return from_blocks(h), C, n, m


def gdn_chunked(q, k, v, beta, lg, S0):
    L = min(CHUNK, q.shape[2])
    incl = jnp.tril(jnp.ones((L, L), dtype=bool))
    strict = jnp.tril(jnp.ones((L, L), dtype=bool), -1)
    eye = jnp.eye(L, dtype=jnp.float32)

    def step(S, blk):
        qb, kb, vb, bb, lgb = blk
        b = jnp.cumsum(lgb, axis=-1)
        rel = b[..., :, None] - b[..., None, :]
        dec_strict = jnp.exp(jnp.where(strict, rel, -jnp.inf))
        dec_incl = jnp.exp(jnp.where(incl, rel, -jnp.inf))
        M = bb[..., None] * jnp.einsum('bhtd,bhsd->bhts', kb, kb) * dec_strict
        rhs = bb[..., None] * (vb - jnp.exp(b)[..., None] * jnp.einsum('bhtd,bhde->bhte', kb, S))
        delta = lax.linalg.triangular_solve(M + eye, rhs, left_side=True, lower=True,
                                            unit_diagonal=True)
        o = (jnp.exp(b)[..., None] * jnp.einsum('bhtd,bhde->bhte', qb, S)
             + jnp.einsum('bhts,bhse->bhte', jnp.einsum('bhtd,bhsd->bhts', qb, kb) * dec_incl, delta))
        b_last = b[..., -1]
        S_new = (jnp.exp(b_last)[..., None, None] * S
                 + jnp.einsum('bhsd,bhse->bhde', kb * jnp.exp(b_last[..., None] - b)[..., None], delta))
        return S_new, o

    S, o = lax.scan(step, S0, (to_blocks(q, L), to_blocks(k, L), to_blocks(v, L),
                               to_blocks(beta, L), to_blocks(lg, L)))
    return from_blocks(o), S


def diff_attention(q, k, v, q_pos, k_pos, lam):
    B, T = q.shape[:2]
    Lq = min(Q_BLOCK, T)
    nq = T // Lq
    slopes = 2.0 ** (-8.0 * jnp.arange(1, H_D + 1, dtype=jnp.float32) / H_D)
    qb = jnp.moveaxis(q.reshape(B, nq, Lq, H_D, 2, D_HD), 1, 0)
    pb = q_pos.reshape(nq, Lq)
    k_chunk = k_pos // CHUNK

    def block(args):
        qi, pi = args
        s = jnp.einsum('bthcd,bshcd->bhcts', qi, k).astype(jnp.float32) * (D_HD ** -0.5)
        dist = jnp.abs(pi[:, None] - k_pos[None, :]).astype(jnp.float32)
        s = s - slopes[:, None, None, None] * dist
        allowed = k_chunk[None, :] <= (pi // CHUNK)[:, None]
        p = jax.nn.softmax(jnp.where(allowed, s, -jnp.inf), axis=-1)
        a = p[:, :, 0] - lam * p[:, :, 1]
        return jnp.einsum('bhts,bshe->bthe', a, v)

    o = lax.map(block, (qb, pb))
    return jnp.moveaxis(o, 0, 1).reshape(B, T, H_D, DV_D)


def mixer_ab(h, s, p, lower_bounds, layer):
    f32 = jnp.float32
    z = (h @ p['w_in']).astype(f32)
    aq, af, ai, ag, bq, bk, bv, big, bfg, bo = split_cols(z, SPLIT_AB)
    lb = jnp.cumsum(jax.nn.softmax(lower_bounds.astype(f32), axis=0), axis=0)[layer // 2]
    f = lb + (1.0 - lb) * jax.nn.sigmoid(af)
    o_a, S_new = hgrn2_chunked(to_heads(aq, H_A), to_heads(1.0 - f, H_A), to_heads(ai, H_A),
                               to_heads(jnp.log(f), H_A), s['hgrn'].astype(f32))
    o_a = merge_heads(rms_norm(o_a, p['hgrn_norm'])) * jax.nn.silu(ag)
    ig = (big + p['i_bias']).transpose(0, 2, 1)
    lf = jax.nn.log_sigmoid(bfg + p['f_bias']).transpose(0, 2, 1)
    o_b, C_new, n_new, m_new = mlstm_chunked(
        to_heads(bq, H_B) * (DK_B ** -0.5), to_heads(bk, H_B), to_heads(bv, H_B), ig, lf,
        s['mlstm_c'].astype(f32), s['mlstm_n'].astype(f32), s['mlstm_m'].astype(f32))
    o_b = merge_heads(rms_norm(o_b, p['mlstm_norm'])) * jax.nn.sigmoid(bo)
    out = jnp.concatenate([o_a, o_b], axis=-1).astype(h.dtype) @ p['w_out']
    return out, (S_new, C_new, n_new, m_new)


def mixer_cd(h, s, p, layer):
    f32 = jnp.float32
    B, T, _ = h.shape
    z = (h @ p['w_in']).astype(f32)
    cqkv, cb, ca, cz, dq, dk, dv = split_cols(z, SPLIT_CD)
    conv_out, conv_new = causal_dwconv(cqkv, s['gdn_conv'].astype(f32), p['conv_w'].astype(f32))
    cq, ck, cv = split_cols(jax.nn.silu(conv_out), (H_C * DK_C, H_C * DK_C, H_C * DV_C))
    q = l2norm(to_heads(cq, H_C)) * (DK_C ** -0.5)
    k = l2norm(to_heads(ck, H_C))
    v = to_heads(cv, H_C)
    beta = jax.nn.sigmoid(cb).transpose(0, 2, 1)
    lg = (-jnp.exp(p['a_log'].astype(f32)) * jax.nn.softplus(ca + p['dt_bias'])).transpose(0, 2, 1)
    o_c, S_new = gdn_chunked(q, k, v, beta, lg, s['gdn'].astype(f32))
    o_c = merge_heads(rms_norm(o_c, p['gdn_norm'])) * jax.nn.silu(cz)
    P = s['k_cache'].shape[1]
    k_new = dk.reshape(B, T, H_D, 2 * D_HD)
    v_new = dv.reshape(B, T, H_D, DV_D)
    kd = jnp.concatenate([s['k_cache'].astype(f32), k_new], axis=1).reshape(B, P + T, H_D, 2, D_HD)
    vd = jnp.concatenate([s['v_cache'].astype(f32), v_new], axis=1)
    lam_init = 0.8 - 0.6 * math.exp(-0.3 * layer)
    lp = p['lam'].astype(f32)
    lam = jnp.exp(jnp.sum(lp[0] * lp[1])) - jnp.exp(jnp.sum(lp[2] * lp[3])) + lam_init
    o_d = diff_attention(dq.reshape(B, T, H_D, 2, D_HD), kd, vd,
                         P + jnp.arange(T, dtype=jnp.int32), jnp.arange(P + T, dtype=jnp.int32), lam)
    o_d = (rms_norm(o_d, p['diff_norm']) * (1.0 - lam_init)).reshape(B, T, H_D * DV_D)
    out = jnp.concatenate([o_c, o_d], axis=-1).astype(h.dtype) @ p['w_out']
    return out, (S_new, conv_new, k_new, v_new)


def conv_ffn(h, buf, p):
    up = h @ p['ffn_up']
    a, u = jnp.split(up, 2, axis=-1)
    a_c, buf_new = causal_dwconv(a, buf, p['ffn_conv_w'])
    y = (jax.nn.silu(a_c + p['ffn_conv_b']) * u) @ p['ffn_down']
    return y, buf_new


def run_group(x, st, params, lower_bounds, norm_final):
    new_states = []
    for layer in range(DEPTH):
        p = params[layer]
        s = st[layer]
        h = rms_norm(x, p['norm_mix'])
        if layer % 2 == 0:
            mix, new_mix = mixer_ab(h, s, p, lower_bounds, layer)
        else:
            mix, new_mix = mixer_cd(h, s, p, layer)
        x = x + mix.astype(x.dtype)
        f, new_ffn = conv_ffn(rms_norm(x, p['norm_ffn']), s['ffn_conv'], p)
        x = x + f.astype(x.dtype)
        new_states.append(new_mix + (new_ffn,))
    return rms_norm(x, norm_final), new_states


def setup_inputs(seed: int = 0) -> dict:
    key = jax.random.key(seed)
    ks = iter(jax.random.split(key, 64))
    f32 = jnp.float32

    def nrm(shape, scale=1.0):
        return scale * jax.random.normal(next(ks), shape, f32)

    def gain(n):
        return 1.0 + nrm((n,), 0.01)

    a_log = jnp.log(jax.random.uniform(next(ks), (H_C,), f32, 1.0, 16.0))
    dt = jnp.exp(jax.random.uniform(next(ks), (H_C,), f32, math.log(1e-3), math.log(1e-1)))
    dt_bias = dt + jnp.log(-jnp.expm1(-dt))
    return {
        'x_prompt': nrm((BATCH, SEQ, D_MODEL)),
        'x_sample': nrm((DEC_BATCH, DEC_SEQ, D_MODEL)),
        'state_hgrn_0': nrm((DEC_BATCH, H_A, DK_A, DV_A), 0.5),
        'state_mlstm_c_0': nrm((DEC_BATCH, H_B, DK_B, DV_B), 0.5),
        'state_mlstm_n_0': nrm((DEC_BATCH, H_B, DK_B), 0.5),
        'state_mlstm_m_0': nrm((DEC_BATCH, H_B)),
        'state_ffn_conv_0': nrm((DEC_BATCH, CONV_FFN - 1, D_FF)),
        'state_gdn_1': nrm((DEC_BATCH, H_C, DK_C, DV_C), 0.5),
        'state_gdn_conv_1': nrm((DEC_BATCH, CONV_C - 1, CONV_DIM_C)),
        'cache_k_1': nrm((DEC_BATCH, PAST_LEN, H_D, 2 * D_HD)),
        'cache_v_1': nrm((DEC_BATCH, PAST_LEN, H_D, DV_D)),
        'state_ffn_conv_1': nrm((DEC_BATCH, CONV_FFN - 1, D_FF)),
        'hgrn_lower_bounds': 1.0 + nrm((N_HGRN_LAYERS + 1, H_A * DK_A), 0.1),
        'norm_mix_0': gain(D_MODEL),
        'w_in_0': nrm((D_MODEL, IN_AB), D_MODEL ** -0.5),
        'mlstm_i_bias_0': nrm((H_B,), 0.1),
        'mlstm_f_bias_0': jnp.linspace(3.0, 6.0, H_B, dtype=f32) + nrm((H_B,), 0.01),
        'hgrn_norm_0': gain(DV_A),
        'mlstm_norm_0': gain(DV_B),
        'w_out_0': nrm((OUT_AB, D_MODEL), OUT_AB ** -0.5),
        'norm_ffn_0': gain(D_MODEL),
        'ffn_up_0': nrm((D_MODEL, 2 * D_FF), D_MODEL ** -0.5),
        'ffn_conv_w_0': nrm((CONV_FFN, D_FF), CONV_FFN ** -0.5),
        'ffn_conv_b_0': nrm((D_FF,), 0.01),
        'ffn_down_0': nrm((D_FF, D_MODEL), D_FF ** -0.5),
        'norm_mix_1': gain(D_MODEL),
        'w_in_1': nrm((D_MODEL, IN_CD), D_MODEL ** -0.5),
        'gdn_conv_w_1': nrm((CONV_C, CONV_DIM_C), CONV_C ** -0.5),
        'gdn_a_log_1': a_log,
        'gdn_dt_bias_1': dt_bias,
        'gdn_norm_1': gain(DV_C),
        'diff_lambda_1': nrm((4, D_HD), 0.1),
        'diff_norm_1': gain(DV_D),
        'w_out_1': nrm((OUT_CD, D_MODEL), OUT_CD ** -0.5),
        'norm_ffn_1': gain(D_MODEL),
        'ffn_up_1': nrm((D_MODEL, 2 * D_FF), D_MODEL ** -0.5),
        'ffn_conv_w_1': nrm((CONV_FFN, D_FF), CONV_FFN ** -0.5),
        'ffn_conv_b_1': nrm((D_FF,), 0.01),
        'ffn_down_1': nrm((D_FF, D_MODEL), D_FF ** -0.5),
        'norm_final': gain(D_MODEL),
    }


def reference(x_prompt, x_sample, state_hgrn_0, state_mlstm_c_0, state_mlstm_n_0, state_mlstm_m_0,
              state_ffn_conv_0, state_gdn_1, state_gdn_conv_1, cache_k_1, cache_v_1, state_ffn_conv_1,
              hgrn_lower_bounds, norm_mix_0, w_in_0, mlstm_i_bias_0, mlstm_f_bias_0, hgrn_norm_0,
              mlstm_norm_0, w_out_0, norm_ffn_0, ffn_up_0, ffn_conv_w_0, ffn_conv_b_0, ffn_down_0,
              norm_mix_1, w_in_1, gdn_conv_w_1, gdn_a_log_1, gdn_dt_bias_1, gdn_norm_1, diff_lambda_1,
              diff_norm_1, w_out_1, norm_ffn_1, ffn_up_1, ffn_conv_w_1, ffn_conv_b_1, ffn_down_1,
              norm_final):
    f32 = jnp.float32
    params = [
        dict(norm_mix=norm_mix_0, w_in=w_in_0, i_bias=mlstm_i_bias_0, f_bias=mlstm_f_bias_0,
             hgrn_norm=hgrn_norm_0, mlstm_norm=mlstm_norm_0, w_out=w_out_0, norm_ffn=norm_ffn_0,
             ffn_up=ffn_up_0, ffn_conv_w=ffn_conv_w_0, ffn_conv_b=ffn_conv_b_0, ffn_down=ffn_down_0),
        dict(norm_mix=norm_mix_1, w_in=w_in_1, conv_w=gdn_conv_w_1, a_log=gdn_a_log_1,
             dt_bias=gdn_dt_bias_1, gdn_norm=gdn_norm_1, lam=diff_lambda_1, diff_norm=diff_norm_1,
             w_out=w_out_1, norm_ffn=norm_ffn_1, ffn_up=ffn_up_1, ffn_conv_w=ffn_conv_w_1,
             ffn_conv_b=ffn_conv_b_1, ffn_down=ffn_down_1),
    ]
    Bp = x_prompt.shape[0]
    st_prompt = [
        dict(hgrn=jnp.zeros((Bp, H_A, DK_A, DV_A), f32), mlstm_c=jnp.zeros((Bp, H_B, DK_B, DV_B), f32),
             mlstm_n=jnp.zeros((Bp, H_B, DK_B), f32), mlstm_m=jnp.full((Bp, H_B), NEG_BIG, f32),
             ffn_conv=jnp.zeros((Bp, CONV_FFN - 1, D_FF), x_prompt.dtype)),
        dict(gdn=jnp.zeros((Bp, H_C, DK_C, DV_C), f32), gdn_conv=jnp.zeros((Bp, CONV_C - 1, CONV_DIM_C), f32),
             k_cache=jnp.zeros((Bp, 0, H_D, 2 * D_HD), f32), v_cache=jnp.zeros((Bp, 0, H_D, DV_D), f32),
             ffn_conv=jnp.zeros((Bp, CONV_FFN - 1, D_FF), x_prompt.dtype)),
    ]
    st_sample = [
        dict(hgrn=state_hgrn_0, mlstm_c=state_mlstm_c_0, mlstm_n=state_mlstm_n_0, mlstm_m=state_mlstm_m_0,
             ffn_conv=state_ffn_conv_0),
        dict(gdn=state_gdn_1, gdn_conv=state_gdn_conv_1, k_cache=cache_k_1, v_cache=cache_v_1,
             ffn_conv=state_ffn_conv_1),
    ]
    y_prompt, new_p = run_group(x_prompt, st_prompt, params, hgrn_lower_bounds, norm_final)
    y_sample, new_s = run_group(x_sample, st_sample, params, hgrn_lower_bounds, norm_final)
    (hgrn_p, mlstm_c_p, mlstm_n_p, mlstm_m_p, ffn_conv0_p), (gdn_p, gdn_conv_p, k_new_p, v_new_p, ffn_conv1_p) = new_p
    (hgrn_s, mlstm_c_s, mlstm_n_s, mlstm_m_s, ffn_conv0_s), (gdn_s, gdn_conv_s, k_new_s, v_new_s, ffn_conv1_s) = new_s
    return (y_prompt, y_sample, hgrn_p, hgrn_s, mlstm_c_p, mlstm_c_s, mlstm_n_p, mlstm_n_s,
            mlstm_m_p, mlstm_m_s, ffn_conv0_p, ffn_conv0_s, gdn_p, gdn_s, gdn_conv_p, gdn_conv_s,
            k_new_p, k_new_s, v_new_p, v_new_s, ffn_conv1_p, ffn_conv1_s)
```

```python
import functools
import math

import jax
import jax.numpy as jnp
from jax import lax
from jax.experimental import pallas as pl
from jax.experimental.pallas import tpu as pltpu

F32 = jnp.float32
BF16 = jnp.bfloat16
HI = lax.Precision.HIGHEST

D_MODEL = 2048
CHUNK = 64
H_A, DK_A, DV_A = 8, 128, 128
H_B, DK_B, DV_B = 4, 128, 256
H_C, DK_C, DV_C = 8, 128, 128
CONV_C = 4
H_D, D_HD = 8, 64
DV_D = 2 * D_HD
D_FF = 5632
CONV_FFN = 3
EPS = 1e-6
NEG_BIG = -1e30

LANES = 128
SUBLANES = 8
VMEM_LIMIT = 56 * 1024 * 1024

AB_Q, AB_F, AB_I, AB_G = 0, 8, 16, 24
AB_BQ, AB_BK = 32, 36
AB_BV, AB_BO = 20, 24
AB_GATES = 56
IN_AB_PAD = 57 * LANES
CD_Q, CD_K, CD_V, CD_Z = 0, 8, 16, 24
CD_DQ, CD_DK, CD_DV = 32, 40, 48
CD_GATES = 56
IN_CD_PAD = 57 * LANES


def _mm(a, b, prec=None):
    return lax.dot_general(a, b, (((1,), (0,)), ((), ())), precision=prec,
                           preferred_element_type=F32)


def _mm_nt(a, b, prec=None):
    return lax.dot_general(a, b, (((1,), (1,)), ((), ())), precision=prec,
                           preferred_element_type=F32)


def _mm_tn(a, b, prec=None):
    return lax.dot_general(a, b, (((0,), (0,)), ((), ())), precision=prec,
                           preferred_element_type=F32)


def _sigmoid(x):
    return 1.0 / (1.0 + jnp.exp(-x))


def _silu(x):
    return x * _sigmoid(x)


def _softplus(x):
    return jnp.maximum(x, 0.0) + jnp.log1p(jnp.exp(-jnp.abs(x)))


def _rms(x, g):
    return x * lax.rsqrt(jnp.mean(x * x, axis=-1, keepdims=True) + EPS) * g


def _params(*sem):
    return pltpu.CompilerParams(dimension_semantics=sem, vmem_limit_bytes=VMEM_LIMIT)


def _norm_matmul_kernel(x_ref, g_ref, w_ref, o_ref, h_ref):
    @pl.when(pl.program_id(1) == 0)
    def _():
        h_ref[...] = _rms(x_ref[...], g_ref[...]).astype(BF16)

    o_ref[...] = _mm(h_ref[...], w_ref[...])


def norm_matmul(x, g, w, tm, tn):
    n, k = x.shape
    m = w.shape[1]
    return pl.pallas_call(
        _norm_matmul_kernel,
        grid=(n // tm, m // tn),
        in_specs=[pl.BlockSpec((tm, k), lambda i, j: (i, 0)),
                  pl.BlockSpec((1, k), lambda i, j: (0, 0)),
                  pl.BlockSpec((k, tn), lambda i, j: (0, j))],
        out_specs=pl.BlockSpec((tm, tn), lambda i, j: (i, j)),
        out_shape=jax.ShapeDtypeStruct((n, m), F32),
        scratch_shapes=[pltpu.VMEM((tm, k), BF16)],
        compiler_params=_params("arbitrary", "arbitrary"),
        name="norm_matmul",
    )(x, g.reshape(1, k), w)


def _matmul_res_kernel(y_ref, w_ref, x_ref, g_ref, o_ref, *, nk, final_norm):
    kk = pl.program_id(1)
    d = _mm(y_ref[...], w_ref[...])

    @pl.when(kk == 0)
    def _():
        o_ref[...] = x_ref[...] + d

    @pl.when(kk != 0)
    def _():
        o_ref[...] += d

    if final_norm:
        @pl.when(kk == nk - 1)
        def _():
            o_ref[...] = _rms(o_ref[...], g_ref[...])


def matmul_residual(y, w, x, g, tm, tk, final_norm):
    n, k = y.shape
    d = w.shape[1]
    nk = k // tk
    return pl.pallas_call(
        functools.partial(_matmul_res_kernel, nk=nk, final_norm=final_norm),
        grid=(n // tm, nk),
        in_specs=[pl.BlockSpec((tm, tk), lambda i, kk: (i, kk)),
                  pl.BlockSpec((tk, d), lambda i, kk: (kk, 0)),
                  pl.BlockSpec((tm, d), lambda i, kk: (i, 0)),
                  pl.BlockSpec((1, d), lambda i, kk: (0, 0))],
        out_specs=pl.BlockSpec((tm, d), lambda i, kk: (i, 0)),
        out_shape=jax.ShapeDtypeStruct((n, d), F32),
        compiler_params=_params("arbitrary", "arbitrary"),
        name="matmul_residual",
    )(y, w, x, g.reshape(1, d))


def _ffn_up_kernel(x_ref, g_ref, wa_ref, wu_ref, cw_ref, cb_ref, st_ref, act_ref, tail_ref,
                   h_ref, ext_ref, carry_ref, *, nseq, rows, tiles_per_seq):
    i = pl.program_id(0)
    j = pl.program_id(1)
    halo = CONV_FFN - 1

    @pl.when(j == 0)
    def _():
        h_ref[...] = _rms(x_ref[...], g_ref[...]).astype(BF16)

    if tiles_per_seq > 1:
        @pl.when(i == 0)
        def _():
            carry_ref[j] = jnp.zeros(carry_ref.shape[1:], F32)

    h = h_ref[...]
    a = _mm(h, wa_ref[...])
    u = _mm(h, wu_ref[...])
    cw = cw_ref[...]
    cb = cb_ref[...]
    for s in range(nseq):
        lo = s * rows
        ext_ref[s, SUBLANES:, :] = a[lo:lo + rows]
        prev = st_ref[s]
        if tiles_per_seq > 1:
            prev = jnp.where((i % tiles_per_seq) == 0, prev, carry_ref[j, SUBLANES - halo:, :])
        ext_ref[s, SUBLANES - halo:SUBLANES, :] = prev
        conv = cb
        for t in range(CONV_FFN):
            off = SUBLANES - halo + t
            conv = conv + ext_ref[s, off:off + rows, :] * cw[t:t + 1]
        act_ref[lo:lo + rows, :] = (_silu(conv) * u[lo:lo + rows]).astype(BF16)
        tail_ref[s] = ext_ref[s, SUBLANES + rows - halo:SUBLANES + rows, :]
        carry_ref[j] = ext_ref[s, rows:rows + SUBLANES, :]


def ffn_up(x, g, w_up, conv_w, conv_b, state, batch, seq, tm, tf):
    n, k = x.shape
    nff = D_FF // tf
    if tm >= seq:
        nseq, rows, tiles_per_seq = tm // seq, seq, 1
    else:
        nseq, rows, tiles_per_seq = 1, tm, seq // tm
    ntiles = n // tm
    halo = CONV_FFN - 1
    act, tail = pl.pallas_call(
        functools.partial(_ffn_up_kernel, nseq=nseq, rows=rows, tiles_per_seq=tiles_per_seq),
        grid=(ntiles, nff),
        in_specs=[pl.BlockSpec((tm, k), lambda i, j: (i, 0)),
                  pl.BlockSpec((1, k), lambda i, j: (0, 0)),
                  pl.BlockSpec((k, tf), lambda i, j: (0, j)),
                  pl.BlockSpec((k, tf), lambda i, j: (0, j + nff)),
                  pl.BlockSpec((CONV_FFN, tf), lambda i, j: (0, j)),
                  pl.BlockSpec((1, tf), lambda i, j: (0, j)),
                  pl.BlockSpec((nseq, halo, tf), lambda i, j: (i // tiles_per_seq, 0, j))],
        out_specs=[pl.BlockSpec((tm, tf), lambda i, j: (i, j)),
                   pl.BlockSpec((nseq, halo, tf), lambda i, j: (i, 0, j))],
        out_shape=[jax.ShapeDtypeStruct((n, D_FF), BF16),
                   jax.ShapeDtypeStruct((ntiles * nseq, halo, D_FF), F32)],
        scratch_shapes=[pltpu.VMEM((tm, k), BF16),
                        pltpu.VMEM((nseq, SUBLANES + rows, tf), F32),
                        pltpu.VMEM((nff, SUBLANES, tf), F32)],
        compiler_params=_params("arbitrary", "arbitrary"),
        name="ffn_up",
    )(x, g.reshape(1, k), w_up, w_up, conv_w, conv_b.reshape(1, D_FF), state)
    new_state = tail.reshape(batch, -1, halo, D_FF)[:, -1]
    return act, new_state


def _iota2(shape, dim):
    return lax.broadcasted_iota(jnp.int32, shape, dim)


def _tril(n):
    return (_iota2((n, n), 1) <= _iota2((n, n), 0)).astype(F32)


def _lane_col(x, lane):
    return jnp.sum(jnp.where(_iota2(x.shape, 1) == lane, x, 0.0), axis=-1, keepdims=True)


def _lane_row(x, lane):
    sel = (_iota2((SUBLANES, LANES), 1) == lane).astype(F32)
    return _mm_nt(sel, x, HI)[0:1]


def _hgrn_kernel(q_ref, f_ref, i_ref, g_ref, lb_ref, gn_ref, s0_ref, o_ref, s_out_ref, st_ref,
                 *, L, n_chunks, lb_index):
    tb = pl.program_id(2)

    @pl.when(tb == 0)
    def _():
        st_ref[...] = s0_ref[0, 0]

    lbp = lb_ref[...]
    e = jnp.exp(lbp - jnp.max(lbp, axis=0, keepdims=True))
    lb = jnp.sum(e[:lb_index + 1], axis=0, keepdims=True) / jnp.sum(e, axis=0, keepdims=True)

    sizes = [L >> l for l in range(int(math.log2(L)))]
    ti = _iota2((L, L), 0)
    si = _iota2((L, L), 1)
    tcol = _iota2((L, 1), 0)
    mats = [si <= ti]
    for sz in sizes:
        mats.append(si <= ((ti & ~(sz - 1)) + (sz // 2 - 1)))
    cum_mat = jnp.concatenate([m.astype(F32) for m in mats], axis=0)
    eye = ti == si
    gn = gn_ref[...]

    def chunk(c, carry):
        rows = pl.ds(pl.multiple_of(c * L, L), L)
        q = q_ref[rows, :]
        v = i_ref[rows, :]
        f = lb + (1.0 - lb) * _sigmoid(f_ref[rows, :])
        k = 1.0 - f
        cums = _mm(cum_mat, jnp.log(f), HI)
        b = cums[:L]
        st = st_ref[...]
        scores = jnp.where(eye, jnp.sum(q * k, axis=-1, keepdims=True), 0.0)
        for l, sz in enumerate(sizes):
            w = jnp.exp(-jnp.abs(b - cums[(l + 1) * L:(l + 2) * L]))
            right = (tcol & (sz - 1)) >= sz // 2
            qm = jnp.where(right, q * w, 0.0)
            km = jnp.where(right, 0.0, k * w)
            same = (ti & ~(sz - 1)) == (si & ~(sz - 1))
            scores = scores + jnp.where(same, _mm_nt(qm, km, HI), 0.0)
        o = _mm_nt(q * jnp.exp(b), st, HI) + _mm(scores, v, HI)
        bl = b[L - 1:L]
        st_ref[...] = jnp.exp(bl) * st + _mm_tn(v, k * jnp.exp(bl - b), HI)
        o_ref[rows, :] = (_rms(o, gn) * _silu(g_ref[rows, :])).astype(BF16)
        return carry

    lax.fori_loop(0, n_chunks, chunk, 0)

    @pl.when(tb == pl.num_programs(2) - 1)
    def _():
        s_out_ref[0, 0] = st_ref[...]


def hgrn2(z, lower_bounds, norm_g, s0, batch, seq, tb, lb_index):
    L = min(CHUNK, seq)
    nt = seq // tb

    def col(c0):
        return pl.BlockSpec((tb, LANES), lambda b, h, t: (b * nt + t, c0 + h))

    st_spec = pl.BlockSpec((1, 1, DV_A, DK_A), lambda b, h, t: (b, h, 0, 0))
    o, st = pl.pallas_call(
        functools.partial(_hgrn_kernel, L=L, n_chunks=tb // L, lb_index=lb_index),
        grid=(batch, H_A, nt),
        in_specs=[col(AB_Q), col(AB_F), col(AB_I), col(AB_G),
                  pl.BlockSpec((lower_bounds.shape[0], LANES), lambda b, h, t: (0, h)),
                  pl.BlockSpec((1, DV_A), lambda b, h, t: (0, 0)),
                  st_spec],
        out_specs=[pl.BlockSpec((tb, DV_A), lambda b, h, t: (b * nt + t, h)), st_spec],
        out_shape=[jax.ShapeDtypeStruct((batch * seq, H_A * DV_A), BF16),
                   jax.ShapeDtypeStruct((batch, H_A, DV_A, DK_A), F32)],
        scratch_shapes=[pltpu.VMEM((DV_A, DK_A), F32)],
        compiler_params=_params("arbitrary", "arbitrary", "arbitrary"),
        name="hgrn2",
    )(z, z, z, z, lower_bounds, norm_g.reshape(1, DV_A), jnp.swapaxes(s0, 2, 3))
    return o, jnp.swapaxes(st, 2, 3)


def _mlstm_kernel(q_ref, k_ref, v_ref, og_ref, gt_ref, bias_ref, gn_ref, c0_ref, n0_ref, m0_ref,
                  o_ref, c_out_ref, n_out_ref, m_out_ref, ct_ref, n_ref, m_ref, *, L, n_chunks):
    h = pl.program_id(1)
    tb = pl.program_id(2)

    @pl.when(tb == 0)
    def _():
        ct_ref[...] = c0_ref[0, 0]
        n_ref[...] = n0_ref[0, 0]
        m_ref[...] = m0_ref[0, 0]

    tril = _tril(L)
    causal = _iota2((L, L), 1) <= _iota2((L, L), 0)
    bias = bias_ref[...]
    gn = gn_ref[...]
    scale = DK_B ** -0.5

    def chunk(c, carry):
        rows = pl.ds(pl.multiple_of(c * L, L), L)
        q = q_ref[rows, :] * scale
        k = k_ref[rows, :]
        v = v_ref[rows, :]
        gates = gt_ref[rows, :] + bias
        logf = jnp.minimum(gates, 0.0) - jnp.log1p(jnp.exp(-jnp.abs(gates)))
        cum = _mm(tril, logf, HI)
        ig_col = _lane_col(gates, h)
        ig_row = _lane_row(gates, h)
        b_col = _lane_col(cum, H_B + h)
        b_row = _lane_row(cum, H_B + h)
        ct = ct_ref[...]
        n = n_ref[...]
        m_prev = m_ref[...][:, 0:1]
        d = b_col - b_row + ig_row
        inter = b_col + m_prev
        m_t = jnp.maximum(jnp.max(jnp.where(causal, d, -jnp.inf), axis=-1, keepdims=True), inter)
        p = jnp.exp(jnp.where(causal, d - m_t, -jnp.inf)) * _mm_nt(q, k, HI)
        w_inter = jnp.exp(inter - m_t)
        num = _mm(p, v, HI) + w_inter * _mm_nt(q, ct, HI)
        den = (jnp.sum(p, axis=-1, keepdims=True)
               + w_inter * jnp.sum(q * n, axis=-1, keepdims=True))
        hid = num / jnp.maximum(jnp.abs(den), jnp.exp(-m_t))
        b_last = b_col[L - 1:L]
        wk = b_last - b_col + ig_col
        m_new = jnp.maximum(b_last + m_prev, jnp.max(wk, axis=0, keepdims=True))
        sc = jnp.exp(wk - m_new)
        carry_w = jnp.exp(b_last + m_prev - m_new)
        ks = sc * k
        ct_ref[...] = carry_w * ct + _mm_tn(v, ks, HI)
        n_ref[...] = carry_w * n + jnp.sum(ks, axis=0, keepdims=True)
        m_ref[...] = jnp.broadcast_to(m_new, (1, LANES))
        o_ref[rows, :] = (_rms(hid, gn) * _sigmoid(og_ref[rows, :])).astype(BF16)
        return carry

    lax.fori_loop(0, n_chunks, chunk, 0)

    @pl.when(tb == pl.num_programs(2) - 1)
    def _():
        c_out_ref[0, 0] = ct_ref[...]
        n_out_ref[0, 0] = n_ref[...]
        m_out_ref[0, 0] = m_ref[...]


def mlstm(z, i_bias, f_bias, norm_g, c0, n0, m0, batch, seq, tb):
    L = min(CHUNK, seq)
    nt = seq // tb
    bias = jnp.zeros((1, LANES), F32).at[0, :H_B].set(i_bias).at[0, H_B:2 * H_B].set(f_bias)

    def col(c0_, width):
        return pl.BlockSpec((tb, width), lambda b, h, t: (b * nt + t, c0_ + h))

    def state(r, c):
        return pl.BlockSpec((1, 1, r, c), lambda b, h, t: (b, h, 0, 0))

    o, ct, n, m = pl.pallas_call(
        functools.partial(_mlstm_kernel, L=L, n_chunks=tb // L),
        grid=(batch, H_B, nt),
        in_specs=[col(AB_BQ, DK_B), col(AB_BK, DK_B), col(AB_BV, DV_B), col(AB_BO, DV_B),
                  pl.BlockSpec((tb, LANES), lambda b, h, t: (b * nt + t, AB_GATES)),
                  pl.BlockSpec((1, LANES), lambda b, h, t: (0, 0)),
                  pl.BlockSpec((1, DV_B), lambda b, h, t: (0, 0)),
                  state(DV_B, DK_B), state(1, DK_B), state(1, LANES)],
        out_specs=[pl.BlockSpec((tb, DV_B), lambda b, h, t: (b * nt + t, h)),
                   state(DV_B, DK_B), state(1, DK_B), state(1, LANES)],
        out_shape=[jax.ShapeDtypeStruct((batch * seq, H_B * DV_B), BF16),
                   jax.ShapeDtypeStruct((batch, H_B, DV_B, DK_B), F32),
                   jax.ShapeDtypeStruct((batch, H_B, 1, DK_B), F32),
                   jax.ShapeDtypeStruct((batch, H_B, 1, LANES), F32)],
        scratch_shapes=[pltpu.VMEM((DV_B, DK_B), F32), pltpu.VMEM((1, DK_B), F32),
                        pltpu.VMEM((1, LANES), F32)],
        compiler_params=_params("arbitrary", "arbitrary", "arbitrary"),
        name="mlstm",
    )(z, z, z, z, z, bias, norm_g.reshape(1, DV_B), jnp.swapaxes(c0, 2, 3),
      n0.reshape(batch, H_B, 1, DK_B),
      jnp.broadcast_to(m0.reshape(batch, H_B, 1, 1), (batch, H_B, 1, LANES)))
    return o, jnp.swapaxes(ct, 2, 3), n.reshape(batch, H_B, DK_B), m[:, :, 0, 0]


def _gdn_kernel(q_ref, k_ref, v_ref, z_ref, gt_ref, cw_ref, cs_ref, gp_ref, gn_ref, s0_ref,
                o_ref, s_out_ref, st_ref, ext_ref, cv_ref, *, L, n_chunks, tb):
    h = pl.program_id(1)
    t = pl.program_id(2)
    halo = CONV_C - 1

    @pl.when(t == 0)
    def _():
        st_ref[...] = s0_ref[0, 0]
        for p in range(3):
            ext_ref[p, SUBLANES - halo:SUBLANES, :] = cs_ref[p, 0]

    srcs = (q_ref, k_ref, v_ref)
    for p in range(3):
        ext_ref[p, SUBLANES:, :] = srcs[p][...]
        cw = cw_ref[p]
        acc = None
        for j in range(CONV_C):
            off = SUBLANES - halo + j
            term = ext_ref[p, off:off + tb, :] * cw[j:j + 1]
            acc = term if acc is None else acc + term
        cv_ref[p] = _silu(acc)
    tril = _tril(L)
    ti = _iota2((L, L), 0)
    si = _iota2((L, L), 1)
    gp = gp_ref[...]
    gn = gn_ref[...]

    def l2n(x):
        return x * lax.rsqrt(jnp.sum(x * x, axis=-1, keepdims=True) + EPS)

    def chunk(c, carry):
        rows = pl.ds(pl.multiple_of(c * L, L), L)
        q = l2n(cv_ref[0, rows, :]) * (DK_C ** -0.5)
        k = l2n(cv_ref[1, rows, :])
        v = cv_ref[2, rows, :]
        gates = gt_ref[rows, :]
        beta = _lane_col(_sigmoid(gates), h)
        logg = -jnp.exp(gp[0:1]) * _softplus(gates + gp[1:2])
        cum = _mm(tril, logg, HI)
        b_col = _lane_col(cum, H_C + h)
        b_row = _lane_row(cum, H_C + h)
        rel = b_col - b_row
        dec_incl = jnp.exp(jnp.where(si <= ti, rel, -jnp.inf))
        dec_strict = jnp.where(si < ti, dec_incl, 0.0)
        st = st_ref[...]
        eb = jnp.exp(b_col)
        neg_m = -(beta * _mm_nt(k, k, HI) * dec_strict)
        delta = beta * (v - eb * _mm_nt(k, st, HI))
        delta = delta + _mm(neg_m, delta, HI)
        pw = neg_m
        for _ in range(int(math.log2(L)) - 1):
            pw = _mm(pw, pw, HI)
            delta = delta + _mm(pw, delta, HI)
        o = eb * _mm_nt(q, st, HI) + _mm(_mm_nt(q, k, HI) * dec_incl, delta, HI)
        b_last = b_col[L - 1:L]
        st_ref[...] = jnp.exp(b_last) * st + _mm_tn(delta, k * jnp.exp(b_last - b_col), HI)
        o_ref[rows, :] = (_rms(o, gn) * _silu(z_ref[rows, :])).astype(BF16)
        return carry

    lax.fori_loop(0, n_chunks, chunk, 0)
    for p in range(3):
        ext_ref[p, 0:SUBLANES, :] = ext_ref[p, tb:tb + SUBLANES, :]

    @pl.when(t == pl.num_programs(2) - 1)
    def _():
        s_out_ref[0, 0] = st_ref[...]


def gdn(z, conv_w, conv_state, a_log, dt_bias, norm_g, s0, batch, seq, tb):
    L = min(CHUNK, seq)
    nt = seq // tb
    halo = CONV_C - 1
    gate_params = (jnp.zeros((2, LANES), F32).at[0, H_C:2 * H_C].set(a_log)
                   .at[1, H_C:2 * H_C].set(dt_bias))
    cw = conv_w.reshape(CONV_C, 3, H_C * DK_C).transpose(1, 0, 2)
    cs = conv_state.reshape(batch, halo, 3, H_C * DK_C).transpose(2, 0, 1, 3)

    def col(c0):
        return pl.BlockSpec((tb, LANES), lambda b, h, t: (b * nt + t, c0 + h))

    st_spec = pl.BlockSpec((1, 1, DV_C, DK_C), lambda b, h, t: (b, h, 0, 0))
    o, st = pl.pallas_call(
        functools.partial(_gdn_kernel, L=L, n_chunks=tb // L, tb=tb),
        grid=(batch, H_C, nt),
        in_specs=[col(CD_Q), col(CD_K), col(CD_V), col(CD_Z),
                  pl.BlockSpec((tb, LANES), lambda b, h, t: (b * nt + t, CD_GATES)),
                  pl.BlockSpec((3, CONV_C, LANES), lambda b, h, t: (0, 0, h)),
                  pl.BlockSpec((3, 1, halo, LANES), lambda b, h, t: (0, b, 0, h)),
                  pl.BlockSpec((2, LANES), lambda b, h, t: (0, 0)),
                  pl.BlockSpec((1, DV_C), lambda b, h, t: (0, 0)),
                  st_spec],
        out_specs=[pl.BlockSpec((tb, DV_C), lambda b, h, t: (b * nt + t, h)), st_spec],
        out_shape=[jax.ShapeDtypeStruct((batch * seq, H_C * DV_C), BF16),
                   jax.ShapeDtypeStruct((batch, H_C, DV_C, DK_C), F32)],
        scratch_shapes=[pltpu.VMEM((DV_C, DK_C), F32),
                        pltpu.VMEM((3, SUBLANES + tb, LANES), F32),
                        pltpu.VMEM((3, tb, LANES), F32)],
        compiler_params=_params("arbitrary", "arbitrary", "arbitrary"),
        name="gdn",
    )(z, z, z, z, z, cw, cs, gate_params, norm_g.reshape(1, DV_C), jnp.swapaxes(s0, 2, 3))
    return o, jnp.swapaxes(st, 2, 3)


def _attn_kernel(q_ref, k_ref, v_ref, lp_ref, gn_ref, o_ref, m_ref, l_ref, acc_ref,
                 *, seq, kv_len, tq, tk, lam_init):
    h = pl.program_id(1)
    qi = pl.program_id(2)
    past = kv_len - seq
    shift = int(math.log2(CHUNK))

    lp = lp_ref[...]
    lam = (jnp.exp(jnp.sum(lp[0:1] * lp[1:2], axis=-1, keepdims=True))
           - jnp.exp(jnp.sum(lp[2:3] * lp[3:4], axis=-1, keepdims=True)) + lam_init)
    hf = jnp.full((1, 1), h + 1, jnp.int32).astype(F32)
    slope = jnp.exp(hf * (-8.0 / H_D * math.log(2.0)))

    q = q_ref[...] * (D_HD ** -0.5)
    first = _iota2((tq, 2 * D_HD), 1) < D_HD
    q_maps = (jnp.where(first, q, 0.0), jnp.where(first, 0.0, q))
    q_pos = past + qi * tq + _iota2((tq, 1), 0)
    q_chunk = q_pos >> shift

    m_ref[...] = jnp.full(m_ref.shape, NEG_BIG, F32)
    l_ref[...] = jnp.zeros(l_ref.shape, F32)
    acc_ref[...] = jnp.zeros(acc_ref.shape, F32)

    last_allowed = (((past + qi * tq + tq - 1) >> shift) << shift) + (CHUNK - 1)
    n_blocks = jnp.minimum(last_allowed, kv_len - 1) // tk + 1

    def block(j, carry):
        rows = pl.ds(pl.multiple_of(j * tk, SUBLANES), tk)
        ks = k_ref[rows, :]
        vs = v_ref[rows, :].astype(BF16)
        k_pos = j * tk + _iota2((1, tk), 1)
        allowed = (k_pos >> shift) <= q_chunk
        bias = -slope * jnp.abs(q_pos - k_pos).astype(F32)
        for c in range(2):
            s = jnp.where(allowed, _mm_nt(q_maps[c], ks) + bias, NEG_BIG)
            m_old = m_ref[c]
            m_new = jnp.maximum(m_old, jnp.max(s, axis=-1, keepdims=True))
            p = jnp.where(allowed, jnp.exp(s - m_new), 0.0)
            alpha = jnp.exp(m_old - m_new)
            l_ref[c] = alpha * l_ref[c] + jnp.sum(p, axis=-1, keepdims=True)
            acc_ref[c] = alpha * acc_ref[c] + _mm(p.astype(BF16), vs)
            m_ref[c] = m_new
        return carry

    lax.fori_loop(0, n_blocks, block, 0)
    o = acc_ref[0] / l_ref[0] - lam * (acc_ref[1] / l_ref[1])
    o_ref[...] = (_rms(o, gn_ref[...]) * (1.0 - lam_init)).astype(BF16)


def diff_attention(zq, q_col, kd, k_col, vd, v_col, lam_params, norm_g, batch, seq, kv_len,
                   tq, tk, lam_init):
    nq = seq // tq
    return pl.pallas_call(
        functools.partial(_attn_kernel, seq=seq, kv_len=kv_len, tq=tq, tk=tk, lam_init=lam_init),
        grid=(batch, H_D, nq),
        in_specs=[pl.BlockSpec((tq, DV_D), lambda b, h, i: (b * nq + i, q_col + h)),
                  pl.BlockSpec((kv_len, DV_D), lambda b, h, i: (b, k_col + h)),
                  pl.BlockSpec((kv_len, DV_D), lambda b, h, i: (b, v_col + h)),
                  pl.BlockSpec((4, D_HD), lambda b, h, i: (0, 0)),
                  pl.BlockSpec((1, DV_D), lambda b, h, i: (0, 0))],
        out_specs=pl.BlockSpec((tq, DV_D), lambda b, h, i: (b * nq + i, h)),
        out_shape=jax.ShapeDtypeStruct((batch * seq, H_D * DV_D), BF16),
        scratch_shapes=[pltpu.VMEM((2, tq, 1), F32), pltpu.VMEM((2, tq, 1), F32),
                        pltpu.VMEM((2, tq, DV_D), F32)],
        compiler_params=_params("arbitrary", "arbitrary", "arbitrary"),
        name="diff_attention",
    )(zq, kd, vd, lam_params, norm_g.reshape(1, DV_D))


def _pad_cols(w, total):
    return jnp.pad(w, ((0, 0), (0, total - w.shape[1])))


def _prep_w_in_ab(w):
    sizes = (H_A * DK_A, H_A * DK_A, H_A * DV_A, H_A * DV_A, H_B * DK_B, H_B * DK_B, H_B * DV_B,
             H_B, H_B, H_B * DV_B)
    aq, af, ai, ag, bq, bk, bv, big, bfg, bo = jnp.split(w, np_cumsum(sizes), axis=1)
    return _pad_cols(jnp.concatenate([aq, af, ai, ag, bq, bk, bv, bo, big, bfg], axis=1),
                     IN_AB_PAD).astype(BF16)


def _prep_w_in_cd(w):
    sizes = (H_C * (2 * DK_C + DV_C), H_C, H_C, H_C * DV_C, H_D * 2 * D_HD, H_D * 2 * D_HD,
             H_D * DV_D)
    cqkv, cb, ca, cz, dq, dk, dv = jnp.split(w, np_cumsum(sizes), axis=1)
    return _pad_cols(jnp.concatenate([cqkv, cz, dq, dk, dv, cb, ca], axis=1),
                     IN_CD_PAD).astype(BF16)


def np_cumsum(sizes):
    out, acc = [], 0
    for s in sizes[:-1]:
        acc += s
        out.append(acc)
    return out


def _tiles(n_rows, seq):
    tm = min(512, n_rows)
    tb = min(512, seq)
    return tm, tb


def _layer_ab(x, st, p, lower_bounds, batch, seq, layer):
    n = batch * seq
    tm, tb = _tiles(n, seq)
    z = norm_matmul(x, p['norm_mix'], p['w_in'], tm, IN_AB_PAD // 3)
    o_a, s_new = hgrn2(z, lower_bounds, p['hgrn_norm'], st['hgrn'], batch, seq, tb, layer // 2)
    o_b, c_new, n_new, m_new = mlstm(z, p['i_bias'], p['f_bias'], p['mlstm_norm'], st['mlstm_c'],
                                     st['mlstm_n'], st['mlstm_m'], batch, seq, tb)
    y = jnp.concatenate([o_a, o_b], axis=1)
    x = matmul_residual(y, p['w_out'], x, p['norm_ffn'], tm, 1024, False)
    return x, (s_new, c_new, n_new, m_new)


def _layer_cd(x, st, p, batch, seq, layer):
    n = batch * seq
    tm, tb = _tiles(n, seq)
    z = norm_matmul(x, p['norm_mix'], p['w_in'], tm, IN_CD_PAD // 3)
    o_c, s_new = gdn(z, p['conv_w'], st['gdn_conv'], p['a_log'], p['dt_bias'], p['gdn_norm'],
                     st['gdn'], batch, seq, tb)
    z3 = z.reshape(batch, seq, IN_CD_PAD)
    conv_new = z3[:, seq - (CONV_C - 1):, :H_C * (2 * DK_C + DV_C)]
    k_new = z3[:, :, CD_DK * LANES:CD_DV * LANES]
    v_new = z3[:, :, CD_DV * LANES:CD_GATES * LANES]
    past = st['k_cache'].shape[1]
    lam_init = 0.8 - 0.6 * math.exp(-0.3 * layer)
    if past == 0:
        o_d = diff_attention(z, CD_DQ, z, CD_DK, z, CD_DV, p['lam'], p['diff_norm'], batch, seq,
                             seq, min(512, seq), min(512, seq), lam_init)
    else:
        kv_len = past + seq
        kd = jnp.concatenate([st['k_cache'].reshape(batch, past, -1), k_new], axis=1)
        vd = jnp.concatenate([st['v_cache'].reshape(batch, past, -1), v_new], axis=1)
        o_d = diff_attention(z, CD_DQ, kd.reshape(batch * kv_len, -1), 0,
                             vd.reshape(batch * kv_len, -1), 0, p['lam'], p['diff_norm'], batch,
                             seq, kv_len, seq, kv_len, lam_init)
    y = jnp.concatenate([o_c, o_d], axis=1)
    x = matmul_residual(y, p['w_out'], x, p['norm_ffn'], tm, 1024, False)
    return x, (s_new, conv_new, k_new.reshape(batch, seq, H_D, 2 * D_HD),
               v_new.reshape(batch, seq, H_D, DV_D))


def _ffn(x, state, p, batch, seq, final_g):
    n = batch * seq
    tm, _ = _tiles(n, seq)
    act, conv_new = ffn_up(x, p['norm_ffn'], p['ffn_up'], p['ffn_conv_w'], p['ffn_conv_b'], state,
                           batch, seq, tm, 512)
    g = p['norm_ffn'] if final_g is None else final_g
    x = matmul_residual(act, p['ffn_down'], x, g, tm, D_FF // 4, final_g is not None)
    return x, conv_new


def _run_group(x, st, params, lower_bounds, norm_final):
    batch, seq, _ = x.shape
    x = x.reshape(batch * seq, D_MODEL)
    new_states = []
    for layer, (p, s) in enumerate(zip(params, st)):
        if layer % 2 == 0:
            x, new_mix = _layer_ab(x, s, p, lower_bounds, batch, seq, layer)
        else:
            x, new_mix = _layer_cd(x, s, p, batch, seq, layer)
        last = layer == len(params) - 1
        x, new_ffn = _ffn(x, s['ffn_conv'], p, batch, seq, norm_final if last else None)
        new_states.append(new_mix + (new_ffn,))
    return x.reshape(batch, seq, D_MODEL), new_states


def kernel(x_prompt, x_sample, state_hgrn_0, state_mlstm_c_0, state_mlstm_n_0, state_mlstm_m_0,
           state_ffn_conv_0, state_gdn_1, state_gdn_conv_1, cache_k_1, cache_v_1, state_ffn_conv_1,
           hgrn_lower_bounds, norm_mix_0, w_in_0, mlstm_i_bias_0, mlstm_f_bias_0, hgrn_norm_0,
           mlstm_norm_0, w_out_0, norm_ffn_0, ffn_up_0, ffn_conv_w_0, ffn_conv_b_0, ffn_down_0,
           norm_mix_1, w_in_1, gdn_conv_w_1, gdn_a_log_1, gdn_dt_bias_1, gdn_norm_1, diff_lambda_1,
           diff_norm_1, w_out_1, norm_ffn_1, ffn_up_1, ffn_conv_w_1, ffn_conv_b_1, ffn_down_1,
           norm_final):
    params = [
        dict(norm_mix=norm_mix_0, w_in=_prep_w_in_ab(w_in_0), i_bias=mlstm_i_bias_0,
             f_bias=mlstm_f_bias_0, hgrn_norm=hgrn_norm_0, mlstm_norm=mlstm_norm_0,
             w_out=w_out_0.astype(BF16), norm_ffn=norm_ffn_0, ffn_up=ffn_up_0.astype(BF16),
             ffn_conv_w=ffn_conv_w_0, ffn_conv_b=ffn_conv_b_0, ffn_down=ffn_down_0.astype(BF16)),
        dict(norm_mix=norm_mix_1, w_in=_prep_w_in_cd(w_in_1), conv_w=gdn_conv_w_1,
             a_log=gdn_a_log_1, dt_bias=gdn_dt_bias_1, gdn_norm=gdn_norm_1, lam=diff_lambda_1,
             diff_norm=diff_norm_1, w_out=w_out_1.astype(BF16), norm_ffn=norm_ffn_1,
             ffn_up=ffn_up_1.astype(BF16), ffn_conv_w=ffn_conv_w_1, ffn_conv_b=ffn_conv_b_1,
             ffn_down=ffn_down_1.astype(BF16)),
    ]
    bp = x_prompt.shape[0]
    st_prompt = [
        dict(hgrn=jnp.zeros((bp, H_A, DK_A, DV_A), F32), mlstm_c=jnp.zeros((bp, H_B, DK_B, DV_B), F32),
             mlstm_n=jnp.zeros((bp, H_B, DK_B), F32), mlstm_m=jnp.full((bp, H_B), NEG_BIG, F32),
             ffn_conv=jnp.zeros((bp, CONV_FFN - 1, D_FF), F32)),
        dict(gdn=jnp.zeros((bp, H_C, DK_C, DV_C), F32),
             gdn_conv=jnp.zeros((bp, CONV_C - 1, H_C * (2 * DK_C + DV_C)), F32),
             k_cache=jnp.zeros((bp, 0, H_D, 2 * D_HD), F32), v_cache=jnp.zeros((bp, 0, H_D, DV_D), F32),
             ffn_conv=jnp.zeros((bp, CONV_FFN - 1, D_FF), F32)),
    ]
    st_sample = [
        dict(hgrn=state_hgrn_0, mlstm_c=state_mlstm_c_0, mlstm_n=state_mlstm_n_0,
             mlstm_m=state_mlstm_m_0, ffn_conv=state_ffn_conv_0),
        dict(gdn=state_gdn_1, gdn_conv=state_gdn_conv_1, k_cache=cache_k_1, v_cache=cache_v_1,
             ffn_conv=state_ffn_conv_1),
    ]
    y_p, new_p = _run_group(x_prompt, st_prompt, params, hgrn_lower_bounds, norm_final)
    y_s, new_s = _run_group(x_sample, st_sample, params, hgrn_lower_bounds, norm_final)
    (hgrn_p, c_p, n_p, m_p, f0_p), (gdn_p, gc_p, k_p, v_p, f1_p) = new_p
    (hgrn_s, c_s, n_s, m_s, f0_s), (gdn_s, gc_s, k_s, v_s, f1_s) = new_s
    return (y_p, y_s, hgrn_p, hgrn_s, c_p, c_s, n_p, n_s, m_p, m_s, f0_p, f0_s, gdn_p, gdn_s,
            gc_p, gc_s, k_p, k_s, v_p, v_s, f1_p, f1_s)
```

```python
import functools
import math

import jax
import jax.numpy as jnp
from jax import lax
from jax.experimental import pallas as pl
from jax.experimental.pallas import tpu as pltpu

F32 = jnp.float32
BF16 = jnp.bfloat16
HI = lax.Precision.HIGHEST

D_MODEL = 2048
CHUNK = 64
H_A, DK_A, DV_A = 8, 128, 128
H_B, DK_B, DV_B = 4, 128, 256
H_C, DK_C, DV_C = 8, 128, 128
CONV_C = 4
H_D, D_HD = 8, 64
DV_D = 2 * D_HD
D_FF = 5632
CONV_FFN = 3
EPS = 1e-6
NEG_BIG = -1e30

LANES = 128
SUBLANES = 8
VMEM_LIMIT = 56 * 1024 * 1024
UNROLL = 4

AB_Q, AB_F, AB_I, AB_G = 0, 8, 16, 24
AB_BQ, AB_BK = 32, 36
AB_BV, AB_BO = 20, 24
AB_GATES = 56
IN_AB_PAD = 57 * LANES
CD_Q, CD_K, CD_V, CD_Z = 0, 8, 16, 24
CD_DQ, CD_DK, CD_DV = 32, 40, 48
CD_GATES = 56
IN_CD_PAD = 57 * LANES


def _mm(a, b, prec=None):
    return lax.dot_general(a, b, (((1,), (0,)), ((), ())), precision=prec,
                           preferred_element_type=F32)


def _mm_nt(a, b, prec=None):
    return lax.dot_general(a, b, (((1,), (1,)), ((), ())), precision=prec,
                           preferred_element_type=F32)


def _mm_tn(a, b, prec=None):
    return lax.dot_general(a, b, (((0,), (0,)), ((), ())), precision=prec,
                           preferred_element_type=F32)


def _sigmoid(x):
    return 1.0 / (1.0 + jnp.exp(-x))


def _silu(x):
    return x * _sigmoid(x)


def _softplus(x):
    return jnp.maximum(x, 0.0) + jnp.log1p(jnp.exp(-jnp.abs(x)))


def _rms(x, g):
    return x * lax.rsqrt(jnp.mean(x * x, axis=-1, keepdims=True) + EPS) * g


def _params(*sem):
    return pltpu.CompilerParams(dimension_semantics=sem, vmem_limit_bytes=VMEM_LIMIT)


def _norm_matmul_kernel(x_ref, g_ref, w_ref, o_ref, h_ref):
    @pl.when(pl.program_id(1) == 0)
    def _():
        h_ref[...] = _rms(x_ref[...], g_ref[...]).astype(BF16)

    o_ref[...] = _mm(h_ref[...], w_ref[...])


def norm_matmul(x, g, w, tm, tn):
    n, k = x.shape
    m = w.shape[1]
    return pl.pallas_call(
        _norm_matmul_kernel,
        grid=(n // tm, m // tn),
        in_specs=[pl.BlockSpec((tm, k), lambda i, j: (i, 0)),
                  pl.BlockSpec((1, k), lambda i, j: (0, 0)),
                  pl.BlockSpec((k, tn), lambda i, j: (0, j))],
        out_specs=pl.BlockSpec((tm, tn), lambda i, j: (i, j)),
        out_shape=jax.ShapeDtypeStruct((n, m), F32),
        scratch_shapes=[pltpu.VMEM((tm, k), BF16)],
        compiler_params=_params("arbitrary", "arbitrary"),
        name="norm_matmul",
    )(x, g.reshape(1, k), w)


def _matmul_res_kernel(y_ref, w_ref, x_ref, g_ref, o_ref, *, nk, final_norm):
    kk = pl.program_id(1)
    d = _mm(y_ref[...], w_ref[...])

    @pl.when(kk == 0)
    def _():
        o_ref[...] = x_ref[...] + d

    @pl.when(kk != 0)
    def _():
        o_ref[...] += d

    if final_norm:
        @pl.when(kk == nk - 1)
        def _():
            o_ref[...] = _rms(o_ref[...], g_ref[...])


def matmul_residual(y, w, x, g, tm, tk, final_norm):
    n, k = y.shape
    d = w.shape[1]
    nk = k // tk
    return pl.pallas_call(
        functools.partial(_matmul_res_kernel, nk=nk, final_norm=final_norm),
        grid=(n // tm, nk),
        in_specs=[pl.BlockSpec((tm, tk), lambda i, kk: (i, kk)),
                  pl.BlockSpec((tk, d), lambda i, kk: (kk, 0)),
                  pl.BlockSpec((tm, d), lambda i, kk: (i, 0)),
                  pl.BlockSpec((1, d), lambda i, kk: (0, 0))],
        out_specs=pl.BlockSpec((tm, d), lambda i, kk: (i, 0)),
        out_shape=jax.ShapeDtypeStruct((n, d), F32),
        compiler_params=_params("arbitrary", "arbitrary"),
        name="matmul_residual",
    )(y, w, x, g.reshape(1, d))


def _ffn_up_kernel(x_ref, g_ref, wa_ref, wu_ref, cw_ref, cb_ref, st_ref, act_ref, tail_ref,
                   h_ref, ext_ref, carry_ref, *, nseq, rows, tiles_per_seq):
    i = pl.program_id(0)
    j = pl.program_id(1)
    halo = CONV_FFN - 1

    @pl.when(j == 0)
    def _():
        h_ref[...] = _rms(x_ref[...], g_ref[...]).astype(BF16)

    if tiles_per_seq > 1:
        @pl.when(i == 0)
        def _():
            carry_ref[j] = jnp.zeros(carry_ref.shape[1:], F32)

    h = h_ref[...]
    a = _mm(h, wa_ref[...])
    u = _mm(h, wu_ref[...])
    cw = cw_ref[...]
    cb = cb_ref[...]
    for s in range(nseq):
        lo = s * rows
        ext_ref[s, SUBLANES:, :] = a[lo:lo + rows]
        prev = st_ref[s]
        if tiles_per_seq > 1:
            prev = jnp.where((i % tiles_per_seq) == 0, prev, carry_ref[j, SUBLANES - halo:, :])
        ext_ref[s, SUBLANES - halo:SUBLANES, :] = prev
        conv = cb
        for t in range(CONV_FFN):
            off = SUBLANES - halo + t
            conv = conv + ext_ref[s, off:off + rows, :] * cw[t:t + 1]
        act_ref[lo:lo + rows, :] = (_silu(conv) * u[lo:lo + rows]).astype(BF16)
        tail_ref[s] = ext_ref[s, SUBLANES + rows - halo:SUBLANES + rows, :]
        carry_ref[j] = ext_ref[s, rows:rows + SUBLANES, :]


def ffn_up(x, g, w_up, conv_w, conv_b, state, batch, seq, tm, tf):
    n, k = x.shape
    nff = D_FF // tf
    if tm >= seq:
        nseq, rows, tiles_per_seq = tm // seq, seq, 1
    else:
        nseq, rows, tiles_per_seq = 1, tm, seq // tm
    ntiles = n // tm
    halo = CONV_FFN - 1
    act, tail = pl.pallas_call(
        functools.partial(_ffn_up_kernel, nseq=nseq, rows=rows, tiles_per_seq=tiles_per_seq),
        grid=(ntiles, nff),
        in_specs=[pl.BlockSpec((tm, k), lambda i, j: (i, 0)),
                  pl.BlockSpec((1, k), lambda i, j: (0, 0)),
                  pl.BlockSpec((k, tf), lambda i, j: (0, j)),
                  pl.BlockSpec((k, tf), lambda i, j: (0, j + nff)),
                  pl.BlockSpec((CONV_FFN, tf), lambda i, j: (0, j)),
                  pl.BlockSpec((1, tf), lambda i, j: (0, j)),
                  pl.BlockSpec((nseq, halo, tf), lambda i, j: (i // tiles_per_seq, 0, j))],
        out_specs=[pl.BlockSpec((tm, tf), lambda i, j: (i, j)),
                   pl.BlockSpec((nseq, halo, tf), lambda i, j: (i, 0, j))],
        out_shape=[jax.ShapeDtypeStruct((n, D_FF), BF16),
                   jax.ShapeDtypeStruct((ntiles * nseq, halo, D_FF), F32)],
        scratch_shapes=[pltpu.VMEM((tm, k), BF16),
                        pltpu.VMEM((nseq, SUBLANES + rows, tf), F32),
                        pltpu.VMEM((nff, SUBLANES, tf), F32)],
        compiler_params=_params("arbitrary", "arbitrary"),
        name="ffn_up",
    )(x, g.reshape(1, k), w_up, w_up, conv_w, conv_b.reshape(1, D_FF), state)
    new_state = tail.reshape(batch, -1, halo, D_FF)[:, -1]
    return act, new_state


def _iota2(shape, dim):
    return lax.broadcasted_iota(jnp.int32, shape, dim)


def _tril(n):
    return (_iota2((n, n), 1) <= _iota2((n, n), 0)).astype(F32)


def _lane_col(x, lane):
    return jnp.sum(jnp.where(_iota2(x.shape, 1) == lane, x, 0.0), axis=-1, keepdims=True)


def _bf(x):
    return x.astype(BF16)


def _split3(x):
    h1 = _bf(x)
    r1 = x - h1.astype(F32)
    h2 = _bf(r1)
    h3 = _bf(r1 - h2.astype(F32))
    return h1, h2, h3


def _mm_sel(sel, x):
    n = x.shape[1]
    y = _mm(_bf(sel), jnp.concatenate(_split3(x), axis=1))
    return y[:, :n] + y[:, n:2 * n] + y[:, 2 * n:]


def _lane_row(x, lane):
    sel = _bf(jnp.where(_iota2((SUBLANES, LANES), 1) == lane, 1.0, 0.0))
    h1, h2, h3 = _split3(x)
    return (_mm_nt(sel, h1) + _mm_nt(sel, h2) + _mm_nt(sel, h3))[0:1]


def _rows(c, L):
    return pl.ds(c * L if isinstance(c, int) else pl.multiple_of(c * L, L), L)


def _chunk_loop(n_chunks, unroll, body):
    if n_chunks <= unroll:
        for c in range(n_chunks):
            body(c)
        return

    def group(i, carry):
        for u in range(unroll):
            body(i * unroll + u)
        return carry

    lax.fori_loop(0, n_chunks // unroll, group, 0)


def _hgrn_kernel(q_ref, f_ref, i_ref, g_ref, lb_ref, gn_ref, s0_ref, o_ref, s_out_ref, st_ref,
                 *, L, n_chunks, unroll, lb_index):
    tb = pl.program_id(2)

    @pl.when(tb == 0)
    def _():
        st_ref[...] = s0_ref[0, 0]

    lbp = lb_ref[...]
    e = jnp.exp(lbp - jnp.max(lbp, axis=0, keepdims=True))
    lb = jnp.sum(e[:lb_index + 1], axis=0, keepdims=True) / jnp.sum(e, axis=0, keepdims=True)

    sizes = [L >> l for l in range(int(math.log2(L)))]
    ti = _iota2((L, L), 0)
    si = _iota2((L, L), 1)
    tcol = _iota2((L, 1), 0)
    mats = [si <= ti]
    for sz in sizes:
        mats.append(si <= ((ti & ~(sz - 1)) + (sz // 2 - 1)))
    cum_mat = jnp.concatenate([m.astype(F32) for m in mats], axis=0)
    eye = ti == si
    gn = gn_ref[...]

    def chunk(c):
        rows = _rows(c, L)
        q = q_ref[rows, :]
        v = i_ref[rows, :]
        f = lb + (1.0 - lb) * _sigmoid(f_ref[rows, :])
        k = 1.0 - f
        cums = _mm_sel(cum_mat, jnp.log(f))
        b = cums[:L]
        scores = jnp.where(eye, jnp.sum(q * k, axis=-1, keepdims=True), 0.0)
        for l, sz in enumerate(sizes):
            w = jnp.exp(-jnp.abs(b - cums[(l + 1) * L:(l + 2) * L]))
            right = (tcol & (sz - 1)) >= sz // 2
            qm = jnp.where(right, q * w, 0.0)
            km = jnp.where(right, 0.0, k * w)
            same = (ti & ~(sz - 1)) == (si & ~(sz - 1))
            scores = scores + jnp.where(same, _mm_nt(_bf(qm), _bf(km)), 0.0)
        vb = _bf(v)
        bl = b[L - 1:L]
        update = _mm_tn(vb, _bf(k * jnp.exp(bl - b)))
        st = st_ref[...]
        o = _mm_nt(_bf(q * jnp.exp(b)), _bf(st)) + _mm(_bf(scores), vb)
        st_ref[...] = jnp.exp(bl) * st + update
        o_ref[rows, :] = (_rms(o, gn) * _silu(g_ref[rows, :])).astype(BF16)

    _chunk_loop(n_chunks, unroll, chunk)

    @pl.when(tb == pl.num_programs(2) - 1)
    def _():
        s_out_ref[0, 0] = st_ref[...]


def hgrn2(z, lower_bounds, norm_g, s0, batch, seq, tb, lb_index):
    L = min(CHUNK, seq)
    nt = seq // tb

    def col(c0):
        return pl.BlockSpec((tb, LANES), lambda b, h, t: (b * nt + t, c0 + h))

    st_spec = pl.BlockSpec((1, 1, DV_A, DK_A), lambda b, h, t: (b, h, 0, 0))
    o, st = pl.pallas_call(
        functools.partial(_hgrn_kernel, L=L, n_chunks=tb // L, unroll=UNROLL, lb_index=lb_index),
        grid=(batch, H_A, nt),
        in_specs=[col(AB_Q), col(AB_F), col(AB_I), col(AB_G),
                  pl.BlockSpec((lower_bounds.shape[0], LANES), lambda b, h, t: (0, h)),
                  pl.BlockSpec((1, DV_A), lambda b, h, t: (0, 0)),
                  st_spec],
        out_specs=[pl.BlockSpec((tb, DV_A), lambda b, h, t: (b * nt + t, h)), st_spec],
        out_shape=[jax.ShapeDtypeStruct((batch * seq, H_A * DV_A), BF16),
                   jax.ShapeDtypeStruct((batch, H_A, DV_A, DK_A), F32)],
        scratch_shapes=[pltpu.VMEM((DV_A, DK_A), F32)],
        compiler_params=_params("arbitrary", "arbitrary", "arbitrary"),
        name="hgrn2",
    )(z, z, z, z, lower_bounds, norm_g.reshape(1, DV_A), jnp.swapaxes(s0, 2, 3))
    return o, jnp.swapaxes(st, 2, 3)


def _mlstm_kernel(q_ref, k_ref, v_ref, og_ref, gt_ref, bias_ref, gn_ref, c0_ref, n0_ref, m0_ref,
                  o_ref, c_out_ref, n_out_ref, m_out_ref, ct_ref, n_ref, m_ref,
                  *, L, n_chunks, unroll):
    h = pl.program_id(1)
    tb = pl.program_id(2)

    @pl.when(tb == 0)
    def _():
        ct_ref[...] = c0_ref[0, 0]
        n_ref[...] = n0_ref[0, 0]
        m_ref[...] = m0_ref[0, 0]

    tril = _tril(L)
    causal = _iota2((L, L), 1) <= _iota2((L, L), 0)
    bias = bias_ref[...]
    gn = gn_ref[...]
    scale = DK_B ** -0.5

    def chunk(c):
        rows = _rows(c, L)
        q = q_ref[rows, :] * scale
        k = k_ref[rows, :]
        vb = _bf(v_ref[rows, :])
        qb = _bf(q)
        gates = gt_ref[rows, :] + bias
        logf = jnp.minimum(gates, 0.0) - jnp.log1p(jnp.exp(-jnp.abs(gates)))
        cum = _mm_sel(tril, logf)
        ig_col = _lane_col(gates, h)
        ig_row = _lane_row(gates, h)
        b_col = _lane_col(cum, H_B + h)
        b_row = _lane_row(cum, H_B + h)
        d = b_col - b_row + ig_row
        d_max = jnp.max(jnp.where(causal, d, -jnp.inf), axis=-1, keepdims=True)
        qk = _mm_nt(qb, _bf(k))
        b_last = b_col[L - 1:L]
        wk = b_last - b_col + ig_col
        wk_max = jnp.max(wk, axis=0, keepdims=True)
        ct = ct_ref[...]
        n = n_ref[...]
        m_prev = m_ref[...][:, 0:1]
        inter = b_col + m_prev
        m_t = jnp.maximum(d_max, inter)
        p = jnp.exp(jnp.where(causal, d - m_t, -jnp.inf)) * qk
        w_inter = jnp.exp(inter - m_t)
        num = _mm(_bf(p), vb) + w_inter * _mm_nt(qb, _bf(ct))
        den = (jnp.sum(p, axis=-1, keepdims=True)
               + w_inter * jnp.sum(q * n, axis=-1, keepdims=True))
        hid = num / jnp.maximum(jnp.abs(den), jnp.exp(-m_t))
        m_new = jnp.maximum(b_last + m_prev, wk_max)
        sc = jnp.exp(wk - m_new)
        carry_w = jnp.exp(b_last + m_prev - m_new)
        ks = sc * k
        ct_ref[...] = carry_w * ct + _mm_tn(vb, _bf(ks))
        n_ref[...] = carry_w * n + jnp.sum(ks, axis=0, keepdims=True)
        m_ref[...] = jnp.broadcast_to(m_new, (1, LANES))
        o_ref[rows, :] = (_rms(hid, gn) * _sigmoid(og_ref[rows, :])).astype(BF16)

    _chunk_loop(n_chunks, unroll, chunk)

    @pl.when(tb == pl.num_programs(2) - 1)
    def _():
        c_out_ref[0, 0] = ct_ref[...]
        n_out_ref[0, 0] = n_ref[...]
        m_out_ref[0, 0] = m_ref[...]


def mlstm(z, i_bias, f_bias, norm_g, c0, n0, m0, batch, seq, tb):
    L = min(CHUNK, seq)
    nt = seq // tb
    bias = jnp.zeros((1, LANES), F32).at[0, :H_B].set(i_bias).at[0, H_B:2 * H_B].set(f_bias)

    def col(c0_, width):
        return pl.BlockSpec((tb, width), lambda b, h, t: (b * nt + t, c0_ + h))

    def state(r, c):
        return pl.BlockSpec((1, 1, r, c), lambda b, h, t: (b, h, 0, 0))

    o, ct, n, m = pl.pallas_call(
        functools.partial(_mlstm_kernel, L=L, n_chunks=tb // L, unroll=UNROLL),
        grid=(batch, H_B, nt),
        in_specs=[col(AB_BQ, DK_B), col(AB_BK, DK_B), col(AB_BV, DV_B), col(AB_BO, DV_B),
                  pl.BlockSpec((tb, LANES), lambda b, h, t: (b * nt + t, AB_GATES)),
                  pl.BlockSpec((1, LANES), lambda b, h, t: (0, 0)),
                  pl.BlockSpec((1, DV_B), lambda b, h, t: (0, 0)),
                  state(DV_B, DK_B), state(1, DK_B), state(1, LANES)],
        out_specs=[pl.BlockSpec((tb, DV_B), lambda b, h, t: (b * nt + t, h)),
                   state(DV_B, DK_B), state(1, DK_B), state(1, LANES)],
        out_shape=[jax.ShapeDtypeStruct((batch * seq, H_B * DV_B), BF16),
                   jax.ShapeDtypeStruct((batch, H_B, DV_B, DK_B), F32),
                   jax.ShapeDtypeStruct((batch, H_B, 1, DK_B), F32),
                   jax.ShapeDtypeStruct((batch, H_B, 1, LANES), F32)],
        scratch_shapes=[pltpu.VMEM((DV_B, DK_B), F32), pltpu.VMEM((1, DK_B), F32),
                        pltpu.VMEM((1, LANES), F32)],
        compiler_params=_params("arbitrary", "arbitrary", "arbitrary"),
        name="mlstm",
    )(z, z, z, z, z, bias, norm_g.reshape(1, DV_B), jnp.swapaxes(c0, 2, 3),
      n0.reshape(batch, H_B, 1, DK_B),
      jnp.broadcast_to(m0.reshape(batch, H_B, 1, 1), (batch, H_B, 1, LANES)))
    return o, jnp.swapaxes(ct, 2, 3), n.reshape(batch, H_B, DK_B), m[:, :, 0, 0]


def _gdn_kernel(q_ref, k_ref, v_ref, z_ref, gt_ref, cw_ref, cs_ref, gp_ref, gn_ref, s0_ref,
                o_ref, s_out_ref, st_ref, ext_ref, cv_ref, *, L, n_chunks, unroll, tb):
    h = pl.program_id(1)
    t = pl.program_id(2)
    halo = CONV_C - 1

    @pl.when(t == 0)
    def _():
        st_ref[...] = s0_ref[0, 0]
        for p in range(3):
            ext_ref[p, SUBLANES - halo:SUBLANES, :] = cs_ref[p, 0]

    srcs = (q_ref, k_ref, v_ref)
    for p in range(3):
        ext_ref[p, SUBLANES:, :] = srcs[p][...]
        cw = cw_ref[p]
        acc = None
        for j in range(CONV_C):
            off = SUBLANES - halo + j
            term = ext_ref[p, off:off + tb, :] * cw[j:j + 1]
            acc = term if acc is None else acc + term
        cv_ref[p] = _silu(acc)
    tril = _tril(L)
    ti = _iota2((L, L), 0)
    si = _iota2((L, L), 1)
    gp = gp_ref[...]
    gn = gn_ref[...]

    def l2n(x):
        return x * lax.rsqrt(jnp.sum(x * x, axis=-1, keepdims=True) + EPS)

    def chunk(c):
        rows = _rows(c, L)
        q = l2n(cv_ref[0, rows, :]) * (DK_C ** -0.5)
        k = l2n(cv_ref[1, rows, :])
        v = cv_ref[2, rows, :]
        gates = gt_ref[rows, :]
        beta = _lane_col(_sigmoid(gates), h)
        logg = -jnp.exp(gp[0:1]) * _softplus(gates + gp[1:2])
        cum = _mm_sel(tril, logg)
        b_col = _lane_col(cum, H_C + h)
        b_row = _lane_row(cum, H_C + h)
        rel = b_col - b_row
        dec_incl = jnp.exp(jnp.where(si <= ti, rel, -jnp.inf))
        dec_strict = jnp.where(si < ti, dec_incl, 0.0)
        eb = jnp.exp(b_col)
        kb = _bf(k)
        qb = _bf(q)
        neg_m = -(beta * _mm_nt(kb, kb) * dec_strict)
        inv = jnp.where(ti == si, 1.0, 0.0) + neg_m
        pw = neg_m
        for _ in range(int(math.log2(L)) - 1):
            pwb = _bf(pw)
            pw = _mm(pwb, pwb)
            inv = inv + _mm(_bf(pw), _bf(inv))
        wu = _mm(_bf(inv), _bf(jnp.concatenate([beta * v, (beta * eb) * k], axis=1)))
        w = wu[:, :DV_C]
        u = wu[:, DV_C:]
        attn = _bf(_mm_nt(qb, kb) * dec_incl)
        b_last = b_col[L - 1:L]
        k_dec = _bf(k * jnp.exp(b_last - b_col))
        st = st_ref[...]
        stb = _bf(st)
        delta = w - _mm_nt(_bf(u), stb)
        db = _bf(delta)
        o = eb * _mm_nt(qb, stb) + _mm(attn, db)
        st_ref[...] = jnp.exp(b_last) * st + _mm_tn(db, k_dec)
        o_ref[rows, :] = (_rms(o, gn) * _silu(z_ref[rows, :])).astype(BF16)

    _chunk_loop(n_chunks, unroll, chunk)
    for p in range(3):
        ext_ref[p, 0:SUBLANES, :] = ext_ref[p, tb:tb + SUBLANES, :]

    @pl.when(t == pl.num_programs(2) - 1)
    def _():
        s_out_ref[0, 0] = st_ref[...]


def gdn(z, conv_w, conv_state, a_log, dt_bias, norm_g, s0, batch, seq, tb):
    L = min(CHUNK, seq)
    nt = seq // tb
    halo = CONV_C - 1
    gate_params = (jnp.zeros((2, LANES), F32).at[0, H_C:2 * H_C].set(a_log)
                   .at[1, H_C:2 * H_C].set(dt_bias))
    cw = conv_w.reshape(CONV_C, 3, H_C * DK_C).transpose(1, 0, 2)
    cs = conv_state.reshape(batch, halo, 3, H_C * DK_C).transpose(2, 0, 1, 3)

    def col(c0):
        return pl.BlockSpec((tb, LANES), lambda b, h, t: (b * nt + t, c0 + h))

    st_spec = pl.BlockSpec((1, 1, DV_C, DK_C), lambda b, h, t: (b, h, 0, 0))
    o, st = pl.pallas_call(
        functools.partial(_gdn_kernel, L=L, n_chunks=tb // L, unroll=UNROLL, tb=tb),
        grid=(batch, H_C, nt),
        in_specs=[col(CD_Q), col(CD_K), col(CD_V), col(CD_Z),
                  pl.BlockSpec((tb, LANES), lambda b, h, t: (b * nt + t, CD_GATES)),
                  pl.BlockSpec((3, CONV_C, LANES), lambda b, h, t: (0, 0, h)),
                  pl.BlockSpec((3, 1, halo, LANES), lambda b, h, t: (0, b, 0, h)),
                  pl.BlockSpec((2, LANES), lambda b, h, t: (0, 0)),
                  pl.BlockSpec((1, DV_C), lambda b, h, t: (0, 0)),
                  st_spec],
        out_specs=[pl.BlockSpec((tb, DV_C), lambda b, h, t: (b * nt + t, h)), st_spec],
        out_shape=[jax.ShapeDtypeStruct((batch * seq, H_C * DV_C), BF16),
                   jax.ShapeDtypeStruct((batch, H_C, DV_C, DK_C), F32)],
        scratch_shapes=[pltpu.VMEM((DV_C, DK_C), F32),
                        pltpu.VMEM((3, SUBLANES + tb, LANES), F32),
                        pltpu.VMEM((3, tb, LANES), F32)],
        compiler_params=_params("arbitrary", "arbitrary", "arbitrary"),
        name="gdn",
    )(z, z, z, z, z, cw, cs, gate_params, norm_g.reshape(1, DV_C), jnp.swapaxes(s0, 2, 3))
    return o, jnp.swapaxes(st, 2, 3)


def _attn_kernel(q_ref, k_ref, v_ref, lp_ref, gn_ref, o_ref, kb_ref, vb_ref, m_ref, acc_ref,
                 *, seq, kv_len, tq, tk, lam_init):
    h = pl.program_id(1)
    qi = pl.program_id(2)
    past = kv_len - seq
    shift = int(math.log2(CHUNK))

    @pl.when(qi == 0)
    def _():
        kb_ref[...] = _bf(k_ref[...])
        vb_ref[:, :DV_D] = _bf(v_ref[...])
        vb_ref[:, DV_D:] = jnp.ones((kv_len, LANES), BF16)

    lp = lp_ref[...]
    lam = (jnp.exp(jnp.sum(lp[0:1] * lp[1:2], axis=-1, keepdims=True))
           - jnp.exp(jnp.sum(lp[2:3] * lp[3:4], axis=-1, keepdims=True)) + lam_init)
    hf = jnp.full((1, 1), h + 1, jnp.int32).astype(F32)
    slope = jnp.exp(hf * (-8.0 / H_D * math.log(2.0)))

    q = q_ref[...] * (D_HD ** -0.5)
    first = _iota2((tq, 2 * D_HD), 1) < D_HD
    q_maps = (_bf(jnp.where(first, q, 0.0)), _bf(jnp.where(first, 0.0, q)))
    q_start = past + qi * tq
    q_pos = q_start + _iota2((tq, 1), 0)
    q_chunk = q_pos >> shift
    q_shift = slope * q_pos.astype(F32)

    m_ref[...] = jnp.full(m_ref.shape, NEG_BIG, F32)
    acc_ref[...] = jnp.zeros(acc_ref.shape, F32)

    last_allowed = (((q_start + tq - 1) >> shift) << shift) + (CHUNK - 1)
    n_blocks = jnp.minimum(last_allowed, kv_len - 1) // tk + 1
    n_before = jnp.minimum((q_start + 1) // tk, n_blocks)

    def update(c, s, p_mask, vs):
        m_old = m_ref[c]
        s_max = s[:, :LANES]
        for g in range(1, s.shape[1] // LANES):
            s_max = jnp.maximum(s_max, s[:, g * LANES:(g + 1) * LANES])
        if s.shape[1] % LANES:
            tail = s[:, (s.shape[1] // LANES) * LANES:]
            m_new = jnp.maximum(jnp.max(s_max, axis=-1, keepdims=True),
                                jnp.max(tail, axis=-1, keepdims=True))
        else:
            m_new = jnp.max(s_max, axis=-1, keepdims=True)
        m_new = jnp.maximum(m_old, m_new)
        p = jnp.exp(s - m_new)
        if p_mask is not None:
            p = jnp.where(p_mask, p, 0.0)
        acc_ref[c] = jnp.exp(m_old - m_new) * acc_ref[c] + _mm(_bf(p), vs)
        m_ref[c] = m_new

    def block_before(j, carry):
        rows = pl.ds(pl.multiple_of(j * tk, 2 * SUBLANES), tk)
        ks = kb_ref[rows, :]
        vs = vb_ref[rows, :]
        k_bias = slope * (j * tk + _iota2((1, tk), 1)).astype(F32)
        for c in range(2):
            update(c, _mm_nt(q_maps[c], ks) + k_bias, None, vs)
        return carry

    def block_masked(j, carry):
        rows = pl.ds(pl.multiple_of(j * tk, 2 * SUBLANES), tk)
        ks = kb_ref[rows, :]
        vs = vb_ref[rows, :]
        k_pos = j * tk + _iota2((1, tk), 1)
        allowed = (k_pos >> shift) <= q_chunk
        bias = q_shift - slope * jnp.abs(q_pos - k_pos).astype(F32)
        for c in range(2):
            s = jnp.where(allowed, _mm_nt(q_maps[c], ks) + bias, NEG_BIG)
            update(c, s, allowed, vs)
        return carry

    lax.fori_loop(0, n_before, block_before, 0)
    lax.fori_loop(n_before, n_blocks, block_masked, 0)
    a0 = acc_ref[0]
    a1 = acc_ref[1]
    o = (a0[:, :DV_D] / a0[:, DV_D:DV_D + 1] - lam * (a1[:, :DV_D] / a1[:, DV_D:DV_D + 1]))
    o_ref[...] = (_rms(o, gn_ref[...]) * (1.0 - lam_init)).astype(BF16)


def diff_attention(zq, q_col, kd, k_col, vd, v_col, lam_params, norm_g, batch, seq, kv_len,
                   tq, tk, lam_init):
    nq = seq // tq
    return pl.pallas_call(
        functools.partial(_attn_kernel, seq=seq, kv_len=kv_len, tq=tq, tk=tk, lam_init=lam_init),
        grid=(batch, H_D, nq),
        in_specs=[pl.BlockSpec((tq, DV_D), lambda b, h, i: (b * nq + i, q_col + h)),
                  pl.BlockSpec((kv_len, DV_D), lambda b, h, i: (b, k_col + h)),
                  pl.BlockSpec((kv_len, DV_D), lambda b, h, i: (b, v_col + h)),
                  pl.BlockSpec((4, D_HD), lambda b, h, i: (0, 0)),
                  pl.BlockSpec((1, DV_D), lambda b, h, i: (0, 0))],
        out_specs=pl.BlockSpec((tq, DV_D), lambda b, h, i: (b * nq + i, h)),
        out_shape=jax.ShapeDtypeStruct((batch * seq, H_D * DV_D), BF16),
        scratch_shapes=[pltpu.VMEM((kv_len, DV_D), BF16), pltpu.VMEM((kv_len, DV_D + LANES), BF16),
                        pltpu.VMEM((2, tq, 1), F32), pltpu.VMEM((2, tq, DV_D + LANES), F32)],
        compiler_params=_params("arbitrary", "arbitrary", "arbitrary"),
        name="diff_attention",
    )(zq, kd, vd, lam_params, norm_g.reshape(1, DV_D))


def _pad_cols(w, total):
    return jnp.pad(w, ((0, 0), (0, total - w.shape[1])))


def _prep_w_in_ab(w):
    sizes = (H_A * DK_A, H_A * DK_A, H_A * DV_A, H_A * DV_A, H_B * DK_B, H_B * DK_B, H_B * DV_B,
             H_B, H_B, H_B * DV_B)
    aq, af, ai, ag, bq, bk, bv, big, bfg, bo = jnp.split(w, np_cumsum(sizes), axis=1)
    return _pad_cols(jnp.concatenate([aq, af, ai, ag, bq, bk, bv, bo, big, bfg], axis=1),
                     IN_AB_PAD).astype(BF16)


def _prep_w_in_cd(w):
    sizes = (H_C * (2 * DK_C + DV_C), H_C, H_C, H_C * DV_C, H_D * 2 * D_HD, H_D * 2 * D_HD,
             H_D * DV_D)
    cqkv, cb, ca, cz, dq, dk, dv = jnp.split(w, np_cumsum(sizes), axis=1)
    return _pad_cols(jnp.concatenate([cqkv, cz, dq, dk, dv, cb, ca], axis=1),
                     IN_CD_PAD).astype(BF16)


def np_cumsum(sizes):
    out, acc = [], 0
    for s in sizes[:-1]:
        acc += s
        out.append(acc)
    return out


def _tiles(n_rows, seq):
    tm = min(512, n_rows)
    tb = min(512, seq)
    return tm, tb


def _layer_ab(x, st, p, lower_bounds, batch, seq, layer):
    n = batch * seq
    tm, tb = _tiles(n, seq)
    z = norm_matmul(x, p['norm_mix'], p['w_in'], tm, IN_AB_PAD // 3)
    o_a, s_new = hgrn2(z, lower_bounds, p['hgrn_norm'], st['hgrn'], batch, seq, tb, layer // 2)
    o_b, c_new, n_new, m_new = mlstm(z, p['i_bias'], p['f_bias'], p['mlstm_norm'], st['mlstm_c'],
                                     st['mlstm_n'], st['mlstm_m'], batch, seq, tb)
    y = jnp.concatenate([o_a, o_b], axis=1)
    x = matmul_residual(y, p['w_out'], x, p['norm_ffn'], tm, 1024, False)
    return x, (s_new, c_new, n_new, m_new)


def _layer_cd(x, st, p, batch, seq, layer):
    n = batch * seq
    tm, tb = _tiles(n, seq)
    z = norm_matmul(x, p['norm_mix'], p['w_in'], tm, IN_CD_PAD // 3)
    o_c, s_new = gdn(z, p['conv_w'], st['gdn_conv'], p['a_log'], p['dt_bias'], p['gdn_norm'],
                     st['gdn'], batch, seq, tb)
    z3 = z.reshape(batch, seq, IN_CD_PAD)
    conv_new = z3[:, seq - (CONV_C - 1):, :H_C * (2 * DK_C + DV_C)]
    k_new = z3[:, :, CD_DK * LANES:CD_DV * LANES]
    v_new = z3[:, :, CD_DV * LANES:CD_GATES * LANES]
    past = st['k_cache'].shape[1]
    lam_init = 0.8 - 0.6 * math.exp(-0.3 * layer)
    if past == 0:
        o_d = diff_attention(z, CD_DQ, z, CD_DK, z, CD_DV, p['lam'], p['diff_norm'], batch, seq,
                             seq, min(512, seq), min(512, seq), lam_init)
    else:
        kv_len = past + seq
        kd = jnp.concatenate([st['k_cache'].reshape(batch, past, -1), k_new], axis=1)
        vd = jnp.concatenate([st['v_cache'].reshape(batch, past, -1), v_new], axis=1)
        o_d = diff_attention(z, CD_DQ, kd.reshape(batch * kv_len, -1), 0,
                             vd.reshape(batch * kv_len, -1), 0, p['lam'], p['diff_norm'], batch,
                             seq, kv_len, seq, kv_len, lam_init)
    y = jnp.concatenate([o_c, o_d], axis=1)
    x = matmul_residual(y, p['w_out'], x, p['norm_ffn'], tm, 1024, False)
    return x, (s_new, conv_new, k_new.reshape(batch, seq, H_D, 2 * D_HD),
               v_new.reshape(batch, seq, H_D, DV_D))


def _ffn(x, state, p, batch, seq, final_g):
    n = batch * seq
    tm, _ = _tiles(n, seq)
    act, conv_new = ffn_up(x, p['norm_ffn'], p['ffn_up'], p['ffn_conv_w'], p['ffn_conv_b'], state,
                           batch, seq, tm, 512)
    g = p['norm_ffn'] if final_g is None else final_g
    x = matmul_residual(act, p['ffn_down'], x, g, tm, D_FF // 4, final_g is not None)
    return x, conv_new


def _run_group(x, st, params, lower_bounds, norm_final):
    batch, seq, _ = x.shape
    x = x.reshape(batch * seq, D_MODEL)
    new_states = []
    for layer, (p, s) in enumerate(zip(params, st)):
        if layer % 2 == 0:
            x, new_mix = _layer_ab(x, s, p, lower_bounds, batch, seq, layer)
        else:
            x, new_mix = _layer_cd(x, s, p, batch, seq, layer)
        last = layer == len(params) - 1
        x, new_ffn = _ffn(x, s['ffn_conv'], p, batch, seq, norm_final if last else None)
        new_states.append(new_mix + (new_ffn,))
    return x.reshape(batch, seq, D_MODEL), new_states


def kernel(x_prompt, x_sample, state_hgrn_0, state_mlstm_c_0, state_mlstm_n_0, state_mlstm_m_0,
           state_ffn_conv_0, state_gdn_1, state_gdn_conv_1, cache_k_1, cache_v_1, state_ffn_conv_1,
           hgrn_lower_bounds, norm_mix_0, w_in_0, mlstm_i_bias_0, mlstm_f_bias_0, hgrn_norm_0,
           mlstm_norm_0, w_out_0, norm_ffn_0, ffn_up_0, ffn_conv_w_0, ffn_conv_b_0, ffn_down_0,
           norm_mix_1, w_in_1, gdn_conv_w_1, gdn_a_log_1, gdn_dt_bias_1, gdn_norm_1, diff_lambda_1,
           diff_norm_1, w_out_1, norm_ffn_1, ffn_up_1, ffn_conv_w_1, ffn_conv_b_1, ffn_down_1,
           norm_final):
    params = [
        dict(norm_mix=norm_mix_0, w_in=_prep_w_in_ab(w_in_0), i_bias=mlstm_i_bias_0,
             f_bias=mlstm_f_bias_0, hgrn_norm=hgrn_norm_0, mlstm_norm=mlstm_norm_0,
             w_out=w_out_0.astype(BF16), norm_ffn=norm_ffn_0, ffn_up=ffn_up_0.astype(BF16),
             ffn_conv_w=ffn_conv_w_0, ffn_conv_b=ffn_conv_b_0, ffn_down=ffn_down_0.astype(BF16)),
        dict(norm_mix=norm_mix_1, w_in=_prep_w_in_cd(w_in_1), conv_w=gdn_conv_w_1,
             a_log=gdn_a_log_1, dt_bias=gdn_dt_bias_1, gdn_norm=gdn_norm_1, lam=diff_lambda_1,
             diff_norm=diff_norm_1, w_out=w_out_1.astype(BF16), norm_ffn=norm_ffn_1,
             ffn_up=ffn_up_1.astype(BF16), ffn_conv_w=ffn_conv_w_1, ffn_conv_b=ffn_conv_b_1,
             ffn_down=ffn_down_1.astype(BF16)),
    ]
    bp = x_prompt.shape[0]
    st_prompt = [
        dict(hgrn=jnp.zeros((bp, H_A, DK_A, DV_A), F32), mlstm_c=jnp.zeros((bp, H_B, DK_B, DV_B), F32),
             mlstm_n=jnp.zeros((bp, H_B, DK_B), F32), mlstm_m=jnp.full((bp, H_B), NEG_BIG, F32),
             ffn_conv=jnp.zeros((bp, CONV_FFN - 1, D_FF), F32)),
        dict(gdn=jnp.zeros((bp, H_C, DK_C, DV_C), F32),
             gdn_conv=jnp.zeros((bp, CONV_C - 1, H_C * (2 * DK_C + DV_C)), F32),
             k_cache=jnp.zeros((bp, 0, H_D, 2 * D_HD), F32), v_cache=jnp.zeros((bp, 0, H_D, DV_D), F32),
             ffn_conv=jnp.zeros((bp, CONV_FFN - 1, D_FF), F32)),
    ]
    st_sample = [
        dict(hgrn=state_hgrn_0, mlstm_c=state_mlstm_c_0, mlstm_n=state_mlstm_n_0,
             mlstm_m=state_mlstm_m_0, ffn_conv=state_ffn_conv_0),
        dict(gdn=state_gdn_1, gdn_conv=state_gdn_conv_1, k_cache=cache_k_1, v_cache=cache_v_1,
             ffn_conv=state_ffn_conv_1),
    ]
    y_p, new_p = _run_group(x_prompt, st_prompt, params, hgrn_lower_bounds, norm_final)
    y_s, new_s = _run_group(x_sample, st_sample, params, hgrn_lower_bounds, norm_final)
    (hgrn_p, c_p, n_p, m_p, f0_p), (gdn_p, gc_p, k_p, v_p, f1_p) = new_p
    (hgrn_s, c_s, n_s, m_s, f0_s), (gdn_s, gc_s, k_s, v_s, f1_s) = new_s
    return (y_p, y_s, hgrn_p, hgrn_s, c_p, c_s, n_p, n_s, m_p, m_s, f0_p, f0_s, gdn_p, gdn_s,
            gc_p, gc_s, k_p, k_s, v_p, v_s, f1_p, f1_s)
```

```python
import functools
import math

import jax
import jax.numpy as jnp
from jax import lax
from jax.experimental import pallas as pl
from jax.experimental.pallas import tpu as pltpu

F32 = jnp.float32
BF16 = jnp.bfloat16
HI = lax.Precision.HIGHEST

D_MODEL = 2048
CHUNK = 64
H_A, DK_A, DV_A = 8, 128, 128
H_B, DK_B, DV_B = 4, 128, 256
H_C, DK_C, DV_C = 8, 128, 128
CONV_C = 4
H_D, D_HD = 8, 64
DV_D = 2 * D_HD
D_FF = 5632
CONV_FFN = 3
EPS = 1e-6
NEG_BIG = -1e30

LANES = 128
SUBLANES = 8
VMEM_LIMIT = 56 * 1024 * 1024
MIX_CHUNK = 128

AB_Q, AB_F, AB_I, AB_G = 0, 8, 16, 24
AB_BQ, AB_BK = 32, 36
AB_BV, AB_BO = 20, 24
AB_GATES = 56
IN_AB_PAD = 57 * LANES
CD_Q, CD_K, CD_V, CD_Z = 0, 8, 16, 24
CD_DQ, CD_DK, CD_DV = 32, 40, 48
CD_GATES = 56
IN_CD_PAD = 57 * LANES


def _mm(a, b, prec=None):
    return lax.dot_general(a, b, (((1,), (0,)), ((), ())), precision=prec,
                           preferred_element_type=F32)


def _mm_nt(a, b, prec=None):
    return lax.dot_general(a, b, (((1,), (1,)), ((), ())), precision=prec,
                           preferred_element_type=F32)


def _mm_tn(a, b, prec=None):
    return lax.dot_general(a, b, (((0,), (0,)), ((), ())), precision=prec,
                           preferred_element_type=F32)


def _sigmoid(x):
    return 1.0 / (1.0 + jnp.exp(-x))


def _silu(x):
    return x * _sigmoid(x)


def _softplus(x):
    return jnp.maximum(x, 0.0) + jnp.log1p(jnp.exp(-jnp.abs(x)))


def _rms(x, g):
    return x * lax.rsqrt(jnp.mean(x * x, axis=-1, keepdims=True) + EPS) * g


def _params(*sem):
    return pltpu.CompilerParams(dimension_semantics=sem, vmem_limit_bytes=VMEM_LIMIT)


def _norm_matmul_kernel(x_ref, g_ref, w_ref, o_ref, h_ref):
    @pl.when(pl.program_id(1) == 0)
    def _():
        h_ref[...] = _rms(x_ref[...], g_ref[...]).astype(BF16)

    o_ref[...] = _mm(h_ref[...], w_ref[...])


def norm_matmul(x, g, w, tm, tn):
    n, k = x.shape
    m = w.shape[1]
    return pl.pallas_call(
        _norm_matmul_kernel,
        grid=(n // tm, m // tn),
        in_specs=[pl.BlockSpec((tm, k), lambda i, j: (i, 0)),
                  pl.BlockSpec((1, k), lambda i, j: (0, 0)),
                  pl.BlockSpec((k, tn), lambda i, j: (0, j))],
        out_specs=pl.BlockSpec((tm, tn), lambda i, j: (i, j)),
        out_shape=jax.ShapeDtypeStruct((n, m), F32),
        scratch_shapes=[pltpu.VMEM((tm, k), BF16)],
        compiler_params=_params("arbitrary", "arbitrary"),
        name="norm_matmul",
    )(x, g.reshape(1, k), w)


def _matmul_res_kernel(y_ref, w_ref, x_ref, g_ref, o_ref, *, nk, final_norm):
    kk = pl.program_id(1)
    d = _mm(y_ref[...], w_ref[...])

    @pl.when(kk == 0)
    def _():
        o_ref[...] = x_ref[...] + d

    @pl.when(kk != 0)
    def _():
        o_ref[...] += d

    if final_norm:
        @pl.when(kk == nk - 1)
        def _():
            o_ref[...] = _rms(o_ref[...], g_ref[...])


def matmul_residual(y, w, x, g, tm, tk, final_norm):
    n, k = y.shape
    d = w.shape[1]
    nk = k // tk
    return pl.pallas_call(
        functools.partial(_matmul_res_kernel, nk=nk, final_norm=final_norm),
        grid=(n // tm, nk),
        in_specs=[pl.BlockSpec((tm, tk), lambda i, kk: (i, kk)),
                  pl.BlockSpec((tk, d), lambda i, kk: (kk, 0)),
                  pl.BlockSpec((tm, d), lambda i, kk: (i, 0)),
                  pl.BlockSpec((1, d), lambda i, kk: (0, 0))],
        out_specs=pl.BlockSpec((tm, d), lambda i, kk: (i, 0)),
        out_shape=jax.ShapeDtypeStruct((n, d), F32),
        compiler_params=_params("arbitrary", "arbitrary"),
        name="matmul_residual",
    )(y, w, x, g.reshape(1, d))


def _ffn_up_kernel(x_ref, g_ref, wa_ref, wu_ref, cw_ref, cb_ref, st_ref, act_ref, tail_ref,
                   h_ref, ext_ref, carry_ref, *, nseq, rows, tiles_per_seq):
    i = pl.program_id(0)
    j = pl.program_id(1)
    halo = CONV_FFN - 1

    @pl.when(j == 0)
    def _():
        h_ref[...] = _rms(x_ref[...], g_ref[...]).astype(BF16)

    if tiles_per_seq > 1:
        @pl.when(i == 0)
        def _():
            carry_ref[j] = jnp.zeros(carry_ref.shape[1:], F32)

    h = h_ref[...]
    a = _mm(h, wa_ref[...])
    u = _mm(h, wu_ref[...])
    cw = cw_ref[...]
    cb = cb_ref[...]
    for s in range(nseq):
        lo = s * rows
        ext_ref[s, SUBLANES:, :] = a[lo:lo + rows]
        prev = st_ref[s]
        if tiles_per_seq > 1:
            prev = jnp.where((i % tiles_per_seq) == 0, prev, carry_ref[j, SUBLANES - halo:, :])
        ext_ref[s, SUBLANES - halo:SUBLANES, :] = prev
        conv = cb
        for t in range(CONV_FFN):
            off = SUBLANES - halo + t
            conv = conv + ext_ref[s, off:off + rows, :] * cw[t:t + 1]
        act_ref[lo:lo + rows, :] = (_silu(conv) * u[lo:lo + rows]).astype(BF16)
        tail_ref[s] = ext_ref[s, SUBLANES + rows - halo:SUBLANES + rows, :]
        carry_ref[j] = ext_ref[s, rows:rows + SUBLANES, :]


def ffn_up(x, g, w_up, conv_w, conv_b, state, batch, seq, tm, tf):
    n, k = x.shape
    nff = D_FF // tf
    if tm >= seq:
        nseq, rows, tiles_per_seq = tm // seq, seq, 1
    else:
        nseq, rows, tiles_per_seq = 1, tm, seq // tm
    ntiles = n // tm
    halo = CONV_FFN - 1
    act, tail = pl.pallas_call(
        functools.partial(_ffn_up_kernel, nseq=nseq, rows=rows, tiles_per_seq=tiles_per_seq),
        grid=(ntiles, nff),
        in_specs=[pl.BlockSpec((tm, k), lambda i, j: (i, 0)),
                  pl.BlockSpec((1, k), lambda i, j: (0, 0)),
                  pl.BlockSpec((k, tf), lambda i, j: (0, j)),
                  pl.BlockSpec((k, tf), lambda i, j: (0, j + nff)),
                  pl.BlockSpec((CONV_FFN, tf), lambda i, j: (0, j)),
                  pl.BlockSpec((1, tf), lambda i, j: (0, j)),
                  pl.BlockSpec((nseq, halo, tf), lambda i, j: (i // tiles_per_seq, 0, j))],
        out_specs=[pl.BlockSpec((tm, tf), lambda i, j: (i, j)),
                   pl.BlockSpec((nseq, halo, tf), lambda i, j: (i, 0, j))],
        out_shape=[jax.ShapeDtypeStruct((n, D_FF), BF16),
                   jax.ShapeDtypeStruct((ntiles * nseq, halo, D_FF), F32)],
        scratch_shapes=[pltpu.VMEM((tm, k), BF16),
                        pltpu.VMEM((nseq, SUBLANES + rows, tf), F32),
                        pltpu.VMEM((nff, SUBLANES, tf), F32)],
        compiler_params=_params("arbitrary", "arbitrary"),
        name="ffn_up",
    )(x, g.reshape(1, k), w_up, w_up, conv_w, conv_b.reshape(1, D_FF), state)
    new_state = tail.reshape(batch, -1, halo, D_FF)[:, -1]
    return act, new_state


def _iota2(shape, dim):
    return lax.broadcasted_iota(jnp.int32, shape, dim)


def _tril(n):
    return (_iota2((n, n), 1) <= _iota2((n, n), 0)).astype(F32)


def _lane_col(x, lane):
    return jnp.sum(jnp.where(_iota2(x.shape, 1) == lane, x, 0.0), axis=-1, keepdims=True)


def _bf(x):
    return x.astype(BF16)


def _split3(x):
    h1 = _bf(x)
    r1 = x - h1.astype(F32)
    h2 = _bf(r1)
    h3 = _bf(r1 - h2.astype(F32))
    return h1, h2, h3


def _mm_sel(sel, x):
    n = x.shape[1]
    y = _mm(_bf(sel), jnp.concatenate(_split3(x), axis=1))
    return y[:, :n] + y[:, n:2 * n] + y[:, 2 * n:]


def _transpose_rows(x):
    r = x.shape[0]
    pad = -r % LANES
    if pad:
        x = jnp.concatenate([x, jnp.zeros((pad, x.shape[1]), x.dtype)], axis=0)
    return x.T


def _chunk_cumsum_rows(row, tri):
    r = row.shape[1]
    h1, h2, h3 = (p.astype(F32) for p in _split3(row))
    sub = _iota2((2 * SUBLANES, r), 0)
    parts = jnp.where(sub == 0, h1, jnp.where(sub == 1, h2, jnp.where(sub == 2, h3, 0.0)))
    y = _mm(_bf(parts), tri)
    return y[0:1] + y[1:2] + y[2:3]


def _row_to_col(row):
    n = row.shape[1]
    eye = _iota2((n, n), 0) == _iota2((n, n), 1)
    return jnp.sum(jnp.where(eye, row, 0.0), axis=-1, keepdims=True)


def _chunk_tri(tb, chunk):
    s = lax.broadcasted_iota(jnp.int32, (tb, tb), 0)
    t = lax.broadcasted_iota(jnp.int32, (tb, tb), 1)
    return ((s <= t) & (s // chunk == t // chunk)).astype(BF16)


def _hgrn_kernel(q_ref, f_ref, i_ref, g_ref, lb_ref, gn_ref, s0_ref, o_ref, s_out_ref, s_ref,
                 *, L, n_chunks, lb_index):
    tb = pl.program_id(2)

    @pl.when(tb == 0)
    def _():
        s_ref[...] = s0_ref[0, 0]

    lbp = lb_ref[...]
    e = jnp.exp(lbp - jnp.max(lbp, axis=0, keepdims=True))
    lb = jnp.sum(e[:lb_index + 1], axis=0, keepdims=True) / jnp.sum(e, axis=0, keepdims=True)

    sizes = [L >> l for l in range(int(math.log2(L)))]
    ti = _iota2((L, L), 0)
    si = _iota2((L, L), 1)
    tcol = _iota2((L, 1), 0)
    mats = [si <= ti]
    for sz in sizes:
        mats.append(si <= ((ti & ~(sz - 1)) + (sz // 2 - 1)))
    cum_mat = jnp.concatenate([m.astype(F32) for m in mats], axis=0)
    eye = ti == si
    gn = gn_ref[...]

    chunks = []
    for c in range(n_chunks):
        rows = pl.ds(c * L, L)
        q = q_ref[rows, :]
        f = lb + (1.0 - lb) * _sigmoid(f_ref[rows, :])
        k = 1.0 - f
        cums = _mm_sel(cum_mat, jnp.log(f))
        chunks.append(dict(rows=rows, q=q, k=k, cums=cums, b=cums[:L], vb=_bf(i_ref[rows, :]),
                           scores=jnp.where(eye, jnp.sum(q * k, axis=-1, keepdims=True), 0.0)))
    for l, sz in enumerate(sizes):
        right = (tcol & (sz - 1)) >= sz // 2
        same = (ti & ~(sz - 1)) == (si & ~(sz - 1))
        for ch in chunks:
            w = jnp.exp(-jnp.abs(ch['b'] - ch['cums'][(l + 1) * L:(l + 2) * L]))
            qm = jnp.where(right, ch['q'] * w, 0.0)
            km = jnp.where(right, 0.0, ch['k'] * w)
            ch['scores'] = ch['scores'] + jnp.where(same, _mm_nt(_bf(qm), _bf(km)), 0.0)
    for ch in chunks:
        bl = ch['b'][L - 1:L]
        ch['update'] = _mm_tn(_bf(ch['k'] * jnp.exp(bl - ch['b'])), ch['vb'])
        ch['decay'] = _row_to_col(jnp.exp(bl))
    for ch in chunks:
        s = s_ref[...]
        o = _mm(jnp.concatenate([_bf(ch['q'] * jnp.exp(ch['b'])), _bf(ch['scores'])], axis=1),
                jnp.concatenate([_bf(s), ch['vb']], axis=0))
        s_ref[...] = ch['decay'] * s + ch['update']
        o_ref[ch['rows'], :] = (_rms(o, gn) * _silu(g_ref[ch['rows'], :])).astype(BF16)

    @pl.when(tb == pl.num_programs(2) - 1)
    def _():
        s_out_ref[0, 0] = s_ref[...]


def hgrn2(z, lower_bounds, norm_g, s0, batch, seq, tb, lb_index):
    L = min(MIX_CHUNK, seq)
    nt = seq // tb

    def col(c0):
        return pl.BlockSpec((tb, LANES), lambda b, h, t: (b * nt + t, c0 + h))

    st_spec = pl.BlockSpec((1, 1, DK_A, DV_A), lambda b, h, t: (b, h, 0, 0))
    return pl.pallas_call(
        functools.partial(_hgrn_kernel, L=L, n_chunks=tb // L, lb_index=lb_index),
        grid=(batch, H_A, nt),
        in_specs=[col(AB_Q), col(AB_F), col(AB_I), col(AB_G),
                  pl.BlockSpec((lower_bounds.shape[0], LANES), lambda b, h, t: (0, h)),
                  pl.BlockSpec((1, DV_A), lambda b, h, t: (0, 0)),
                  st_spec],
        out_specs=[pl.BlockSpec((tb, DV_A), lambda b, h, t: (b * nt + t, h)), st_spec],
        out_shape=[jax.ShapeDtypeStruct((batch * seq, H_A * DV_A), BF16),
                   jax.ShapeDtypeStruct((batch, H_A, DK_A, DV_A), F32)],
        scratch_shapes=[pltpu.VMEM((DK_A, DV_A), F32)],
        compiler_params=_params("arbitrary", "arbitrary", "arbitrary"),
        name="hgrn2",
    )(z, z, z, z, lower_bounds, norm_g.reshape(1, DV_A), s0)


def _mlstm_kernel(q_ref, k_ref, v_ref, og_ref, gt_ref, bias_ref, tri_ref, gn_ref, c0_ref, n0_ref,
                  m0_ref, o_ref, c_out_ref, n_out_ref, m_out_ref, c_ref, n_ref, m_ref, gtt_ref,
                  *, L, n_chunks):
    h = pl.program_id(1)
    tb = pl.program_id(2)

    @pl.when(tb == 0)
    def _():
        c_ref[...] = c0_ref[0, 0]
        n_ref[...] = n0_ref[0, 0]
        m_ref[...] = m0_ref[0, 0]

    n_rows = L * n_chunks
    gtt_ref[...] = _transpose_rows(gt_ref[...] + bias_ref[...])
    ig_rows = gtt_ref[pl.ds(h, 1), :][:, :n_rows]
    f_rows = gtt_ref[pl.ds(H_B + h, 1), :][:, :n_rows]
    logf = jnp.minimum(f_rows, 0.0) - jnp.log1p(jnp.exp(-jnp.abs(f_rows)))
    b_rows = _chunk_cumsum_rows(logf, tri_ref[...])

    causal = _iota2((L, L), 1) <= _iota2((L, L), 0)
    gn = gn_ref[...]
    scale = DK_B ** -0.5

    chunks = []
    for c in range(n_chunks):
        rows = pl.ds(c * L, L)
        q = q_ref[rows, :] * scale
        k = k_ref[rows, :]
        vb = _bf(v_ref[rows, :])
        ig_row = ig_rows[:, c * L:(c + 1) * L]
        b_row = b_rows[:, c * L:(c + 1) * L]
        b_col = _row_to_col(b_row)
        d = b_col - b_row + ig_row
        d_max = jnp.max(jnp.where(causal, d, -jnp.inf), axis=-1, keepdims=True)
        p = jnp.exp(jnp.where(causal, d - d_max, -jnp.inf)) * _mm_nt(_bf(q), _bf(k))
        b_last = b_row[:, L - 1:L]
        wk_max = jnp.max(b_last - b_row + ig_row, axis=-1, keepdims=True)
        ks = jnp.exp(b_last - b_col + _row_to_col(ig_row) - wk_max) * k
        chunks.append(dict(rows=rows, q=q, b_col=b_col, d_max=d_max, b_last=b_last, wk_max=wk_max,
                           p_sum=jnp.sum(p, axis=-1, keepdims=True), pv=_mm(_bf(p), vb),
                           k_sum=jnp.sum(ks, axis=0, keepdims=True), kv=_mm_tn(_bf(ks), vb)))

    for ch in chunks:
        q = ch['q']
        cs = c_ref[...]
        n = n_ref[...]
        m_prev = m_ref[...][:, 0:1]
        inter = ch['b_col'] + m_prev
        m_t = jnp.maximum(ch['d_max'], inter)
        w_intra = jnp.exp(ch['d_max'] - m_t)
        w_inter = jnp.exp(inter - m_t)
        num = w_intra * ch['pv'] + w_inter * _mm(_bf(q), _bf(cs))
        den = w_intra * ch['p_sum'] + w_inter * jnp.sum(q * n, axis=-1, keepdims=True)
        hid = num / jnp.maximum(jnp.abs(den), jnp.exp(-m_t))
        m_new = jnp.maximum(ch['b_last'] + m_prev, ch['wk_max'])
        w_new = jnp.exp(ch['wk_max'] - m_new)
        carry_w = jnp.exp(ch['b_last'] + m_prev - m_new)
        c_ref[...] = carry_w * cs + w_new * ch['kv']
        n_ref[...] = carry_w * n + w_new * ch['k_sum']
        m_ref[...] = jnp.broadcast_to(m_new, (1, LANES))
        o_ref[ch['rows'], :] = (_rms(hid, gn) * _sigmoid(og_ref[ch['rows'], :])).astype(BF16)

    @pl.when(tb == pl.num_programs(2) - 1)
    def _():
        c_out_ref[0, 0] = c_ref[...]
        n_out_ref[0, 0] = n_ref[...]
        m_out_ref[0, 0] = m_ref[...]


def mlstm(z, i_bias, f_bias, norm_g, c0, n0, m0, batch, seq, tb):
    L = min(MIX_CHUNK, seq)
    nt = seq // tb
    bias = jnp.zeros((1, LANES), F32).at[0, :H_B].set(i_bias).at[0, H_B:2 * H_B].set(f_bias)

    def col(c0_, width):
        return pl.BlockSpec((tb, width), lambda b, h, t: (b * nt + t, c0_ + h))

    def state(r, c):
        return pl.BlockSpec((1, 1, r, c), lambda b, h, t: (b, h, 0, 0))

    o, cs, n, m = pl.pallas_call(
        functools.partial(_mlstm_kernel, L=L, n_chunks=tb // L),
        grid=(batch, H_B, nt),
        in_specs=[col(AB_BQ, DK_B), col(AB_BK, DK_B), col(AB_BV, DV_B), col(AB_BO, DV_B),
                  pl.BlockSpec((tb, LANES), lambda b, h, t: (b * nt + t, AB_GATES)),
                  pl.BlockSpec((1, LANES), lambda b, h, t: (0, 0)),
                  pl.BlockSpec((tb, tb), lambda b, h, t: (0, 0)),
                  pl.BlockSpec((1, DV_B), lambda b, h, t: (0, 0)),
                  state(DK_B, DV_B), state(1, DK_B), state(1, LANES)],
        out_specs=[pl.BlockSpec((tb, DV_B), lambda b, h, t: (b * nt + t, h)),
                   state(DK_B, DV_B), state(1, DK_B), state(1, LANES)],
        out_shape=[jax.ShapeDtypeStruct((batch * seq, H_B * DV_B), BF16),
                   jax.ShapeDtypeStruct((batch, H_B, DK_B, DV_B), F32),
                   jax.ShapeDtypeStruct((batch, H_B, 1, DK_B), F32),
                   jax.ShapeDtypeStruct((batch, H_B, 1, LANES), F32)],
        scratch_shapes=[pltpu.VMEM((DK_B, DV_B), F32), pltpu.VMEM((1, DK_B), F32),
                        pltpu.VMEM((1, LANES), F32),
                        pltpu.VMEM((LANES, tb + (-tb % LANES)), F32)],
        compiler_params=_params("arbitrary", "arbitrary", "arbitrary"),
        name="mlstm",
    )(z, z, z, z, z, bias, _chunk_tri(tb, L), norm_g.reshape(1, DV_B), c0,
      n0.reshape(batch, H_B, 1, DK_B),
      jnp.broadcast_to(m0.reshape(batch, H_B, 1, 1), (batch, H_B, 1, LANES)))
    return o, cs, n.reshape(batch, H_B, DK_B), m[:, :, 0, 0]


def _gdn_kernel(q_ref, k_ref, v_ref, z_ref, gt_ref, cw_ref, cs_ref, gp_ref, tri_ref, gn_ref, s0_ref,
                o_ref, s_out_ref, st_ref, ext_ref, cv_ref, lgt_ref, *, L, n_chunks, tb):
    h = pl.program_id(1)
    t = pl.program_id(2)
    halo = CONV_C - 1

    @pl.when(t == 0)
    def _():
        st_ref[...] = s0_ref[0, 0]
        for p in range(3):
            ext_ref[p, SUBLANES - halo:SUBLANES, :] = cs_ref[p, 0]

    srcs = (q_ref, k_ref, v_ref)
    for p in range(3):
        ext_ref[p, SUBLANES:, :] = srcs[p][...]
        cw = cw_ref[p]
        acc = None
        for j in range(CONV_C):
            off = SUBLANES - halo + j
            term = ext_ref[p, off:off + tb, :] * cw[j:j + 1]
            acc = term if acc is None else acc + term
        cv_ref[p] = _silu(acc)
    ti = _iota2((L, L), 0)
    si = _iota2((L, L), 1)
    gp = gp_ref[...]
    gn = gn_ref[...]

    lgt_ref[...] = _transpose_rows(-jnp.exp(gp[0:1]) * _softplus(gt_ref[...] + gp[1:2]))
    b_rows = _chunk_cumsum_rows(lgt_ref[pl.ds(H_C + h, 1), :][:, :tb], tri_ref[...])

    def l2n(x):
        return x * lax.rsqrt(jnp.sum(x * x, axis=-1, keepdims=True) + EPS)

    chunks = []
    for c in range(n_chunks):
        rows = pl.ds(c * L, L)
        q = l2n(cv_ref[0, rows, :]) * (DK_C ** -0.5)
        k = l2n(cv_ref[1, rows, :])
        v = cv_ref[2, rows, :]
        beta = _lane_col(_sigmoid(gt_ref[rows, :]), h)
        b_row = b_rows[:, c * L:(c + 1) * L]
        b_col = _row_to_col(b_row)
        dec_incl = jnp.exp(jnp.where(si <= ti, b_col - b_row, -jnp.inf))
        eb = jnp.exp(b_col)
        kb = _bf(k)
        qb = _bf(q)
        kq = _mm_nt(jnp.concatenate([kb, qb], axis=0), kb)
        b_last = b_row[:, L - 1:L]
        chunks.append(dict(
            rows=rows, qb=qb, eb=eb, decay=jnp.exp(b_last),
            pw=-(beta * kq[:L] * jnp.where(si < ti, dec_incl, 0.0)),
            attn=_bf(kq[L:] * dec_incl),
            x=jnp.concatenate([beta * v, (beta * eb) * k], axis=1),
            k_dec=_bf(k * jnp.exp(b_last - b_col))))

    eye = jnp.where(ti == si, 1.0, 0.0)
    n_steps = int(math.log2(L))
    for ch in chunks:
        ch['inv'] = eye + ch['pw']
        pwb = _bf(ch['pw'])
        ch['pw'] = _mm(pwb, pwb)
    for step in range(1, n_steps):
        for ch in chunks:
            inv, pw = ch['inv'], ch['pw']
            if step == n_steps - 1:
                ch['inv'] = inv + _mm(_bf(pw), _bf(inv))
            else:
                y = _mm(_bf(pw), jnp.concatenate([_bf(inv), _bf(pw)], axis=1))
                ch['inv'] = inv + y[:, :L]
                ch['pw'] = y[:, L:]
    for ch in chunks:
        ch['x'] = _mm(_bf(ch['inv']), _bf(ch['x']))

    for ch in chunks:
        w = ch['x'][:, :DV_C]
        u = ch['x'][:, DV_C:]
        st = st_ref[...]
        uq = _mm_nt(jnp.concatenate([_bf(u), ch['qb']], axis=0), _bf(st))
        db = _bf(w - uq[:L])
        o = ch['eb'] * uq[L:] + _mm(ch['attn'], db)
        st_ref[...] = ch['decay'] * st + _mm_tn(db, ch['k_dec'])
        o_ref[ch['rows'], :] = (_rms(o, gn) * _silu(z_ref[ch['rows'], :])).astype(BF16)

    for p in range(3):
        ext_ref[p, 0:SUBLANES, :] = ext_ref[p, tb:tb + SUBLANES, :]

    @pl.when(t == pl.num_programs(2) - 1)
    def _():
        s_out_ref[0, 0] = st_ref[...]


def gdn(z, conv_w, conv_state, a_log, dt_bias, norm_g, s0, batch, seq, tb):
    L = min(CHUNK, seq)
    nt = seq // tb
    halo = CONV_C - 1
    gate_params = (jnp.zeros((2, LANES), F32).at[0, H_C:2 * H_C].set(a_log)
                   .at[1, H_C:2 * H_C].set(dt_bias))
    cw = conv_w.reshape(CONV_C, 3, H_C * DK_C).transpose(1, 0, 2)
    cs = conv_state.reshape(batch, halo, 3, H_C * DK_C).transpose(2, 0, 1, 3)

    def col(c0):
        return pl.BlockSpec((tb, LANES), lambda b, h, t: (b * nt + t, c0 + h))

    st_spec = pl.BlockSpec((1, 1, DV_C, DK_C), lambda b, h, t: (b, h, 0, 0))
    o, st = pl.pallas_call(
        functools.partial(_gdn_kernel, L=L, n_chunks=tb // L, tb=tb),
        grid=(batch, H_C, nt),
        in_specs=[col(CD_Q), col(CD_K), col(CD_V), col(CD_Z),
                  pl.BlockSpec((tb, LANES), lambda b, h, t: (b * nt + t, CD_GATES)),
                  pl.BlockSpec((3, CONV_C, LANES), lambda b, h, t: (0, 0, h)),
                  pl.BlockSpec((3, 1, halo, LANES), lambda b, h, t: (0, b, 0, h)),
                  pl.BlockSpec((2, LANES), lambda b, h, t: (0, 0)),
                  pl.BlockSpec((tb, tb), lambda b, h, t: (0, 0)),
                  pl.BlockSpec((1, DV_C), lambda b, h, t: (0, 0)),
                  st_spec],
        out_specs=[pl.BlockSpec((tb, DV_C), lambda b, h, t: (b * nt + t, h)), st_spec],
        out_shape=[jax.ShapeDtypeStruct((batch * seq, H_C * DV_C), BF16),
                   jax.ShapeDtypeStruct((batch, H_C, DV_C, DK_C), F32)],
        scratch_shapes=[pltpu.VMEM((DV_C, DK_C), F32),
                        pltpu.VMEM((3, SUBLANES + tb, LANES), F32),
                        pltpu.VMEM((3, tb, LANES), F32),
                        pltpu.VMEM((LANES, tb + (-tb % LANES)), F32)],
        compiler_params=_params("arbitrary", "arbitrary", "arbitrary"),
        name="gdn",
    )(z, z, z, z, z, cw, cs, gate_params, _chunk_tri(tb, L), norm_g.reshape(1, DV_C),
      jnp.swapaxes(s0, 2, 3))
    return o, jnp.swapaxes(st, 2, 3)


def _attn_kernel(q_ref, k_ref, v_ref, lp_ref, gn_ref, o_ref, kb_ref, vb_ref, m_ref, acc_ref,
                 *, seq, kv_len, tq, tk, lam_init):
    h = pl.program_id(1)
    qi = pl.program_id(2)
    past = kv_len - seq
    shift = int(math.log2(CHUNK))

    @pl.when(qi == 0)
    def _():
        kb_ref[...] = _bf(k_ref[...])
        vb_ref[:, :DV_D] = _bf(v_ref[...])
        vb_ref[:, DV_D:] = jnp.ones((kv_len, LANES), BF16)

    lp = lp_ref[...]
    lam = (jnp.exp(jnp.sum(lp[0:1] * lp[1:2], axis=-1, keepdims=True))
           - jnp.exp(jnp.sum(lp[2:3] * lp[3:4], axis=-1, keepdims=True)) + lam_init)
    hf = jnp.full((1, 1), h + 1, jnp.int32).astype(F32)
    slope = jnp.exp(hf * (-8.0 / H_D * math.log(2.0)))

    q = q_ref[...] * (D_HD ** -0.5)
    first = _iota2((tq, 2 * D_HD), 1) < D_HD
    q_maps = (_bf(jnp.where(first, q, 0.0)), _bf(jnp.where(first, 0.0, q)))
    q_start = past + qi * tq
    q_pos = q_start + _iota2((tq, 1), 0)
    q_chunk = q_pos >> shift
    q_shift = slope * q_pos.astype(F32)

    m_ref[...] = jnp.full(m_ref.shape, NEG_BIG, F32)
    acc_ref[...] = jnp.zeros(acc_ref.shape, F32)

    last_allowed = (((q_start + tq - 1) >> shift) << shift) + (CHUNK - 1)
    n_blocks = jnp.minimum(last_allowed, kv_len - 1) // tk + 1
    n_before = jnp.minimum((q_start + 1) // tk, n_blocks)

    def update(c, s, p_mask, vs):
        m_old = m_ref[c]
        s_max = s[:, :LANES]
        for g in range(1, s.shape[1] // LANES):
            s_max = jnp.maximum(s_max, s[:, g * LANES:(g + 1) * LANES])
        if s.shape[1] % LANES:
            tail = s[:, (s.shape[1] // LANES) * LANES:]
            m_new = jnp.maximum(jnp.max(s_max, axis=-1, keepdims=True),
                                jnp.max(tail, axis=-1, keepdims=True))
        else:
            m_new = jnp.max(s_max, axis=-1, keepdims=True)
        m_new = jnp.maximum(m_old, m_new)
        p = jnp.exp(s - m_new)
        if p_mask is not None:
            p = jnp.where(p_mask, p, 0.0)
        acc_ref[c] = jnp.exp(m_old - m_new) * acc_ref[c] + _mm(_bf(p), vs)
        m_ref[c] = m_new

    def block_before(j, carry):
        rows = pl.ds(pl.multiple_of(j * tk, 2 * SUBLANES), tk)
        ks = kb_ref[rows, :]
        vs = vb_ref[rows, :]
        k_bias = slope * (j * tk + _iota2((1, tk), 1)).astype(F32)
        for c in range(2):
            update(c, _mm_nt(q_maps[c], ks) + k_bias, None, vs)
        return carry

    def block_masked(j, carry):
        rows = pl.ds(pl.multiple_of(j * tk, 2 * SUBLANES), tk)
        ks = kb_ref[rows, :]
        vs = vb_ref[rows, :]
        k_pos = j * tk + _iota2((1, tk), 1)
        allowed = (k_pos >> shift) <= q_chunk
        bias = q_shift - slope * jnp.abs(q_pos - k_pos).astype(F32)
        for c in range(2):
            s = jnp.where(allowed, _mm_nt(q_maps[c], ks) + bias, NEG_BIG)
            update(c, s, allowed, vs)
        return carry

    lax.fori_loop(0, n_before, block_before, 0)
    lax.fori_loop(n_before, n_blocks, block_masked, 0)
    a0 = acc_ref[0]
    a1 = acc_ref[1]
    o = (a0[:, :DV_D] / a0[:, DV_D:DV_D + 1] - lam * (a1[:, :DV_D] / a1[:, DV_D:DV_D + 1]))
    o_ref[...] = (_rms(o, gn_ref[...]) * (1.0 - lam_init)).astype(BF16)


def diff_attention(zq, q_col, kd, k_col, vd, v_col, lam_params, norm_g, batch, seq, kv_len,
                   tq, tk, lam_init):
    nq = seq // tq
    return pl.pallas_call(
        functools.partial(_attn_kernel, seq=seq, kv_len=kv_len, tq=tq, tk=tk, lam_init=lam_init),
        grid=(batch, H_D, nq),
        in_specs=[pl.BlockSpec((tq, DV_D), lambda b, h, i: (b * nq + i, q_col + h)),
                  pl.BlockSpec((kv_len, DV_D), lambda b, h, i: (b, k_col + h)),
                  pl.BlockSpec((kv_len, DV_D), lambda b, h, i: (b, v_col + h)),
                  pl.BlockSpec((4, D_HD), lambda b, h, i: (0, 0)),
                  pl.BlockSpec((1, DV_D), lambda b, h, i: (0, 0))],
        out_specs=pl.BlockSpec((tq, DV_D), lambda b, h, i: (b * nq + i, h)),
        out_shape=jax.ShapeDtypeStruct((batch * seq, H_D * DV_D), BF16),
        scratch_shapes=[pltpu.VMEM((kv_len, DV_D), BF16), pltpu.VMEM((kv_len, DV_D + LANES), BF16),
                        pltpu.VMEM((2, tq, 1), F32), pltpu.VMEM((2, tq, DV_D + LANES), F32)],
        compiler_params=_params("arbitrary", "arbitrary", "arbitrary"),
        name="diff_attention",
    )(zq, kd, vd, lam_params, norm_g.reshape(1, DV_D))


def _pad_cols(w, total):
    return jnp.pad(w, ((0, 0), (0, total - w.shape[1])))


def _prep_w_in_ab(w):
    sizes = (H_A * DK_A, H_A * DK_A, H_A * DV_A, H_A * DV_A, H_B * DK_B, H_B * DK_B, H_B * DV_B,
             H_B, H_B, H_B * DV_B)
    aq, af, ai, ag, bq, bk, bv, big, bfg, bo = jnp.split(w, np_cumsum(sizes), axis=1)
    return _pad_cols(jnp.concatenate([aq, af, ai, ag, bq, bk, bv, bo, big, bfg], axis=1),
                     IN_AB_PAD).astype(BF16)


def _prep_w_in_cd(w):
    sizes = (H_C * (2 * DK_C + DV_C), H_C, H_C, H_C * DV_C, H_D * 2 * D_HD, H_D * 2 * D_HD,
             H_D * DV_D)
    cqkv, cb, ca, cz, dq, dk, dv = jnp.split(w, np_cumsum(sizes), axis=1)
    return _pad_cols(jnp.concatenate([cqkv, cz, dq, dk, dv, cb, ca], axis=1),
                     IN_CD_PAD).astype(BF16)


def np_cumsum(sizes):
    out, acc = [], 0
    for s in sizes[:-1]:
        acc += s
        out.append(acc)
    return out


def _tiles(n_rows, seq):
    tm = min(512, n_rows)
    tb = min(512, seq)
    return tm, tb


def _layer_ab(x, st, p, lower_bounds, batch, seq, layer):
    n = batch * seq
    tm, tb = _tiles(n, seq)
    z = norm_matmul(x, p['norm_mix'], p['w_in'], tm, IN_AB_PAD // 3)
    o_a, s_new = hgrn2(z, lower_bounds, p['hgrn_norm'], st['hgrn'], batch, seq, tb, layer // 2)
    o_b, c_new, n_new, m_new = mlstm(z, p['i_bias'], p['f_bias'], p['mlstm_norm'], st['mlstm_c'],
                                     st['mlstm_n'], st['mlstm_m'], batch, seq, tb)
    y = jnp.concatenate([o_a, o_b], axis=1)
    x = matmul_residual(y, p['w_out'], x, p['norm_ffn'], tm, 1024, False)
    return x, (s_new, c_new, n_new, m_new)


def _layer_cd(x, st, p, batch, seq, layer):
    n = batch * seq
    tm, tb = _tiles(n, seq)
    z = norm_matmul(x, p['norm_mix'], p['w_in'], tm, IN_CD_PAD // 3)
    o_c, s_new = gdn(z, p['conv_w'], st['gdn_conv'], p['a_log'], p['dt_bias'], p['gdn_norm'],
                     st['gdn'], batch, seq, tb)
    z3 = z.reshape(batch, seq, IN_CD_PAD)
    conv_new = z3[:, seq - (CONV_C - 1):, :H_C * (2 * DK_C + DV_C)]
    k_new = z3[:, :, CD_DK * LANES:CD_DV * LANES]
    v_new = z3[:, :, CD_DV * LANES:CD_GATES * LANES]
    past = st['k_cache'].shape[1]
    lam_init = 0.8 - 0.6 * math.exp(-0.3 * layer)
    if past == 0:
        o_d = diff_attention(z, CD_DQ, z, CD_DK, z, CD_DV, p['lam'], p['diff_norm'], batch, seq,
                             seq, min(512, seq), min(512, seq), lam_init)
    else:
        kv_len = past + seq
        kd = jnp.concatenate([st['k_cache'].reshape(batch, past, -1), k_new], axis=1)
        vd = jnp.concatenate([st['v_cache'].reshape(batch, past, -1), v_new], axis=1)
        o_d = diff_attention(z, CD_DQ, kd.reshape(batch * kv_len, -1), 0,
                             vd.reshape(batch * kv_len, -1), 0, p['lam'], p['diff_norm'], batch,
                             seq, kv_len, seq, kv_len, lam_init)
    y = jnp.concatenate([o_c, o_d], axis=1)
    x = matmul_residual(y, p['w_out'], x, p['norm_ffn'], tm, 1024, False)
    return x, (s_new, conv_new, k_new.reshape(batch, seq, H_D, 2 * D_HD),
               v_new.reshape(batch, seq, H_D, DV_D))


def _ffn(x, state, p, batch, seq, final_g):
    n = batch * seq
    tm, _ = _tiles(n, seq)
    act, conv_new = ffn_up(x, p['norm_ffn'], p['ffn_up'], p['ffn_conv_w'], p['ffn_conv_b'], state,
                           batch, seq, tm, 512)
    g = p['norm_ffn'] if final_g is None else final_g
    x = matmul_residual(act, p['ffn_down'], x, g, tm, D_FF // 4, final_g is not None)
    return x, conv_new


def _run_group(x, st, params, lower_bounds, norm_final):
    batch, seq, _ = x.shape
    x = x.reshape(batch * seq, D_MODEL)
    new_states = []
    for layer, (p, s) in enumerate(zip(params, st)):
        if layer % 2 == 0:
            x, new_mix = _layer_ab(x, s, p, lower_bounds, batch, seq, layer)
        else:
            x, new_mix = _layer_cd(x, s, p, batch, seq, layer)
        last = layer == len(params) - 1
        x, new_ffn = _ffn(x, s['ffn_conv'], p, batch, seq, norm_final if last else None)
        new_states.append(new_mix + (new_ffn,))
    return x.reshape(batch, seq, D_MODEL), new_states


def kernel(x_prompt, x_sample, state_hgrn_0, state_mlstm_c_0, state_mlstm_n_0, state_mlstm_m_0,
           state_ffn_conv_0, state_gdn_1, state_gdn_conv_1, cache_k_1, cache_v_1, state_ffn_conv_1,
           hgrn_lower_bounds, norm_mix_0, w_in_0, mlstm_i_bias_0, mlstm_f_bias_0, hgrn_norm_0,
           mlstm_norm_0, w_out_0, norm_ffn_0, ffn_up_0, ffn_conv_w_0, ffn_conv_b_0, ffn_down_0,
           norm_mix_1, w_in_1, gdn_conv_w_1, gdn_a_log_1, gdn_dt_bias_1, gdn_norm_1, diff_lambda_1,
           diff_norm_1, w_out_1, norm_ffn_1, ffn_up_1, ffn_conv_w_1, ffn_conv_b_1, ffn_down_1,
           norm_final):
    params = [
        dict(norm_mix=norm_mix_0, w_in=_prep_w_in_ab(w_in_0), i_bias=mlstm_i_bias_0,
             f_bias=mlstm_f_bias_0, hgrn_norm=hgrn_norm_0, mlstm_norm=mlstm_norm_0,
             w_out=w_out_0.astype(BF16), norm_ffn=norm_ffn_0, ffn_up=ffn_up_0.astype(BF16),
             ffn_conv_w=ffn_conv_w_0, ffn_conv_b=ffn_conv_b_0, ffn_down=ffn_down_0.astype(BF16)),
        dict(norm_mix=norm_mix_1, w_in=_prep_w_in_cd(w_in_1), conv_w=gdn_conv_w_1,
             a_log=gdn_a_log_1, dt_bias=gdn_dt_bias_1, gdn_norm=gdn_norm_1, lam=diff_lambda_1,
             diff_norm=diff_norm_1, w_out=w_out_1.astype(BF16), norm_ffn=norm_ffn_1,
             ffn_up=ffn_up_1.astype(BF16), ffn_conv_w=ffn_conv_w_1, ffn_conv_b=ffn_conv_b_1,
             ffn_down=ffn_down_1.astype(BF16)),
    ]
    bp = x_prompt.shape[0]
    st_prompt = [
        dict(hgrn=jnp.zeros((bp, H_A, DK_A, DV_A), F32), mlstm_c=jnp.zeros((bp, H_B, DK_B, DV_B), F32),
             mlstm_n=jnp.zeros((bp, H_B, DK_B), F32), mlstm_m=jnp.full((bp, H_B), NEG_BIG, F32),
             ffn_conv=jnp.zeros((bp, CONV_FFN - 1, D_FF), F32)),
        dict(gdn=jnp.zeros((bp, H_C, DK_C, DV_C), F32),
             gdn_conv=jnp.zeros((bp, CONV_C - 1, H_C * (2 * DK_C + DV_C)), F32),
             k_cache=jnp.zeros((bp, 0, H_D, 2 * D_HD), F32), v_cache=jnp.zeros((bp, 0, H_D, DV_D), F32),
             ffn_conv=jnp.zeros((bp, CONV_FFN - 1, D_FF), F32)),
    ]
    st_sample = [
        dict(hgrn=state_hgrn_0, mlstm_c=state_mlstm_c_0, mlstm_n=state_mlstm_n_0,
             mlstm_m=state_mlstm_m_0, ffn_conv=state_ffn_conv_0),
        dict(gdn=state_gdn_1, gdn_conv=state_gdn_conv_1, k_cache=cache_k_1, v_cache=cache_v_1,
             ffn_conv=state_ffn_conv_1),
    ]
    y_p, new_p = _run_group(x_prompt, st_prompt, params, hgrn_lower_bounds, norm_final)
    y_s, new_s = _run_group(x_sample, st_sample, params, hgrn_lower_bounds, norm_final)
    (hgrn_p, c_p, n_p, m_p, f0_p), (gdn_p, gc_p, k_p, v_p, f1_p) = new_p
    (hgrn_s, c_s, n_s, m_s, f0_s), (gdn_s, gc_s, k_s, v_s, f1_s) = new_s
    return (y_p, y_s, hgrn_p, hgrn_s, c_p, c_s, n_p, n_s, m_p, m_s, f0_p, f0_s, gdn_p, gdn_s,
            gc_p, gc_s, k_p, k_s, v_p, v_s, f1_p, f1_s)
```

```python
import functools
import math

import jax
import jax.numpy as jnp
from jax import lax
from jax.experimental import pallas as pl
from jax.experimental.pallas import tpu as pltpu

F32 = jnp.float32
BF16 = jnp.bfloat16
HI = lax.Precision.HIGHEST

D_MODEL = 2048
CHUNK = 64
H_A, DK_A, DV_A = 8, 128, 128
H_B, DK_B, DV_B = 4, 128, 256
H_C, DK_C, DV_C = 8, 128, 128
CONV_C = 4
H_D, D_HD = 8, 64
DV_D = 2 * D_HD
D_FF = 5632
CONV_FFN = 3
EPS = 1e-6
NEG_BIG = -1e30

LANES = 128
SUBLANES = 8
VMEM_LIMIT = 56 * 1024 * 1024
MIX_CHUNK = 128

AB_Q, AB_F, AB_I, AB_G = 0, 8, 16, 24
AB_BQ, AB_BK = 32, 36
AB_BV, AB_BO = 20, 24
AB_GATES = 56
IN_AB_PAD = 57 * LANES
CD_Q, CD_K, CD_V, CD_Z = 0, 8, 16, 24
CD_DQ, CD_DK, CD_DV = 32, 40, 48
CD_GATES = 56
IN_CD_PAD = 57 * LANES


def _mm(a, b, prec=None):
    return lax.dot_general(a, b, (((1,), (0,)), ((), ())), precision=prec,
                           preferred_element_type=F32)


def _mm_nt(a, b, prec=None):
    return lax.dot_general(a, b, (((1,), (1,)), ((), ())), precision=prec,
                           preferred_element_type=F32)


def _mm_tn(a, b, prec=None):
    return lax.dot_general(a, b, (((0,), (0,)), ((), ())), precision=prec,
                           preferred_element_type=F32)


def _sigmoid(x):
    return 1.0 / (1.0 + jnp.exp(-x))


def _silu(x):
    return x * _sigmoid(x)


def _softplus(x):
    return jnp.maximum(x, 0.0) + jnp.log1p(jnp.exp(-jnp.abs(x)))


def _rms(x, g):
    return x * lax.rsqrt(jnp.mean(x * x, axis=-1, keepdims=True) + EPS) * g


def _params(*sem):
    return pltpu.CompilerParams(dimension_semantics=sem, vmem_limit_bytes=VMEM_LIMIT)


def _norm_matmul_kernel(x_ref, g_ref, w_ref, o_ref, h_ref):
    @pl.when(pl.program_id(1) == 0)
    def _():
        h_ref[...] = _rms(x_ref[...], g_ref[...]).astype(BF16)

    o_ref[...] = _mm(h_ref[...], w_ref[...])


def norm_matmul(x, g, w, tm, tn):
    n, k = x.shape
    m = w.shape[1]
    return pl.pallas_call(
        _norm_matmul_kernel,
        grid=(n // tm, m // tn),
        in_specs=[pl.BlockSpec((tm, k), lambda i, j: (i, 0)),
                  pl.BlockSpec((1, k), lambda i, j: (0, 0)),
                  pl.BlockSpec((k, tn), lambda i, j: (0, j))],
        out_specs=pl.BlockSpec((tm, tn), lambda i, j: (i, j)),
        out_shape=jax.ShapeDtypeStruct((n, m), F32),
        scratch_shapes=[pltpu.VMEM((tm, k), BF16)],
        compiler_params=_params("arbitrary", "arbitrary"),
        name="norm_matmul",
    )(x, g.reshape(1, k), w)


def _matmul_res_kernel(*refs, n_parts, nk_part, final_norm):
    y_refs = refs[:n_parts]
    w_ref, x_ref, g_ref, o_ref = refs[n_parts:]
    kk = pl.program_id(1)

    for part, y_ref in enumerate(y_refs):
        @pl.when((kk >= part * nk_part) & (kk < (part + 1) * nk_part))
        def _(y_ref=y_ref):
            d = _mm(y_ref[...], w_ref[...])

            @pl.when(kk == 0)
            def _():
                o_ref[...] = x_ref[...] + d

            @pl.when(kk != 0)
            def _():
                o_ref[...] += d

    if final_norm:
        @pl.when(kk == n_parts * nk_part - 1)
        def _():
            o_ref[...] = _rms(o_ref[...], g_ref[...])


def matmul_residual(ys, w, x, g, tm, tk, final_norm):
    n, k_part = ys[0].shape
    d = w.shape[1]
    nk_part = k_part // tk
    n_parts = len(ys)

    def y_spec(part):
        return pl.BlockSpec(
            (tm, tk), lambda i, kk: (i, jnp.clip(kk - part * nk_part, 0, nk_part - 1)))

    return pl.pallas_call(
        functools.partial(_matmul_res_kernel, n_parts=n_parts, nk_part=nk_part,
                          final_norm=final_norm),
        grid=(n // tm, n_parts * nk_part),
        in_specs=[y_spec(part) for part in range(n_parts)]
        + [pl.BlockSpec((tk, d), lambda i, kk: (kk, 0)),
           pl.BlockSpec((tm, d), lambda i, kk: (i, 0)),
           pl.BlockSpec((1, d), lambda i, kk: (0, 0))],
        out_specs=pl.BlockSpec((tm, d), lambda i, kk: (i, 0)),
        out_shape=jax.ShapeDtypeStruct((n, d), F32),
        compiler_params=_params("arbitrary", "arbitrary"),
        name="matmul_residual",
    )(*ys, w, x, g.reshape(1, d))


def _ffn_up_kernel(x_ref, g_ref, wa_ref, wu_ref, cw_ref, cb_ref, st_ref, act_ref, tail_ref,
                   h_ref, ext_ref, carry_ref, *, nseq, rows, tiles_per_seq):
    i = pl.program_id(0)
    j = pl.program_id(1)
    halo = CONV_FFN - 1

    @pl.when(j == 0)
    def _():
        h_ref[...] = _rms(x_ref[...], g_ref[...]).astype(BF16)

    if tiles_per_seq > 1:
        @pl.when(i == 0)
        def _():
            carry_ref[j] = jnp.zeros(carry_ref.shape[1:], F32)

    h = h_ref[...]
    a = _mm(h, wa_ref[...])
    u = _mm(h, wu_ref[...])
    cw = cw_ref[...]
    cb = cb_ref[...]
    for s in range(nseq):
        lo = s * rows
        ext_ref[s, SUBLANES:, :] = a[lo:lo + rows]
        prev = st_ref[s]
        if tiles_per_seq > 1:
            prev = jnp.where((i % tiles_per_seq) == 0, prev, carry_ref[j, SUBLANES - halo:, :])
        ext_ref[s, SUBLANES - halo:SUBLANES, :] = prev
        conv = cb
        for t in range(CONV_FFN):
            off = SUBLANES - halo + t
            conv = conv + ext_ref[s, off:off + rows, :] * cw[t:t + 1]
        act_ref[lo:lo + rows, :] = (_silu(conv) * u[lo:lo + rows]).astype(BF16)
        tail_ref[s] = ext_ref[s, SUBLANES + rows - halo:SUBLANES + rows, :]
        carry_ref[j] = ext_ref[s, rows:rows + SUBLANES, :]


def ffn_up(x, g, w_up, conv_w, conv_b, state, batch, seq, tm, tf):
    n, k = x.shape
    nff = D_FF // tf
    if tm >= seq:
        nseq, rows, tiles_per_seq = tm // seq, seq, 1
    else:
        nseq, rows, tiles_per_seq = 1, tm, seq // tm
    ntiles = n // tm
    halo = CONV_FFN - 1
    act, tail = pl.pallas_call(
        functools.partial(_ffn_up_kernel, nseq=nseq, rows=rows, tiles_per_seq=tiles_per_seq),
        grid=(ntiles, nff),
        in_specs=[pl.BlockSpec((tm, k), lambda i, j: (i, 0)),
                  pl.BlockSpec((1, k), lambda i, j: (0, 0)),
                  pl.BlockSpec((k, tf), lambda i, j: (0, j)),
                  pl.BlockSpec((k, tf), lambda i, j: (0, j + nff)),
                  pl.BlockSpec((CONV_FFN, tf), lambda i, j: (0, j)),
                  pl.BlockSpec((1, tf), lambda i, j: (0, j)),
                  pl.BlockSpec((nseq, halo, tf), lambda i, j: (i // tiles_per_seq, 0, j))],
        out_specs=[pl.BlockSpec((tm, tf), lambda i, j: (i, j)),
                   pl.BlockSpec((nseq, halo, tf), lambda i, j: (i, 0, j))],
        out_shape=[jax.ShapeDtypeStruct((n, D_FF), BF16),
                   jax.ShapeDtypeStruct((ntiles * nseq, halo, D_FF), F32)],
        scratch_shapes=[pltpu.VMEM((tm, k), BF16),
                        pltpu.VMEM((nseq, SUBLANES + rows, tf), F32),
                        pltpu.VMEM((nff, SUBLANES, tf), F32)],
        compiler_params=_params("arbitrary", "arbitrary"),
        name="ffn_up",
    )(x, g.reshape(1, k), w_up, w_up, conv_w, conv_b.reshape(1, D_FF), state)
    new_state = tail.reshape(batch, -1, halo, D_FF)[:, -1]
    return act, new_state


def _iota2(shape, dim):
    return lax.broadcasted_iota(jnp.int32, shape, dim)


def _tril(n):
    return (_iota2((n, n), 1) <= _iota2((n, n), 0)).astype(F32)


def _lane_col(x, lane):
    return jnp.sum(jnp.where(_iota2(x.shape, 1) == lane, x, 0.0), axis=-1, keepdims=True)


def _bf(x):
    return x.astype(BF16)


def _split3(x):
    h1 = _bf(x)
    r1 = x - h1.astype(F32)
    h2 = _bf(r1)
    h3 = _bf(r1 - h2.astype(F32))
    return h1, h2, h3


def _mm_sel(sel, x):
    n = x.shape[1]
    y = _mm(_bf(sel), jnp.concatenate(_split3(x), axis=1))
    return y[:, :n] + y[:, n:2 * n] + y[:, 2 * n:]


def _transpose_rows(x):
    r = x.shape[0]
    pad = -r % LANES
    if pad:
        x = jnp.concatenate([x, jnp.zeros((pad, x.shape[1]), x.dtype)], axis=0)
    return x.T


def _chunk_cumsum_rows(row, tri):
    r = row.shape[1]
    h1, h2, h3 = (p.astype(F32) for p in _split3(row))
    sub = _iota2((2 * SUBLANES, r), 0)
    parts = jnp.where(sub == 0, h1, jnp.where(sub == 1, h2, jnp.where(sub == 2, h3, 0.0)))
    y = _mm(_bf(parts), tri)
    return y[0:1] + y[1:2] + y[2:3]


def _row_to_col(row):
    n = row.shape[1]
    eye = _iota2((n, n), 0) == _iota2((n, n), 1)
    return jnp.sum(jnp.where(eye, row, 0.0), axis=-1, keepdims=True)


def _chunk_tri(tb, chunk):
    s = lax.broadcasted_iota(jnp.int32, (tb, tb), 0)
    t = lax.broadcasted_iota(jnp.int32, (tb, tb), 1)
    return ((s <= t) & (s // chunk == t // chunk)).astype(BF16)


def _hgrn_kernel(q_ref, f_ref, i_ref, g_ref, lb_ref, gn_ref, s0_ref, o_ref, s_out_ref, s_ref,
                 *, L, n_chunks, lb_index):
    tb = pl.program_id(2)

    @pl.when(tb == 0)
    def _():
        s_ref[...] = s0_ref[0, 0]

    lbp = lb_ref[...]
    e = jnp.exp(lbp - jnp.max(lbp, axis=0, keepdims=True))
    lb = jnp.sum(e[:lb_index + 1], axis=0, keepdims=True) / jnp.sum(e, axis=0, keepdims=True)

    sizes = [L >> l for l in range(int(math.log2(L)))]
    ti = _iota2((L, L), 0)
    si = _iota2((L, L), 1)
    tcol = _iota2((L, 1), 0)
    mats = [si <= ti]
    for sz in sizes:
        mats.append(si <= ((ti & ~(sz - 1)) + (sz // 2 - 1)))
    cum_mat = jnp.concatenate([m.astype(F32) for m in mats], axis=0)
    eye = ti == si
    gn = gn_ref[...]

    chunks = []
    for c in range(n_chunks):
        rows = pl.ds(c * L, L)
        q = q_ref[rows, :]
        f = lb + (1.0 - lb) * _sigmoid(f_ref[rows, :])
        k = 1.0 - f
        cums = _mm_sel(cum_mat, jnp.log(f))
        chunks.append(dict(rows=rows, q=q, k=k, cums=cums, b=cums[:L], vb=_bf(i_ref[rows, :]),
                           scores=jnp.where(eye, jnp.sum(q * k, axis=-1, keepdims=True), 0.0)))
    for l, sz in enumerate(sizes):
        right = (tcol & (sz - 1)) >= sz // 2
        same = (ti & ~(sz - 1)) == (si & ~(sz - 1))
        for ch in chunks:
            w = jnp.exp(-jnp.abs(ch['b'] - ch['cums'][(l + 1) * L:(l + 2) * L]))
            qm = jnp.where(right, ch['q'] * w, 0.0)
            km = jnp.where(right, 0.0, ch['k'] * w)
            ch['scores'] = ch['scores'] + jnp.where(same, _mm_nt(_bf(qm), _bf(km)), 0.0)
    for ch in chunks:
        bl = ch['b'][L - 1:L]
        ch['update'] = _mm_tn(_bf(ch['k'] * jnp.exp(bl - ch['b'])), ch['vb'])
        ch['decay'] = _row_to_col(jnp.exp(bl))
    for ch in chunks:
        s = s_ref[...]
        o = _mm(jnp.concatenate([_bf(ch['q'] * jnp.exp(ch['b'])), _bf(ch['scores'])], axis=1),
                jnp.concatenate([_bf(s), ch['vb']], axis=0))
        s_ref[...] = ch['decay'] * s + ch['update']
        o_ref[ch['rows'], :] = (_rms(o, gn) * _silu(g_ref[ch['rows'], :])).astype(BF16)

    @pl.when(tb == pl.num_programs(2) - 1)
    def _():
        s_out_ref[0, 0] = s_ref[...]


def hgrn2(z, lower_bounds, norm_g, s0, batch, seq, tb, lb_index):
    L = min(MIX_CHUNK, seq)
    nt = seq // tb

    def col(c0):
        return pl.BlockSpec((tb, LANES), lambda b, h, t: (b * nt + t, c0 + h))

    st_spec = pl.BlockSpec((1, 1, DK_A, DV_A), lambda b, h, t: (b, h, 0, 0))
    return pl.pallas_call(
        functools.partial(_hgrn_kernel, L=L, n_chunks=tb // L, lb_index=lb_index),
        grid=(batch, H_A, nt),
        in_specs=[col(AB_Q), col(AB_F), col(AB_I), col(AB_G),
                  pl.BlockSpec((lower_bounds.shape[0], LANES), lambda b, h, t: (0, h)),
                  pl.BlockSpec((1, DV_A), lambda b, h, t: (0, 0)),
                  st_spec],
        out_specs=[pl.BlockSpec((tb, DV_A), lambda b, h, t: (b * nt + t, h)), st_spec],
        out_shape=[jax.ShapeDtypeStruct((batch * seq, H_A * DV_A), BF16),
                   jax.ShapeDtypeStruct((batch, H_A, DK_A, DV_A), F32)],
        scratch_shapes=[pltpu.VMEM((DK_A, DV_A), F32)],
        compiler_params=_params("arbitrary", "arbitrary", "arbitrary"),
        name="hgrn2",
    )(z, z, z, z, lower_bounds, norm_g.reshape(1, DV_A), s0)


def _mlstm_kernel(q_ref, k_ref, v_ref, og_ref, gt_ref, bias_ref, tri_ref, gn_ref, c0_ref, n0_ref,
                  m0_ref, o_ref, c_out_ref, n_out_ref, m_out_ref, c_ref, n_ref, m_ref, gtt_ref,
                  *, L, n_chunks):
    h = pl.program_id(1)
    tb = pl.program_id(2)

    @pl.when(tb == 0)
    def _():
        c_ref[...] = c0_ref[0, 0]
        n_ref[...] = n0_ref[0, 0]
        m_ref[...] = m0_ref[0, 0]

    n_rows = L * n_chunks
    gtt_ref[...] = _transpose_rows(gt_ref[...] + bias_ref[...])
    ig_rows = gtt_ref[pl.ds(h, 1), :][:, :n_rows]
    f_rows = gtt_ref[pl.ds(H_B + h, 1), :][:, :n_rows]
    logf = jnp.minimum(f_rows, 0.0) - jnp.log1p(jnp.exp(-jnp.abs(f_rows)))
    b_rows = _chunk_cumsum_rows(logf, tri_ref[...])

    causal = _iota2((L, L), 1) <= _iota2((L, L), 0)
    gn = gn_ref[...]
    scale = DK_B ** -0.5

    chunks = []
    for c in range(n_chunks):
        rows = pl.ds(c * L, L)
        q = q_ref[rows, :] * scale
        k = k_ref[rows, :]
        vb = _bf(v_ref[rows, :])
        ig_row = ig_rows[:, c * L:(c + 1) * L]
        b_row = b_rows[:, c * L:(c + 1) * L]
        b_col = _row_to_col(b_row)
        d = b_col - b_row + ig_row
        d_max = jnp.max(jnp.where(causal, d, -jnp.inf), axis=-1, keepdims=True)
        p = jnp.exp(jnp.where(causal, d - d_max, -jnp.inf)) * _mm_nt(_bf(q), _bf(k))
        b_last = b_row[:, L - 1:L]
        wk_max = jnp.max(b_last - b_row + ig_row, axis=-1, keepdims=True)
        ks = jnp.exp(b_last - b_col + _row_to_col(ig_row) - wk_max) * k
        chunks.append(dict(rows=rows, q=q, b_col=b_col, d_max=d_max, b_last=b_last, wk_max=wk_max,
                           p_sum=jnp.sum(p, axis=-1, keepdims=True), pv=_mm(_bf(p), vb),
                           k_sum=jnp.sum(ks, axis=0, keepdims=True), kv=_mm_tn(_bf(ks), vb)))

    for ch in chunks:
        q = ch['q']
        cs = c_ref[...]
        n = n_ref[...]
        m_prev = m_ref[...][:, 0:1]
        inter = ch['b_col'] + m_prev
        m_t = jnp.maximum(ch['d_max'], inter)
        w_intra = jnp.exp(ch['d_max'] - m_t)
        w_inter = jnp.exp(inter - m_t)
        num = w_intra * ch['pv'] + w_inter * _mm(_bf(q), _bf(cs))
        den = w_intra * ch['p_sum'] + w_inter * jnp.sum(q * n, axis=-1, keepdims=True)
        hid = num / jnp.maximum(jnp.abs(den), jnp.exp(-m_t))
        m_new = jnp.maximum(ch['b_last'] + m_prev, ch['wk_max'])
        w_new = jnp.exp(ch['wk_max'] - m_new)
        carry_w = jnp.exp(ch['b_last'] + m_prev - m_new)
        c_ref[...] = carry_w * cs + w_new * ch['kv']
        n_ref[...] = carry_w * n + w_new * ch['k_sum']
        m_ref[...] = jnp.broadcast_to(m_new, (1, LANES))
        o_ref[ch['rows'], :] = (_rms(hid, gn) * _sigmoid(og_ref[ch['rows'], :])).astype(BF16)

    @pl.when(tb == pl.num_programs(2) - 1)
    def _():
        c_out_ref[0, 0] = c_ref[...]
        n_out_ref[0, 0] = n_ref[...]
        m_out_ref[0, 0] = m_ref[...]


def mlstm(z, i_bias, f_bias, norm_g, c0, n0, m0, batch, seq, tb):
    L = min(MIX_CHUNK, seq)
    nt = seq // tb
    bias = jnp.zeros((1, LANES), F32).at[0, :H_B].set(i_bias).at[0, H_B:2 * H_B].set(f_bias)

    def col(c0_, width):
        return pl.BlockSpec((tb, width), lambda b, h, t: (b * nt + t, c0_ + h))

    def state(r, c):
        return pl.BlockSpec((1, 1, r, c), lambda b, h, t: (b, h, 0, 0))

    o, cs, n, m = pl.pallas_call(
        functools.partial(_mlstm_kernel, L=L, n_chunks=tb // L),
        grid=(batch, H_B, nt),
        in_specs=[col(AB_BQ, DK_B), col(AB_BK, DK_B), col(AB_BV, DV_B), col(AB_BO, DV_B),
                  pl.BlockSpec((tb, LANES), lambda b, h, t: (b * nt + t, AB_GATES)),
                  pl.BlockSpec((1, LANES), lambda b, h, t: (0, 0)),
                  pl.BlockSpec((tb, tb), lambda b, h, t: (0, 0)),
                  pl.BlockSpec((1, DV_B), lambda b, h, t: (0, 0)),
                  state(DK_B, DV_B), state(1, DK_B), state(1, LANES)],
        out_specs=[pl.BlockSpec((tb, DV_B), lambda b, h, t: (b * nt + t, h)),
                   state(DK_B, DV_B), state(1, DK_B), state(1, LANES)],
        out_shape=[jax.ShapeDtypeStruct((batch * seq, H_B * DV_B), BF16),
                   jax.ShapeDtypeStruct((batch, H_B, DK_B, DV_B), F32),
                   jax.ShapeDtypeStruct((batch, H_B, 1, DK_B), F32),
                   jax.ShapeDtypeStruct((batch, H_B, 1, LANES), F32)],
        scratch_shapes=[pltpu.VMEM((DK_B, DV_B), F32), pltpu.VMEM((1, DK_B), F32),
                        pltpu.VMEM((1, LANES), F32),
                        pltpu.VMEM((LANES, tb + (-tb % LANES)), F32)],
        compiler_params=_params("arbitrary", "arbitrary", "arbitrary"),
        name="mlstm",
    )(z, z, z, z, z, bias, _chunk_tri(tb, L), norm_g.reshape(1, DV_B), c0,
      n0.reshape(batch, H_B, 1, DK_B),
      jnp.broadcast_to(m0.reshape(batch, H_B, 1, 1), (batch, H_B, 1, LANES)))
    return o, cs, n.reshape(batch, H_B, DK_B), m[:, :, 0, 0]


def _gdn_kernel(q_ref, k_ref, v_ref, z_ref, gt_ref, cw_ref, cs_ref, gp_ref, tri_ref, gn_ref, s0_ref,
                o_ref, s_out_ref, st_ref, ext_ref, cv_ref, lgt_ref, *, L, n_chunks, tb):
    h = pl.program_id(1)
    t = pl.program_id(2)
    halo = CONV_C - 1

    @pl.when(t == 0)
    def _():
        st_ref[...] = s0_ref[0, 0]
        for p in range(3):
            ext_ref[p, SUBLANES - halo:SUBLANES, :] = cs_ref[p, 0]

    srcs = (q_ref, k_ref, v_ref)
    for p in range(3):
        ext_ref[p, SUBLANES:, :] = srcs[p][...]
        cw = cw_ref[p]
        acc = None
        for j in range(CONV_C):
            off = SUBLANES - halo + j
            term = ext_ref[p, off:off + tb, :] * cw[j:j + 1]
            acc = term if acc is None else acc + term
        cv_ref[p] = _silu(acc)
    ti = _iota2((L, L), 0)
    si = _iota2((L, L), 1)
    gp = gp_ref[...]
    gn = gn_ref[...]

    lgt_ref[...] = _transpose_rows(-jnp.exp(gp[0:1]) * _softplus(gt_ref[...] + gp[1:2]))
    b_rows = _chunk_cumsum_rows(lgt_ref[pl.ds(H_C + h, 1), :][:, :tb], tri_ref[...])

    def l2n(x):
        return x * lax.rsqrt(jnp.sum(x * x, axis=-1, keepdims=True) + EPS)

    chunks = []
    for c in range(n_chunks):
        rows = pl.ds(c * L, L)
        q = l2n(cv_ref[0, rows, :]) * (DK_C ** -0.5)
        k = l2n(cv_ref[1, rows, :])
        v = cv_ref[2, rows, :]
        beta = _lane_col(_sigmoid(gt_ref[rows, :]), h)
        b_row = b_rows[:, c * L:(c + 1) * L]
        b_col = _row_to_col(b_row)
        dec_incl = jnp.exp(jnp.where(si <= ti, b_col - b_row, -jnp.inf))
        eb = jnp.exp(b_col)
        kb = _bf(k)
        qb = _bf(q)
        kq = _mm_nt(jnp.concatenate([kb, qb], axis=0), kb)
        b_last = b_row[:, L - 1:L]
        chunks.append(dict(
            rows=rows, qb=qb, eb=eb, decay=jnp.exp(b_last),
            pw=-(beta * kq[:L] * jnp.where(si < ti, dec_incl, 0.0)),
            attn=_bf(kq[L:] * dec_incl),
            x=jnp.concatenate([beta * v, (beta * eb) * k], axis=1),
            k_dec=_bf(k * jnp.exp(b_last - b_col))))

    eye = jnp.where(ti == si, 1.0, 0.0)
    n_steps = int(math.log2(L))
    for ch in chunks:
        ch['inv'] = eye + ch['pw']
        pwb = _bf(ch['pw'])
        ch['pw'] = _mm(pwb, pwb)
    for step in range(1, n_steps):
        for ch in chunks:
            inv, pw = ch['inv'], ch['pw']
            if step == n_steps - 1:
                ch['inv'] = inv + _mm(_bf(pw), _bf(inv))
            else:
                y = _mm(_bf(pw), jnp.concatenate([_bf(inv), _bf(pw)], axis=1))
                ch['inv'] = inv + y[:, :L]
                ch['pw'] = y[:, L:]
    for ch in chunks:
        ch['x'] = _mm(_bf(ch['inv']), _bf(ch['x']))

    for ch in chunks:
        w = ch['x'][:, :DV_C]
        u = ch['x'][:, DV_C:]
        st = st_ref[...]
        uq = _mm_nt(jnp.concatenate([_bf(u), ch['qb']], axis=0), _bf(st))
        db = _bf(w - uq[:L])
        o = ch['eb'] * uq[L:] + _mm(ch['attn'], db)
        st_ref[...] = ch['decay'] * st + _mm_tn(db, ch['k_dec'])
        o_ref[ch['rows'], :] = (_rms(o, gn) * _silu(z_ref[ch['rows'], :])).astype(BF16)

    for p in range(3):
        ext_ref[p, 0:SUBLANES, :] = ext_ref[p, tb:tb + SUBLANES, :]

    @pl.when(t == pl.num_programs(2) - 1)
    def _():
        s_out_ref[0, 0] = st_ref[...]


def gdn(z, conv_w, conv_state, a_log, dt_bias, norm_g, s0, batch, seq, tb):
    L = min(CHUNK, seq)
    nt = seq // tb
    halo = CONV_C - 1
    gate_params = (jnp.zeros((2, LANES), F32).at[0, H_C:2 * H_C].set(a_log)
                   .at[1, H_C:2 * H_C].set(dt_bias))
    cw = conv_w.reshape(CONV_C, 3, H_C * DK_C).transpose(1, 0, 2)
    cs = conv_state.reshape(batch, halo, 3, H_C * DK_C).transpose(2, 0, 1, 3)

    def col(c0):
        return pl.BlockSpec((tb, LANES), lambda b, h, t: (b * nt + t, c0 + h))

    st_spec = pl.BlockSpec((1, 1, DV_C, DK_C), lambda b, h, t: (b, h, 0, 0))
    o, st = pl.pallas_call(
        functools.partial(_gdn_kernel, L=L, n_chunks=tb // L, tb=tb),
        grid=(batch, H_C, nt),
        in_specs=[col(CD_Q), col(CD_K), col(CD_V), col(CD_Z),
                  pl.BlockSpec((tb, LANES), lambda b, h, t: (b * nt + t, CD_GATES)),
                  pl.BlockSpec((3, CONV_C, LANES), lambda b, h, t: (0, 0, h)),
                  pl.BlockSpec((3, 1, halo, LANES), lambda b, h, t: (0, b, 0, h)),
                  pl.BlockSpec((2, LANES), lambda b, h, t: (0, 0)),
                  pl.BlockSpec((tb, tb), lambda b, h, t: (0, 0)),
                  pl.BlockSpec((1, DV_C), lambda b, h, t: (0, 0)),
                  st_spec],
        out_specs=[pl.BlockSpec((tb, DV_C), lambda b, h, t: (b * nt + t, h)), st_spec],
        out_shape=[jax.ShapeDtypeStruct((batch * seq, H_C * DV_C), BF16),
                   jax.ShapeDtypeStruct((batch, H_C, DV_C, DK_C), F32)],
        scratch_shapes=[pltpu.VMEM((DV_C, DK_C), F32),
                        pltpu.VMEM((3, SUBLANES + tb, LANES), F32),
                        pltpu.VMEM((3, tb, LANES), F32),
                        pltpu.VMEM((LANES, tb + (-tb % LANES)), F32)],
        compiler_params=_params("arbitrary", "arbitrary", "arbitrary"),
        name="gdn",
    )(z, z, z, z, z, cw, cs, gate_params, _chunk_tri(tb, L), norm_g.reshape(1, DV_C),
      jnp.swapaxes(s0, 2, 3))
    return o, jnp.swapaxes(st, 2, 3)


def _attn_kernel(q_ref, k_ref, v_ref, lp_ref, gn_ref, o_ref, ka_ref, vt_ref, m_ref, acc_ref,
                 *, seq, kv_len, tq, tk, lam_init):
    h = pl.program_id(1)
    qi = pl.program_id(2)
    past = kv_len - seq
    shift = int(math.log2(CHUNK))
    n_kb = kv_len // tk

    @pl.when(qi == 0)
    def _():
        k = k_ref[...]
        lane = _iota2((kv_len, 2 * D_HD), 1)
        k_idx = _iota2((kv_len, 1), 0)
        chunk_f = (k_idx >> shift).astype(F32)
        rem_f = (k_idx & (CHUNK - 1)).astype(F32)

        def pos_lanes(base):
            return jnp.where(lane == base, chunk_f, jnp.where(lane == base + 1, rem_f, 0.0))

        ka_ref[0] = _bf(jnp.where(lane < D_HD, k, pos_lanes(D_HD)))
        ka_ref[1] = _bf(jnp.where(lane >= D_HD, k, pos_lanes(0)))
        vt = _transpose_rows(v_ref[...])
        for jb in range(n_kb):
            vt_ref[jb, :DV_D, :] = _bf(vt[:, jb * tk:(jb + 1) * tk])
            vt_ref[jb, DV_D:, :] = jnp.ones((LANES, tk), BF16)

    lp = lp_ref[...]
    lam = (jnp.exp(jnp.sum(lp[0:1] * lp[1:2], axis=-1, keepdims=True))
           - jnp.exp(jnp.sum(lp[2:3] * lp[3:4], axis=-1, keepdims=True)) + lam_init)
    hf = jnp.full((1, 1), h + 1, jnp.int32).astype(F32)
    slope = jnp.exp(hf * (-8.0 / H_D * math.log(2.0)))

    q = q_ref[...] * (D_HD ** -0.5)
    lane_q = _iota2((tq, 2 * D_HD), 1)

    def slope_lanes(base):
        return jnp.where(lane_q == base, slope * CHUNK, jnp.where(lane_q == base + 1, slope, 0.0))

    qa = (_bf(jnp.where(lane_q < D_HD, q, slope_lanes(D_HD))),
          _bf(jnp.where(lane_q >= D_HD, q, slope_lanes(0))))
    q_start = past + qi * tq
    q_pos = q_start + _iota2((1, tq), 1)
    q_chunk = q_pos >> shift

    m_ref[...] = jnp.full(m_ref.shape, NEG_BIG, F32)
    acc_ref[...] = jnp.zeros(acc_ref.shape, F32)

    last_allowed = (((q_start + tq - 1) >> shift) << shift) + (CHUNK - 1)
    n_blocks = jnp.minimum(last_allowed, kv_len - 1) // tk + 1
    n_before = jnp.minimum((q_start + 1) // tk, n_blocks)

    def scores(j):
        rows = pl.ds(pl.multiple_of(j * tk, 2 * SUBLANES), tk)
        return [_mm_nt(ka_ref[c, rows, :], qa[c]) for c in range(2)]

    def accumulate(j, s_t, mask):
        vt = vt_ref[j]
        m_old = [m_ref[c] for c in range(2)]
        m_new = [jnp.maximum(m_old[c], jnp.max(s_t[c], axis=0, keepdims=True)) for c in range(2)]
        p_t = [jnp.exp(s_t[c] - m_new[c]) for c in range(2)]
        if mask is not None:
            p_t = [jnp.where(mask, p, 0.0) for p in p_t]
        for c in range(2):
            acc_ref[c] = jnp.exp(m_old[c] - m_new[c]) * acc_ref[c] + _mm(vt, _bf(p_t[c]))
            m_ref[c] = m_new[c]

    def block_before(j, carry):
        accumulate(j, scores(j), None)
        return carry

    def block_masked(j, carry):
        k_pos = j * tk + _iota2((tk, 1), 0)
        allowed = (k_pos >> shift) <= q_chunk
        late = (2.0 * slope) * jnp.maximum(k_pos - q_pos, 0).astype(F32)
        s_t = [jnp.where(allowed, s - late, NEG_BIG) for s in scores(j)]
        accumulate(j, s_t, allowed)
        return carry

    lax.fori_loop(0, n_before, block_before, 0)
    lax.fori_loop(n_before, n_blocks, block_masked, 0)
    a0 = acc_ref[0]
    a1 = acc_ref[1]
    o_t = (a0[:DV_D] / a0[DV_D:DV_D + 1] - lam * (a1[:DV_D] / a1[DV_D:DV_D + 1]))
    pad = -tq % LANES
    if pad:
        o_t = jnp.concatenate([o_t, jnp.zeros((DV_D, pad), F32)], axis=1)
    o = o_t.T[:tq]
    o_ref[...] = (_rms(o, gn_ref[...]) * (1.0 - lam_init)).astype(BF16)


def diff_attention(zq, q_col, kd, k_col, vd, v_col, lam_params, norm_g, batch, seq, kv_len,
                   tq, tk, lam_init):
    nq = seq // tq
    return pl.pallas_call(
        functools.partial(_attn_kernel, seq=seq, kv_len=kv_len, tq=tq, tk=tk, lam_init=lam_init),
        grid=(batch, H_D, nq),
        in_specs=[pl.BlockSpec((tq, DV_D), lambda b, h, i: (b * nq + i, q_col + h)),
                  pl.BlockSpec((kv_len, DV_D), lambda b, h, i: (b, k_col + h)),
                  pl.BlockSpec((kv_len, DV_D), lambda b, h, i: (b, v_col + h)),
                  pl.BlockSpec((4, D_HD), lambda b, h, i: (0, 0)),
                  pl.BlockSpec((1, DV_D), lambda b, h, i: (0, 0))],
        out_specs=pl.BlockSpec((tq, DV_D), lambda b, h, i: (b * nq + i, h)),
        out_shape=jax.ShapeDtypeStruct((batch * seq, H_D * DV_D), BF16),
        scratch_shapes=[pltpu.VMEM((2, kv_len, DV_D), BF16),
                        pltpu.VMEM((kv_len // tk, DV_D + LANES, tk), BF16),
                        pltpu.VMEM((2, 1, tq), F32), pltpu.VMEM((2, DV_D + LANES, tq), F32)],
        compiler_params=_params("arbitrary", "arbitrary", "arbitrary"),
        name="diff_attention",
    )(zq, kd, vd, lam_params, norm_g.reshape(1, DV_D))


def _pad_cols(w, total):
    return jnp.pad(w, ((0, 0), (0, total - w.shape[1])))


def _prep_w_in_ab(w):
    sizes = (H_A * DK_A, H_A * DK_A, H_A * DV_A, H_A * DV_A, H_B * DK_B, H_B * DK_B, H_B * DV_B,
             H_B, H_B, H_B * DV_B)
    aq, af, ai, ag, bq, bk, bv, big, bfg, bo = jnp.split(w, np_cumsum(sizes), axis=1)
    return _pad_cols(jnp.concatenate([aq, af, ai, ag, bq, bk, bv, bo, big, bfg], axis=1),
                     IN_AB_PAD).astype(BF16)


def _prep_w_in_cd(w):
    sizes = (H_C * (2 * DK_C + DV_C), H_C, H_C, H_C * DV_C, H_D * 2 * D_HD, H_D * 2 * D_HD,
             H_D * DV_D)
    cqkv, cb, ca, cz, dq, dk, dv = jnp.split(w, np_cumsum(sizes), axis=1)
    return _pad_cols(jnp.concatenate([cqkv, cz, dq, dk, dv, cb, ca], axis=1),
                     IN_CD_PAD).astype(BF16)


def np_cumsum(sizes):
    out, acc = [], 0
    for s in sizes[:-1]:
        acc += s
        out.append(acc)
    return out


def _tiles(n_rows, seq):
    tm = min(512, n_rows)
    tb = min(512, seq)
    return tm, tb


def _layer_ab(x, st, p, lower_bounds, batch, seq, layer):
    n = batch * seq
    tm, tb = _tiles(n, seq)
    z = norm_matmul(x, p['norm_mix'], p['w_in'], tm, IN_AB_PAD // 3)
    o_a, s_new = hgrn2(z, lower_bounds, p['hgrn_norm'], st['hgrn'], batch, seq, tb, layer // 2)
    o_b, c_new, n_new, m_new = mlstm(z, p['i_bias'], p['f_bias'], p['mlstm_norm'], st['mlstm_c'],
                                     st['mlstm_n'], st['mlstm_m'], batch, seq, tb)
    x = matmul_residual((o_a, o_b), p['w_out'], x, p['norm_ffn'], tm, 1024, False)
    return x, (s_new, c_new, n_new, m_new)


def _layer_cd(x, st, p, batch, seq, layer):
    n = batch * seq
    tm, tb = _tiles(n, seq)
    z = norm_matmul(x, p['norm_mix'], p['w_in'], tm, IN_CD_PAD // 3)
    o_c, s_new = gdn(z, p['conv_w'], st['gdn_conv'], p['a_log'], p['dt_bias'], p['gdn_norm'],
                     st['gdn'], batch, seq, tb)
    z3 = z.reshape(batch, seq, IN_CD_PAD)
    conv_new = z3[:, seq - (CONV_C - 1):, :H_C * (2 * DK_C + DV_C)]
    k_new = z3[:, :, CD_DK * LANES:CD_DV * LANES]
    v_new = z3[:, :, CD_DV * LANES:CD_GATES * LANES]
    past = st['k_cache'].shape[1]
    lam_init = 0.8 - 0.6 * math.exp(-0.3 * layer)
    if past == 0:
        o_d = diff_attention(z, CD_DQ, z, CD_DK, z, CD_DV, p['lam'], p['diff_norm'], batch, seq,
                             seq, min(512, seq), min(512, seq), lam_init)
    else:
        kv_len = past + seq
        kd = jnp.concatenate([st['k_cache'].reshape(batch, past, -1), k_new], axis=1)
        vd = jnp.concatenate([st['v_cache'].reshape(batch, past, -1), v_new], axis=1)
        o_d = diff_attention(z, CD_DQ, kd.reshape(batch * kv_len, -1), 0,
                             vd.reshape(batch * kv_len, -1), 0, p['lam'], p['diff_norm'], batch,
                             seq, kv_len, seq, kv_len, lam_init)
    x = matmul_residual((o_c, o_d), p['w_out'], x, p['norm_ffn'], tm, 1024, False)
    return x, (s_new, conv_new, k_new.reshape(batch, seq, H_D, 2 * D_HD),
               v_new.reshape(batch, seq, H_D, DV_D))


def _ffn(x, state, p, batch, seq, final_g):
    n = batch * seq
    tm, _ = _tiles(n, seq)
    act, conv_new = ffn_up(x, p['norm_ffn'], p['ffn_up'], p['ffn_conv_w'], p['ffn_conv_b'], state,
                           batch, seq, tm, 512)
    g = p['norm_ffn'] if final_g is None else final_g
    x = matmul_residual((act,), p['ffn_down'], x, g, tm, D_FF // 4, final_g is not None)
    return x, conv_new


def _run_group(x, st, params, lower_bounds, norm_final):
    batch, seq, _ = x.shape
    x = x.reshape(batch * seq, D_MODEL)
    new_states = []
    for layer, (p, s) in enumerate(zip(params, st)):
        if layer % 2 == 0:
            x, new_mix = _layer_ab(x, s, p, lower_bounds, batch, seq, layer)
        else:
            x, new_mix = _layer_cd(x, s, p, batch, seq, layer)
        last = layer == len(params) - 1
        x, new_ffn = _ffn(x, s['ffn_conv'], p, batch, seq, norm_final if last else None)
        new_states.append(new_mix + (new_ffn,))
    return x.reshape(batch, seq, D_MODEL), new_states


def kernel(x_prompt, x_sample, state_hgrn_0, state_mlstm_c_0, state_mlstm_n_0, state_mlstm_m_0,
           state_ffn_conv_0, state_gdn_1, state_gdn_conv_1, cache_k_1, cache_v_1, state_ffn_conv_1,
           hgrn_lower_bounds, norm_mix_0, w_in_0, mlstm_i_bias_0, mlstm_f_bias_0, hgrn_norm_0,
           mlstm_norm_0, w_out_0, norm_ffn_0, ffn_up_0, ffn_conv_w_0, ffn_conv_b_0, ffn_down_0,
           norm_mix_1, w_in_1, gdn_conv_w_1, gdn_a_log_1, gdn_dt_bias_1, gdn_norm_1, diff_lambda_1,
           diff_norm_1, w_out_1, norm_ffn_1, ffn_up_1, ffn_conv_w_1, ffn_conv_b_1, ffn_down_1,
           norm_final):
    params = [
        dict(norm_mix=norm_mix_0, w_in=_prep_w_in_ab(w_in_0), i_bias=mlstm_i_bias_0,
             f_bias=mlstm_f_bias_0, hgrn_norm=hgrn_norm_0, mlstm_norm=mlstm_norm_0,
             w_out=w_out_0.astype(BF16), norm_ffn=norm_ffn_0, ffn_up=ffn_up_0.astype(BF16),
             ffn_conv_w=ffn_conv_w_0, ffn_conv_b=ffn_conv_b_0, ffn_down=ffn_down_0.astype(BF16)),
        dict(norm_mix=norm_mix_1, w_in=_prep_w_in_cd(w_in_1), conv_w=gdn_conv_w_1,
             a_log=gdn_a_log_1, dt_bias=gdn_dt_bias_1, gdn_norm=gdn_norm_1, lam=diff_lambda_1,
             diff_norm=diff_norm_1, w_out=w_out_1.astype(BF16), norm_ffn=norm_ffn_1,
             ffn_up=ffn_up_1.astype(BF16), ffn_conv_w=ffn_conv_w_1, ffn_conv_b=ffn_conv_b_1,
             ffn_down=ffn_down_1.astype(BF16)),
    ]
    bp = x_prompt.shape[0]
    st_prompt = [
        dict(hgrn=jnp.zeros((bp, H_A, DK_A, DV_A), F32), mlstm_c=jnp.zeros((bp, H_B, DK_B, DV_B), F32),
             mlstm_n=jnp.zeros((bp, H_B, DK_B), F32), mlstm_m=jnp.full((bp, H_B), NEG_BIG, F32),
             ffn_conv=jnp.zeros((bp, CONV_FFN - 1, D_FF), F32)),
        dict(gdn=jnp.zeros((bp, H_C, DK_C, DV_C), F32),
             gdn_conv=jnp.zeros((bp, CONV_C - 1, H_C * (2 * DK_C + DV_C)), F32),
             k_cache=jnp.zeros((bp, 0, H_D, 2 * D_HD), F32), v_cache=jnp.zeros((bp, 0, H_D, DV_D), F32),
             ffn_conv=jnp.zeros((bp, CONV_FFN - 1, D_FF), F32)),
    ]
    st_sample = [
        dict(hgrn=state_hgrn_0, mlstm_c=state_mlstm_c_0, mlstm_n=state_mlstm_n_0,
             mlstm_m=state_mlstm_m_0, ffn_conv=state_ffn_conv_0),
        dict(gdn=state_gdn_1, gdn_conv=state_gdn_conv_1, k_cache=cache_k_1, v_cache=cache_v_1,
             ffn_conv=state_ffn_conv_1),
    ]
    y_p, new_p = _run_group(x_prompt, st_prompt, params, hgrn_lower_bounds, norm_final)
    y_s, new_s = _run_group(x_sample, st_sample, params, hgrn_lower_bounds, norm_final)
    (hgrn_p, c_p, n_p, m_p, f0_p), (gdn_p, gc_p, k_p, v_p, f1_p) = new_p
    (hgrn_s, c_s, n_s, m_s, f0_s), (gdn_s, gc_s, k_s, v_s, f1_s) = new_s
    return (y_p, y_s, hgrn_p, hgrn_s, c_p, c_s, n_p, n_s, m_p, m_s, f0_p, f0_s, gdn_p, gdn_s,
            gc_p, gc_s, k_p, k_s, v_p, v_s, f1_p, f1_s)
```

```python
import functools
import math

import jax
import jax.numpy as jnp
from jax import lax
from jax.experimental import pallas as pl
from jax.experimental.pallas import tpu as pltpu

F32 = jnp.float32
BF16 = jnp.bfloat16
HI = lax.Precision.HIGHEST

D_MODEL = 2048
CHUNK = 64
H_A, DK_A, DV_A = 8, 128, 128
H_B, DK_B, DV_B = 4, 128, 256
H_C, DK_C, DV_C = 8, 128, 128
CONV_C = 4
H_D, D_HD = 8, 64
DV_D = 2 * D_HD
D_FF = 5632
CONV_FFN = 3
EPS = 1e-6
NEG_BIG = -1e30

LANES = 128
SUBLANES = 8
VMEM_LIMIT = 56 * 1024 * 1024
MIX_CHUNK = 128

AB_Q, AB_F, AB_I, AB_G = 0, 8, 16, 24
AB_BQ, AB_BK = 32, 36
AB_BV, AB_BO = 20, 24
AB_GATES = 56
IN_AB_PAD = 57 * LANES
CD_Q, CD_K, CD_V, CD_Z = 0, 8, 16, 24
CD_DQ, CD_DK, CD_DV = 32, 40, 48
CD_GATES = 56
IN_CD_PAD = 57 * LANES


def _mm(a, b, prec=None):
    return lax.dot_general(a, b, (((1,), (0,)), ((), ())), precision=prec,
                           preferred_element_type=F32)


def _mm_nt(a, b, prec=None):
    return lax.dot_general(a, b, (((1,), (1,)), ((), ())), precision=prec,
                           preferred_element_type=F32)


def _mm_tn(a, b, prec=None):
    return lax.dot_general(a, b, (((0,), (0,)), ((), ())), precision=prec,
                           preferred_element_type=F32)


def _sigmoid(x):
    return 1.0 / (1.0 + jnp.exp(-x))


def _silu(x):
    return x * _sigmoid(x)


def _softplus(x):
    return jnp.maximum(x, 0.0) + jnp.log1p(jnp.exp(-jnp.abs(x)))


def _rms(x, g):
    return x * lax.rsqrt(jnp.mean(x * x, axis=-1, keepdims=True) + EPS) * g


def _params(*sem):
    return pltpu.CompilerParams(dimension_semantics=sem, vmem_limit_bytes=VMEM_LIMIT)


def _norm_matmul_kernel(x_ref, g_ref, w_ref, o_ref, h_ref):
    @pl.when(pl.program_id(1) == 0)
    def _():
        h_ref[...] = _rms(x_ref[...], g_ref[...]).astype(BF16)

    o_ref[...] = _mm(h_ref[...], w_ref[...])


def norm_matmul(x, g, w, tm, tn):
    n, k = x.shape
    m = w.shape[1]
    return pl.pallas_call(
        _norm_matmul_kernel,
        grid=(n // tm, m // tn),
        in_specs=[pl.BlockSpec((tm, k), lambda i, j: (i, 0)),
                  pl.BlockSpec((1, k), lambda i, j: (0, 0)),
                  pl.BlockSpec((k, tn), lambda i, j: (0, j))],
        out_specs=pl.BlockSpec((tm, tn), lambda i, j: (i, j)),
        out_shape=jax.ShapeDtypeStruct((n, m), F32),
        scratch_shapes=[pltpu.VMEM((tm, k), BF16)],
        compiler_params=_params("arbitrary", "arbitrary"),
        name="norm_matmul",
    )(x, g.reshape(1, k), w)


def _matmul_res_kernel(*refs, n_parts, nk_part, final_norm):
    y_refs = refs[:n_parts]
    w_ref, x_ref, g_ref, o_ref = refs[n_parts:]
    kk = pl.program_id(1)

    for part, y_ref in enumerate(y_refs):
        @pl.when((kk >= part * nk_part) & (kk < (part + 1) * nk_part))
        def _(y_ref=y_ref):
            d = _mm(y_ref[...], w_ref[...])

            @pl.when(kk == 0)
            def _():
                o_ref[...] = x_ref[...] + d

            @pl.when(kk != 0)
            def _():
                o_ref[...] += d

    if final_norm:
        @pl.when(kk == n_parts * nk_part - 1)
        def _():
            o_ref[...] = _rms(o_ref[...], g_ref[...])


def matmul_residual(ys, w, x, g, tm, tk, final_norm):
    n, k_part = ys[0].shape
    d = w.shape[1]
    nk_part = k_part // tk
    n_parts = len(ys)

    def y_spec(part):
        return pl.BlockSpec(
            (tm, tk), lambda i, kk: (i, jnp.clip(kk - part * nk_part, 0, nk_part - 1)))

    return pl.pallas_call(
        functools.partial(_matmul_res_kernel, n_parts=n_parts, nk_part=nk_part,
                          final_norm=final_norm),
        grid=(n // tm, n_parts * nk_part),
        in_specs=[y_spec(part) for part in range(n_parts)]
        + [pl.BlockSpec((tk, d), lambda i, kk: (kk, 0)),
           pl.BlockSpec((tm, d), lambda i, kk: (i, 0)),
           pl.BlockSpec((1, d), lambda i, kk: (0, 0))],
        out_specs=pl.BlockSpec((tm, d), lambda i, kk: (i, 0)),
        out_shape=jax.ShapeDtypeStruct((n, d), F32),
        compiler_params=_params("arbitrary", "arbitrary"),
        name="matmul_residual",
    )(*ys, w, x, g.reshape(1, d))


def _ffn_up_kernel(x_ref, g_ref, wa_ref, wu_ref, cw_ref, cb_ref, st_ref, act_ref, tail_ref,
                   h_ref, ext_ref, carry_ref, *, nseq, rows, tiles_per_seq):
    i = pl.program_id(0)
    j = pl.program_id(1)
    halo = CONV_FFN - 1

    @pl.when(j == 0)
    def _():
        h_ref[...] = _rms(x_ref[...], g_ref[...]).astype(BF16)

    if tiles_per_seq > 1:
        @pl.when(i == 0)
        def _():
            carry_ref[j] = jnp.zeros(carry_ref.shape[1:], F32)

    h = h_ref[...]
    a = _mm(h, wa_ref[...])
    u = _mm(h, wu_ref[...])
    cw = cw_ref[...]
    cb = cb_ref[...]
    for s in range(nseq):
        lo = s * rows
        ext_ref[s, SUBLANES:, :] = a[lo:lo + rows]
        prev = st_ref[s]
        if tiles_per_seq > 1:
            prev = jnp.where((i % tiles_per_seq) == 0, prev, carry_ref[j, SUBLANES - halo:, :])
        ext_ref[s, SUBLANES - halo:SUBLANES, :] = prev
        conv = cb
        for t in range(CONV_FFN):
            off = SUBLANES - halo + t
            conv = conv + ext_ref[s, off:off + rows, :] * cw[t:t + 1]
        act_ref[lo:lo + rows, :] = (_silu(conv) * u[lo:lo + rows]).astype(BF16)
        tail_ref[s] = ext_ref[s, SUBLANES + rows - halo:SUBLANES + rows, :]
        carry_ref[j] = ext_ref[s, rows:rows + SUBLANES, :]


def ffn_up(x, g, w_up, conv_w, conv_b, state, batch, seq, tm, tf):
    n, k = x.shape
    nff = D_FF // tf
    if tm >= seq:
        nseq, rows, tiles_per_seq = tm // seq, seq, 1
    else:
        nseq, rows, tiles_per_seq = 1, tm, seq // tm
    ntiles = n // tm
    halo = CONV_FFN - 1
    act, tail = pl.pallas_call(
        functools.partial(_ffn_up_kernel, nseq=nseq, rows=rows, tiles_per_seq=tiles_per_seq),
        grid=(ntiles, nff),
        in_specs=[pl.BlockSpec((tm, k), lambda i, j: (i, 0)),
                  pl.BlockSpec((1, k), lambda i, j: (0, 0)),
                  pl.BlockSpec((k, tf), lambda i, j: (0, j)),
                  pl.BlockSpec((k, tf), lambda i, j: (0, j + nff)),
                  pl.BlockSpec((CONV_FFN, tf), lambda i, j: (0, j)),
                  pl.BlockSpec((1, tf), lambda i, j: (0, j)),
                  pl.BlockSpec((nseq, halo, tf), lambda i, j: (i // tiles_per_seq, 0, j))],
        out_specs=[pl.BlockSpec((tm, tf), lambda i, j: (i, j)),
                   pl.BlockSpec((nseq, halo, tf), lambda i, j: (i, 0, j))],
        out_shape=[jax.ShapeDtypeStruct((n, D_FF), BF16),
                   jax.ShapeDtypeStruct((ntiles * nseq, halo, D_FF), F32)],
        scratch_shapes=[pltpu.VMEM((tm, k), BF16),
                        pltpu.VMEM((nseq, SUBLANES + rows, tf), F32),
                        pltpu.VMEM((nff, SUBLANES, tf), F32)],
        compiler_params=_params("arbitrary", "arbitrary"),
        name="ffn_up",
    )(x, g.reshape(1, k), w_up, w_up, conv_w, conv_b.reshape(1, D_FF), state)
    new_state = tail.reshape(batch, -1, halo, D_FF)[:, -1]
    return act, new_state


def _iota2(shape, dim):
    return lax.broadcasted_iota(jnp.int32, shape, dim)


def _tril(n):
    return (_iota2((n, n), 1) <= _iota2((n, n), 0)).astype(F32)


def _lane_col(x, lane):
    return jnp.sum(jnp.where(_iota2(x.shape, 1) == lane, x, 0.0), axis=-1, keepdims=True)


def _bf(x):
    return x.astype(BF16)


def _split3(x):
    h1 = _bf(x)
    r1 = x - h1.astype(F32)
    h2 = _bf(r1)
    h3 = _bf(r1 - h2.astype(F32))
    return h1, h2, h3


def _mm_sel(sel, x):
    n = x.shape[1]
    y = _mm(_bf(sel), jnp.concatenate(_split3(x), axis=1))
    return y[:, :n] + y[:, n:2 * n] + y[:, 2 * n:]


def _transpose_rows(x):
    r = x.shape[0]
    pad = -r % LANES
    if pad:
        x = jnp.concatenate([x, jnp.zeros((pad, x.shape[1]), x.dtype)], axis=0)
    return x.T


def _chunk_cumsum_rows(row, tri):
    r = row.shape[1]
    h1, h2, h3 = (p.astype(F32) for p in _split3(row))
    sub = _iota2((2 * SUBLANES, r), 0)
    parts = jnp.where(sub == 0, h1, jnp.where(sub == 1, h2, jnp.where(sub == 2, h3, 0.0)))
    y = _mm(_bf(parts), tri)
    return y[0:1] + y[1:2] + y[2:3]


def _row_to_col(row):
    n = row.shape[1]
    eye = _iota2((n, n), 0) == _iota2((n, n), 1)
    return jnp.sum(jnp.where(eye, row, 0.0), axis=-1, keepdims=True)


def _chunk_tri(tb, chunk):
    s = lax.broadcasted_iota(jnp.int32, (tb, tb), 0)
    t = lax.broadcasted_iota(jnp.int32, (tb, tb), 1)
    return ((s <= t) & (s // chunk == t // chunk)).astype(BF16)


def _hgrn_kernel(q_ref, f_ref, i_ref, g_ref, lb_ref, gn_ref, s0_ref, o_ref, s_out_ref, s_ref,
                 *, L, n_chunks, lb_index):
    tb = pl.program_id(2)

    @pl.when(tb == 0)
    def _():
        s_ref[...] = s0_ref[0, 0]

    lbp = lb_ref[...]
    e = jnp.exp(lbp - jnp.max(lbp, axis=0, keepdims=True))
    lb = jnp.sum(e[:lb_index + 1], axis=0, keepdims=True) / jnp.sum(e, axis=0, keepdims=True)

    sizes = [L >> l for l in range(int(math.log2(L)))]
    ti = _iota2((L, L), 0)
    si = _iota2((L, L), 1)
    tcol = _iota2((L, 1), 0)
    mats = [si <= ti]
    for sz in sizes:
        mats.append(si <= ((ti & ~(sz - 1)) + (sz // 2 - 1)))
    cum_mat = jnp.concatenate([m.astype(F32) for m in mats], axis=0)
    eye = ti == si
    gn = gn_ref[...]

    chunks = []
    for c in range(n_chunks):
        rows = pl.ds(c * L, L)
        q = q_ref[rows, :]
        f = lb + (1.0 - lb) * _sigmoid(f_ref[rows, :])
        k = 1.0 - f
        cums = _mm_sel(cum_mat, jnp.log(f))
        chunks.append(dict(rows=rows, q=q, k=k, cums=cums, b=cums[:L], vb=_bf(i_ref[rows, :]),
                           scores=jnp.where(eye, jnp.sum(q * k, axis=-1, keepdims=True), 0.0)))
    for l, sz in enumerate(sizes):
        right = (tcol & (sz - 1)) >= sz // 2
        same = (ti & ~(sz - 1)) == (si & ~(sz - 1))
        for ch in chunks:
            w = jnp.exp(-jnp.abs(ch['b'] - ch['cums'][(l + 1) * L:(l + 2) * L]))
            qm = jnp.where(right, ch['q'] * w, 0.0)
            km = jnp.where(right, 0.0, ch['k'] * w)
            ch['scores'] = ch['scores'] + jnp.where(same, _mm_nt(_bf(qm), _bf(km)), 0.0)
    for ch in chunks:
        bl = ch['b'][L - 1:L]
        ch['update'] = _mm_tn(_bf(ch['k'] * jnp.exp(bl - ch['b'])), ch['vb'])
        ch['decay'] = _row_to_col(jnp.exp(bl))
    for ch in chunks:
        s = s_ref[...]
        o = _mm(jnp.concatenate([_bf(ch['q'] * jnp.exp(ch['b'])), _bf(ch['scores'])], axis=1),
                jnp.concatenate([_bf(s), ch['vb']], axis=0))
        s_ref[...] = ch['decay'] * s + ch['update']
        o_ref[ch['rows'], :] = (_rms(o, gn) * _silu(g_ref[ch['rows'], :])).astype(BF16)

    @pl.when(tb == pl.num_programs(2) - 1)
    def _():
        s_out_ref[0, 0] = s_ref[...]


def hgrn2(z, lower_bounds, norm_g, s0, batch, seq, tb, lb_index):
    L = min(MIX_CHUNK, seq)
    nt = seq // tb

    def col(c0):
        return pl.BlockSpec((tb, LANES), lambda b, h, t: (b * nt + t, c0 + h))

    st_spec = pl.BlockSpec((1, 1, DK_A, DV_A), lambda b, h, t: (b, h, 0, 0))
    return pl.pallas_call(
        functools.partial(_hgrn_kernel, L=L, n_chunks=tb // L, lb_index=lb_index),
        grid=(batch, H_A, nt),
        in_specs=[col(AB_Q), col(AB_F), col(AB_I), col(AB_G),
                  pl.BlockSpec((lower_bounds.shape[0], LANES), lambda b, h, t: (0, h)),
                  pl.BlockSpec((1, DV_A), lambda b, h, t: (0, 0)),
                  st_spec],
        out_specs=[pl.BlockSpec((tb, DV_A), lambda b, h, t: (b * nt + t, h)), st_spec],
        out_shape=[jax.ShapeDtypeStruct((batch * seq, H_A * DV_A), BF16),
                   jax.ShapeDtypeStruct((batch, H_A, DK_A, DV_A), F32)],
        scratch_shapes=[pltpu.VMEM((DK_A, DV_A), F32)],
        compiler_params=_params("arbitrary", "arbitrary", "arbitrary"),
        name="hgrn2",
    )(z, z, z, z, lower_bounds, norm_g.reshape(1, DV_A), s0)


def _mlstm_kernel(q_ref, k_ref, v_ref, og_ref, gt_ref, bias_ref, tri_ref, gn_ref, c0_ref, n0_ref,
                  m0_ref, o_ref, c_out_ref, n_out_ref, m_out_ref, c_ref, n_ref, m_ref, gtt_ref,
                  *, L, n_chunks):
    h = pl.program_id(1)
    tb = pl.program_id(2)

    @pl.when(tb == 0)
    def _():
        c_ref[...] = c0_ref[0, 0]
        n_ref[...] = n0_ref[0, 0]
        m_ref[...] = m0_ref[0, 0]

    n_rows = L * n_chunks
    gtt_ref[...] = _transpose_rows(gt_ref[...] + bias_ref[...])
    ig_rows = gtt_ref[pl.ds(h, 1), :][:, :n_rows]
    f_rows = gtt_ref[pl.ds(H_B + h, 1), :][:, :n_rows]
    logf = jnp.minimum(f_rows, 0.0) - jnp.log1p(jnp.exp(-jnp.abs(f_rows)))
    b_rows = _chunk_cumsum_rows(logf, tri_ref[...])

    causal = _iota2((L, L), 1) <= _iota2((L, L), 0)
    gn = gn_ref[...]
    scale = DK_B ** -0.5

    chunks = []
    for c in range(n_chunks):
        rows = pl.ds(c * L, L)
        q = q_ref[rows, :] * scale
        k = k_ref[rows, :]
        vb = _bf(v_ref[rows, :])
        ig_row = ig_rows[:, c * L:(c + 1) * L]
        b_row = b_rows[:, c * L:(c + 1) * L]
        b_col = _row_to_col(b_row)
        d = b_col - b_row + ig_row
        d_max = jnp.max(jnp.where(causal, d, -jnp.inf), axis=-1, keepdims=True)
        p = jnp.exp(jnp.where(causal, d - d_max, -jnp.inf)) * _mm_nt(_bf(q), _bf(k))
        b_last = b_row[:, L - 1:L]
        wk_max = jnp.max(b_last - b_row + ig_row, axis=-1, keepdims=True)
        ks = jnp.exp(b_last - b_col + _row_to_col(ig_row) - wk_max) * k
        chunks.append(dict(rows=rows, q=q, b_col=b_col, d_max=d_max, b_last=b_last, wk_max=wk_max,
                           p_sum=jnp.sum(p, axis=-1, keepdims=True), pv=_mm(_bf(p), vb),
                           k_sum=jnp.sum(ks, axis=0, keepdims=True), kv=_mm_tn(_bf(ks), vb)))

    for ch in chunks:
        q = ch['q']
        cs = c_ref[...]
        n = n_ref[...]
        m_prev = m_ref[...][:, 0:1]
        inter = ch['b_col'] + m_prev
        m_t = jnp.maximum(ch['d_max'], inter)
        w_intra = jnp.exp(ch['d_max'] - m_t)
        w_inter = jnp.exp(inter - m_t)
        num = w_intra * ch['pv'] + w_inter * _mm(_bf(q), _bf(cs))
        den = w_intra * ch['p_sum'] + w_inter * jnp.sum(q * n, axis=-1, keepdims=True)
        hid = num / jnp.maximum(jnp.abs(den), jnp.exp(-m_t))
        m_new = jnp.maximum(ch['b_last'] + m_prev, ch['wk_max'])
        w_new = jnp.exp(ch['wk_max'] - m_new)
        carry_w = jnp.exp(ch['b_last'] + m_prev - m_new)
        c_ref[...] = carry_w * cs + w_new * ch['kv']
        n_ref[...] = carry_w * n + w_new * ch['k_sum']
        m_ref[...] = jnp.broadcast_to(m_new, (1, LANES))
        o_ref[ch['rows'], :] = (_rms(hid, gn) * _sigmoid(og_ref[ch['rows'], :])).astype(BF16)

    @pl.when(tb == pl.num_programs(2) - 1)
    def _():
        c_out_ref[0, 0] = c_ref[...]
        n_out_ref[0, 0] = n_ref[...]
        m_out_ref[0, 0] = m_ref[...]


def mlstm(z, i_bias, f_bias, norm_g, c0, n0, m0, batch, seq, tb):
    L = min(MIX_CHUNK, seq)
    nt = seq // tb
    bias = jnp.zeros((1, LANES), F32).at[0, :H_B].set(i_bias).at[0, H_B:2 * H_B].set(f_bias)

    def col(c0_, width):
        return pl.BlockSpec((tb, width), lambda b, h, t: (b * nt + t, c0_ + h))

    def state(r, c):
        return pl.BlockSpec((1, 1, r, c), lambda b, h, t: (b, h, 0, 0))

    o, cs, n, m = pl.pallas_call(
        functools.partial(_mlstm_kernel, L=L, n_chunks=tb // L),
        grid=(batch, H_B, nt),
        in_specs=[col(AB_BQ, DK_B), col(AB_BK, DK_B), col(AB_BV, DV_B), col(AB_BO, DV_B),
                  pl.BlockSpec((tb, LANES), lambda b, h, t: (b * nt + t, AB_GATES)),
                  pl.BlockSpec((1, LANES), lambda b, h, t: (0, 0)),
                  pl.BlockSpec((tb, tb), lambda b, h, t: (0, 0)),
                  pl.BlockSpec((1, DV_B), lambda b, h, t: (0, 0)),
                  state(DK_B, DV_B), state(1, DK_B), state(1, LANES)],
        out_specs=[pl.BlockSpec((tb, DV_B), lambda b, h, t: (b * nt + t, h)),
                   state(DK_B, DV_B), state(1, DK_B), state(1, LANES)],
        out_shape=[jax.ShapeDtypeStruct((batch * seq, H_B * DV_B), BF16),
                   jax.ShapeDtypeStruct((batch, H_B, DK_B, DV_B), F32),
                   jax.ShapeDtypeStruct((batch, H_B, 1, DK_B), F32),
                   jax.ShapeDtypeStruct((batch, H_B, 1, LANES), F32)],
        scratch_shapes=[pltpu.VMEM((DK_B, DV_B), F32), pltpu.VMEM((1, DK_B), F32),
                        pltpu.VMEM((1, LANES), F32),
                        pltpu.VMEM((LANES, tb + (-tb % LANES)), F32)],
        compiler_params=_params("arbitrary", "arbitrary", "arbitrary"),
        name="mlstm",
    )(z, z, z, z, z, bias, _chunk_tri(tb, L), norm_g.reshape(1, DV_B), c0,
      n0.reshape(batch, H_B, 1, DK_B),
      jnp.broadcast_to(m0.reshape(batch, H_B, 1, 1), (batch, H_B, 1, LANES)))
    return o, cs, n.reshape(batch, H_B, DK_B), m[:, :, 0, 0]


def _gdn_kernel(q_ref, k_ref, v_ref, z_ref, gt_ref, cw_ref, cs_ref, gp_ref, tri_ref, gn_ref, s0_ref,
                o_ref, s_out_ref, st_ref, ext_ref, cv_ref, lgt_ref, *, L, n_chunks, tb, heads):
    h0 = pl.program_id(1) * heads
    t = pl.program_id(2)
    halo = CONV_C - 1

    @pl.when(t == 0)
    def _():
        st_ref[...] = s0_ref[0]
        for p in range(3):
            ext_ref[p, SUBLANES - halo:SUBLANES, :] = cs_ref[p, 0]

    srcs = (q_ref, k_ref, v_ref)
    for p in range(3):
        ext_ref[p, SUBLANES:, :] = srcs[p][...]
        cw = cw_ref[p]
        acc = None
        for j in range(CONV_C):
            off = SUBLANES - halo + j
            term = ext_ref[p, off:off + tb, :] * cw[j:j + 1]
            acc = term if acc is None else acc + term
        cv_ref[p] = _silu(acc)
    ti = _iota2((L, L), 0)
    si = _iota2((L, L), 1)
    gp = gp_ref[...]
    gn = gn_ref[...]

    lgt_ref[...] = _transpose_rows(-jnp.exp(gp[0:1]) * _softplus(gt_ref[...] + gp[1:2]))
    b_rows = [_chunk_cumsum_rows(lgt_ref[pl.ds(H_C + h0 + g, 1), :][:, :tb], tri_ref[...])
              for g in range(heads)]

    def l2n(x):
        return x * lax.rsqrt(jnp.sum(x * x, axis=-1, keepdims=True) + EPS)

    chunks = []
    for c in range(n_chunks):
        rows = pl.ds(c * L, L)
        sig = _sigmoid(gt_ref[rows, :])
        for g in range(heads):
            lanes = slice(g * LANES, (g + 1) * LANES)
            q = l2n(cv_ref[0, rows, lanes]) * (DK_C ** -0.5)
            k = l2n(cv_ref[1, rows, lanes])
            v = cv_ref[2, rows, lanes]
            beta = _lane_col(sig, h0 + g)
            b_row = b_rows[g][:, c * L:(c + 1) * L]
            b_col = _row_to_col(b_row)
            dec_incl = jnp.exp(jnp.where(si <= ti, b_col - b_row, -jnp.inf))
            eb = jnp.exp(b_col)
            kb = _bf(k)
            qb = _bf(q)
            kq = _mm_nt(jnp.concatenate([kb, qb], axis=0), kb)
            b_last = b_row[:, L - 1:L]
            chunks.append(dict(
                rows=rows, lanes=lanes, g=g, qb=qb, eb=eb, decay=jnp.exp(b_last),
                pw=-(beta * kq[:L] * jnp.where(si < ti, dec_incl, 0.0)),
                attn=_bf(kq[L:] * dec_incl),
                x=jnp.concatenate([beta * v, (beta * eb) * k], axis=1),
                k_dec=_bf(k * jnp.exp(b_last - b_col))))

    eye = jnp.where(ti == si, 1.0, 0.0)
    n_steps = int(math.log2(L))
    for ch in chunks:
        ch['inv'] = eye + ch['pw']
        pwb = _bf(ch['pw'])
        ch['pw'] = _mm(pwb, pwb)
    for step in range(1, n_steps):
        for ch in chunks:
            inv, pw = ch['inv'], ch['pw']
            if step == n_steps - 1:
                ch['inv'] = inv + _mm(_bf(pw), _bf(inv))
            else:
                y = _mm(_bf(pw), jnp.concatenate([_bf(inv), _bf(pw)], axis=1))
                ch['inv'] = inv + y[:, :L]
                ch['pw'] = y[:, L:]
    for ch in chunks:
        ch['x'] = _mm(_bf(ch['inv']), _bf(ch['x']))

    for ch in chunks:
        w = ch['x'][:, :DV_C]
        u = ch['x'][:, DV_C:]
        st = st_ref[ch['g']]
        uq = _mm_nt(jnp.concatenate([_bf(u), ch['qb']], axis=0), _bf(st))
        db = _bf(w - uq[:L])
        o = ch['eb'] * uq[L:] + _mm(ch['attn'], db)
        st_ref[ch['g']] = ch['decay'] * st + _mm_tn(db, ch['k_dec'])
        o_ref[ch['rows'], ch['lanes']] = (
            _rms(o, gn) * _silu(z_ref[ch['rows'], ch['lanes']])).astype(BF16)

    for p in range(3):
        ext_ref[p, 0:SUBLANES, :] = ext_ref[p, tb:tb + SUBLANES, :]

    @pl.when(t == pl.num_programs(2) - 1)
    def _():
        s_out_ref[0] = st_ref[...]


def gdn(z, conv_w, conv_state, a_log, dt_bias, norm_g, s0, batch, seq, tb, heads):
    L = min(CHUNK, seq)
    nt = seq // tb
    halo = CONV_C - 1
    gate_params = (jnp.zeros((2, LANES), F32).at[0, H_C:2 * H_C].set(a_log)
                   .at[1, H_C:2 * H_C].set(dt_bias))
    cw = conv_w.reshape(CONV_C, 3, H_C * DK_C).transpose(1, 0, 2)
    cs = conv_state.reshape(batch, halo, 3, H_C * DK_C).transpose(2, 0, 1, 3)

    width = heads * LANES

    def col(c0):
        return pl.BlockSpec((tb, width), lambda b, h, t: (b * nt + t, c0 // heads + h))

    st_spec = pl.BlockSpec((1, heads, DV_C, DK_C), lambda b, h, t: (b, h, 0, 0))
    o, st = pl.pallas_call(
        functools.partial(_gdn_kernel, L=L, n_chunks=tb // L, tb=tb, heads=heads),
        grid=(batch, H_C // heads, nt),
        in_specs=[col(CD_Q), col(CD_K), col(CD_V), col(CD_Z),
                  pl.BlockSpec((tb, LANES), lambda b, h, t: (b * nt + t, CD_GATES)),
                  pl.BlockSpec((3, CONV_C, width), lambda b, h, t: (0, 0, h)),
                  pl.BlockSpec((3, 1, halo, width), lambda b, h, t: (0, b, 0, h)),
                  pl.BlockSpec((2, LANES), lambda b, h, t: (0, 0)),
                  pl.BlockSpec((tb, tb), lambda b, h, t: (0, 0)),
                  pl.BlockSpec((1, DV_C), lambda b, h, t: (0, 0)),
                  st_spec],
        out_specs=[pl.BlockSpec((tb, width), lambda b, h, t: (b * nt + t, h)), st_spec],
        out_shape=[jax.ShapeDtypeStruct((batch * seq, H_C * DV_C), BF16),
                   jax.ShapeDtypeStruct((batch, H_C, DV_C, DK_C), F32)],
        scratch_shapes=[pltpu.VMEM((heads, DV_C, DK_C), F32),
                        pltpu.VMEM((3, SUBLANES + tb, width), F32),
                        pltpu.VMEM((3, tb, width), F32),
                        pltpu.VMEM((LANES, tb + (-tb % LANES)), F32)],
        compiler_params=_params("arbitrary", "arbitrary", "arbitrary"),
        name="gdn",
    )(z, z, z, z, z, cw, cs, gate_params, _chunk_tri(tb, L), norm_g.reshape(1, DV_C),
      jnp.swapaxes(s0, 2, 3))
    return o, jnp.swapaxes(st, 2, 3)


def _attn_kernel(q_ref, k_ref, v_ref, lp_ref, gn_ref, o_ref, ka_ref, vt_ref, m_ref, acc_ref,
                 *, seq, kv_len, tq, tk, lam_init, heads):
    h0 = pl.program_id(1) * heads
    qi = pl.program_id(2)
    past = kv_len - seq
    shift = int(math.log2(CHUNK))
    n_kb = kv_len // tk
    n_streams = 2 * heads

    @pl.when(qi == 0)
    def _():
        lane = _iota2((kv_len, 2 * D_HD), 1)
        k_idx = _iota2((kv_len, 1), 0)
        chunk_f = (k_idx >> shift).astype(F32)
        rem_f = (k_idx & (CHUNK - 1)).astype(F32)

        def pos_lanes(base):
            return jnp.where(lane == base, chunk_f, jnp.where(lane == base + 1, rem_f, 0.0))

        for g in range(heads):
            k = k_ref[:, g * DV_D:(g + 1) * DV_D]
            ka_ref[2 * g] = _bf(jnp.where(lane < D_HD, k, pos_lanes(D_HD)))
            ka_ref[2 * g + 1] = _bf(jnp.where(lane >= D_HD, k, pos_lanes(0)))
            vt = _transpose_rows(v_ref[:, g * DV_D:(g + 1) * DV_D])
            for jb in range(n_kb):
                vt_ref[g, jb, :DV_D, :] = _bf(vt[:, jb * tk:(jb + 1) * tk])
                vt_ref[g, jb, DV_D:, :] = jnp.ones((LANES, tk), BF16)

    lp = lp_ref[...]
    lam = (jnp.exp(jnp.sum(lp[0:1] * lp[1:2], axis=-1, keepdims=True))
           - jnp.exp(jnp.sum(lp[2:3] * lp[3:4], axis=-1, keepdims=True)) + lam_init)

    lane_q = _iota2((tq, 2 * D_HD), 1)
    qa = []
    slopes = []
    for g in range(heads):
        hf = jnp.full((1, 1), h0 + g + 1, jnp.int32).astype(F32)
        slope = jnp.exp(hf * (-8.0 / H_D * math.log(2.0)))
        slopes.append(slope)
        q = q_ref[:, g * DV_D:(g + 1) * DV_D] * (D_HD ** -0.5)

        def slope_lanes(base, slope=slope):
            return jnp.where(lane_q == base, slope * CHUNK,
                             jnp.where(lane_q == base + 1, slope, 0.0))

        qa.append(_bf(jnp.where(lane_q < D_HD, q, slope_lanes(D_HD))))
        qa.append(_bf(jnp.where(lane_q >= D_HD, q, slope_lanes(0))))
    q_start = past + qi * tq
    q_pos = q_start + _iota2((1, tq), 1)
    q_chunk = q_pos >> shift

    m_ref[...] = jnp.full(m_ref.shape, NEG_BIG, F32)
    acc_ref[...] = jnp.zeros(acc_ref.shape, F32)

    last_allowed = (((q_start + tq - 1) >> shift) << shift) + (CHUNK - 1)
    n_blocks = jnp.minimum(last_allowed, kv_len - 1) // tk + 1
    n_before = jnp.minimum((q_start + 1) // tk, n_blocks)

    def scores(j):
        rows = pl.ds(pl.multiple_of(j * tk, 2 * SUBLANES), tk)
        return [_mm_nt(ka_ref[s, rows, :], qa[s]) for s in range(n_streams)]

    def accumulate(j, s_t, mask):
        m_old = [m_ref[s] for s in range(n_streams)]
        m_new = [jnp.maximum(m_old[s], jnp.max(s_t[s], axis=0, keepdims=True))
                 for s in range(n_streams)]
        p_t = [jnp.exp(s_t[s] - m_new[s]) for s in range(n_streams)]
        if mask is not None:
            p_t = [jnp.where(mask, p, 0.0) for p in p_t]
        for s in range(n_streams):
            acc_ref[s] = (jnp.exp(m_old[s] - m_new[s]) * acc_ref[s]
                          + _mm(vt_ref[s // 2, j], _bf(p_t[s])))
            m_ref[s] = m_new[s]

    def block_before(j, carry):
        accumulate(j, scores(j), None)
        return carry

    def block_masked(j, carry):
        k_pos = j * tk + _iota2((tk, 1), 0)
        allowed = (k_pos >> shift) <= q_chunk
        late = jnp.maximum(k_pos - q_pos, 0).astype(F32)
        s_t = [jnp.where(allowed, s - (2.0 * slopes[idx // 2]) * late, NEG_BIG)
               for idx, s in enumerate(scores(j))]
        accumulate(j, s_t, allowed)
        return carry

    lax.fori_loop(0, n_before, block_before, 0)
    lax.fori_loop(n_before, n_blocks, block_masked, 0)
    pad = -tq % LANES
    gn = gn_ref[...]
    for g in range(heads):
        a0 = acc_ref[2 * g]
        a1 = acc_ref[2 * g + 1]
        o_t = (a0[:DV_D] / a0[DV_D:DV_D + 1] - lam * (a1[:DV_D] / a1[DV_D:DV_D + 1]))
        if pad:
            o_t = jnp.concatenate([o_t, jnp.zeros((DV_D, pad), F32)], axis=1)
        o = o_t.T[:tq]
        o_ref[:, g * DV_D:(g + 1) * DV_D] = (_rms(o, gn) * (1.0 - lam_init)).astype(BF16)


def diff_attention(zq, q_col, kd, k_col, vd, v_col, lam_params, norm_g, batch, seq, kv_len,
                   tq, tk, lam_init, heads):
    nq = seq // tq
    width = heads * DV_D
    return pl.pallas_call(
        functools.partial(_attn_kernel, seq=seq, kv_len=kv_len, tq=tq, tk=tk, lam_init=lam_init,
                          heads=heads),
        grid=(batch, H_D // heads, nq),
        in_specs=[pl.BlockSpec((tq, width), lambda b, h, i: (b * nq + i, q_col // heads + h)),
                  pl.BlockSpec((kv_len, width), lambda b, h, i: (b, k_col // heads + h)),
                  pl.BlockSpec((kv_len, width), lambda b, h, i: (b, v_col // heads + h)),
                  pl.BlockSpec((4, D_HD), lambda b, h, i: (0, 0)),
                  pl.BlockSpec((1, DV_D), lambda b, h, i: (0, 0))],
        out_specs=pl.BlockSpec((tq, width), lambda b, h, i: (b * nq + i, h)),
        out_shape=jax.ShapeDtypeStruct((batch * seq, H_D * DV_D), BF16),
        scratch_shapes=[pltpu.VMEM((2 * heads, kv_len, DV_D), BF16),
                        pltpu.VMEM((heads, kv_len // tk, DV_D + LANES, tk), BF16),
                        pltpu.VMEM((2 * heads, 1, tq), F32),
                        pltpu.VMEM((2 * heads, DV_D + LANES, tq), F32)],
        compiler_params=_params("arbitrary", "arbitrary", "arbitrary"),
        name="diff_attention",
    )(zq, kd, vd, lam_params, norm_g.reshape(1, DV_D))


def _pad_cols(w, total):
    return jnp.pad(w, ((0, 0), (0, total - w.shape[1])))


def _prep_w_in_ab(w):
    sizes = (H_A * DK_A, H_A * DK_A, H_A * DV_A, H_A * DV_A, H_B * DK_B, H_B * DK_B, H_B * DV_B,
             H_B, H_B, H_B * DV_B)
    aq, af, ai, ag, bq, bk, bv, big, bfg, bo = jnp.split(w, np_cumsum(sizes), axis=1)
    return _pad_cols(jnp.concatenate([aq, af, ai, ag, bq, bk, bv, bo, big, bfg], axis=1),
                     IN_AB_PAD).astype(BF16)


def _prep_w_in_cd(w):
    sizes = (H_C * (2 * DK_C + DV_C), H_C, H_C, H_C * DV_C, H_D * 2 * D_HD, H_D * 2 * D_HD,
             H_D * DV_D)
    cqkv, cb, ca, cz, dq, dk, dv = jnp.split(w, np_cumsum(sizes), axis=1)
    return _pad_cols(jnp.concatenate([cqkv, cz, dq, dk, dv, cb, ca], axis=1),
                     IN_CD_PAD).astype(BF16)


def np_cumsum(sizes):
    out, acc = [], 0
    for s in sizes[:-1]:
        acc += s
        out.append(acc)
    return out


def _tiles(n_rows, seq):
    tm = min(512, n_rows)
    tb = min(512, seq)
    return tm, tb


def _layer_ab(x, st, p, lower_bounds, batch, seq, layer):
    n = batch * seq
    tm, tb = _tiles(n, seq)
    z = norm_matmul(x, p['norm_mix'], p['w_in'], tm, IN_AB_PAD // 3)
    o_a, s_new = hgrn2(z, lower_bounds, p['hgrn_norm'], st['hgrn'], batch, seq, tb, layer // 2)
    o_b, c_new, n_new, m_new = mlstm(z, p['i_bias'], p['f_bias'], p['mlstm_norm'], st['mlstm_c'],
                                     st['mlstm_n'], st['mlstm_m'], batch, seq, tb)
    x = matmul_residual((o_a, o_b), p['w_out'], x, p['norm_ffn'], tm, 1024, False)
    return x, (s_new, c_new, n_new, m_new)


def _layer_cd(x, st, p, batch, seq, layer):
    n = batch * seq
    tm, tb = _tiles(n, seq)
    z = norm_matmul(x, p['norm_mix'], p['w_in'], tm, IN_CD_PAD // 3)
    heads = max(1, min(H_C, (16 * min(CHUNK, seq)) // tb))
    o_c, s_new = gdn(z, p['conv_w'], st['gdn_conv'], p['a_log'], p['dt_bias'], p['gdn_norm'],
                     st['gdn'], batch, seq, tb, heads)
    z3 = z.reshape(batch, seq, IN_CD_PAD)
    conv_new = z3[:, seq - (CONV_C - 1):, :H_C * (2 * DK_C + DV_C)]
    k_new = z3[:, :, CD_DK * LANES:CD_DV * LANES]
    v_new = z3[:, :, CD_DV * LANES:CD_GATES * LANES]
    past = st['k_cache'].shape[1]
    lam_init = 0.8 - 0.6 * math.exp(-0.3 * layer)
    if past == 0:
        o_d = diff_attention(z, CD_DQ, z, CD_DK, z, CD_DV, p['lam'], p['diff_norm'], batch, seq,
                             seq, min(512, seq), min(512, seq), lam_init, 2)
    else:
        kv_len = past + seq
        kd = jnp.concatenate([st['k_cache'].reshape(batch, past, -1), k_new], axis=1)
        vd = jnp.concatenate([st['v_cache'].reshape(batch, past, -1), v_new], axis=1)
        o_d = diff_attention(z, CD_DQ, kd.reshape(batch * kv_len, -1), 0,
                             vd.reshape(batch * kv_len, -1), 0, p['lam'], p['diff_norm'], batch,
                             seq, kv_len, seq, kv_len, lam_init, 4)
    x = matmul_residual((o_c, o_d), p['w_out'], x, p['norm_ffn'], tm, 1024, False)
    return x, (s_new, conv_new, k_new.reshape(batch, seq, H_D, 2 * D_HD),
               v_new.reshape(batch, seq, H_D, DV_D))


def _ffn(x, state, p, batch, seq, final_g):
    n = batch * seq
    tm, _ = _tiles(n, seq)
    act, conv_new = ffn_up(x, p['norm_ffn'], p['ffn_up'], p['ffn_conv_w'], p['ffn_conv_b'], state,
                           batch, seq, tm, 512)
    g = p['norm_ffn'] if final_g is None else final_g
    x = matmul_residual((act,), p['ffn_down'], x, g, tm, D_FF // 2, final_g is not None)
    return x, conv_new


def _run_group(x, st, params, lower_bounds, norm_final):
    batch, seq, _ = x.shape
    x = x.reshape(batch * seq, D_MODEL)
    new_states = []
    for layer, (p, s) in enumerate(zip(params, st)):
        if layer % 2 == 0:
            x, new_mix = _layer_ab(x, s, p, lower_bounds, batch, seq, layer)
        else:
            x, new_mix = _layer_cd(x, s, p, batch, seq, layer)
        last = layer == len(params) - 1
        x, new_ffn = _ffn(x, s['ffn_conv'], p, batch, seq, norm_final if last else None)
        new_states.append(new_mix + (new_ffn,))
    return x.reshape(batch, seq, D_MODEL), new_states


def kernel(x_prompt, x_sample, state_hgrn_0, state_mlstm_c_0, state_mlstm_n_0, state_mlstm_m_0,
           state_ffn_conv_0, state_gdn_1, state_gdn_conv_1, cache_k_1, cache_v_1, state_ffn_conv_1,
           hgrn_lower_bounds, norm_mix_0, w_in_0, mlstm_i_bias_0, mlstm_f_bias_0, hgrn_norm_0,
           mlstm_norm_0, w_out_0, norm_ffn_0, ffn_up_0, ffn_conv_w_0, ffn_conv_b_0, ffn_down_0,
           norm_mix_1, w_in_1, gdn_conv_w_1, gdn_a_log_1, gdn_dt_bias_1, gdn_norm_1, diff_lambda_1,
           diff_norm_1, w_out_1, norm_ffn_1, ffn_up_1, ffn_conv_w_1, ffn_conv_b_1, ffn_down_1,
           norm_final):
    params = [
        dict(norm_mix=norm_mix_0, w_in=_prep_w_in_ab(w_in_0), i_bias=mlstm_i_bias_0,
             f_bias=mlstm_f_bias_0, hgrn_norm=hgrn_norm_0, mlstm_norm=mlstm_norm_0,
             w_out=w_out_0.astype(BF16), norm_ffn=norm_ffn_0, ffn_up=ffn_up_0.astype(BF16),
             ffn_conv_w=ffn_conv_w_0, ffn_conv_b=ffn_conv_b_0, ffn_down=ffn_down_0.astype(BF16)),
        dict(norm_mix=norm_mix_1, w_in=_prep_w_in_cd(w_in_1), conv_w=gdn_conv_w_1,
             a_log=gdn_a_log_1, dt_bias=gdn_dt_bias_1, gdn_norm=gdn_norm_1, lam=diff_lambda_1,
             diff_norm=diff_norm_1, w_out=w_out_1.astype(BF16), norm_ffn=norm_ffn_1,
             ffn_up=ffn_up_1.astype(BF16), ffn_conv_w=ffn_conv_w_1, ffn_conv_b=ffn_conv_b_1,
             ffn_down=ffn_down_1.astype(BF16)),
    ]
    bp = x_prompt.shape[0]
    st_prompt = [
        dict(hgrn=jnp.zeros((bp, H_A, DK_A, DV_A), F32), mlstm_c=jnp.zeros((bp, H_B, DK_B, DV_B), F32),
             mlstm_n=jnp.zeros((bp, H_B, DK_B), F32), mlstm_m=jnp.full((bp, H_B), NEG_BIG, F32),
             ffn_conv=jnp.zeros((bp, CONV_FFN - 1, D_FF), F32)),
        dict(gdn=jnp.zeros((bp, H_C, DK_C, DV_C), F32),
             gdn_conv=jnp.zeros((bp, CONV_C - 1, H_C * (2 * DK_C + DV_C)), F32),
             k_cache=jnp.zeros((bp, 0, H_D, 2 * D_HD), F32), v_cache=jnp.zeros((bp, 0, H_D, DV_D), F32),
             ffn_conv=jnp.zeros((bp, CONV_FFN - 1, D_FF), F32)),
    ]
    st_sample = [
        dict(hgrn=state_hgrn_0, mlstm_c=state_mlstm_c_0, mlstm_n=state_mlstm_n_0,
             mlstm_m=state_mlstm_m_0, ffn_conv=state_ffn_conv_0),
        dict(gdn=state_gdn_1, gdn_conv=state_gdn_conv_1, k_cache=cache_k_1, v_cache=cache_v_1,
             ffn_conv=state_ffn_conv_1),
    ]
    y_p, new_p = _run_group(x_prompt, st_prompt, params, hgrn_lower_bounds, norm_final)
    y_s, new_s = _run_group(x_sample, st_sample, params, hgrn_lower_bounds, norm_final)
    (hgrn_p, c_p, n_p, m_p, f0_p), (gdn_p, gc_p, k_p, v_p, f1_p) = new_p
    (hgrn_s, c_s, n_s, m_s, f0_s), (gdn_s, gc_s, k_s, v_s, f1_s) = new_s
    return (y_p, y_s, hgrn_p, hgrn_s, c_p, c_s, n_p, n_s, m_p, m_s, f0_p, f0_s, gdn_p, gdn_s,
            gc_p, gc_s, k_p, k_s, v_p, v_s, f1_p, f1_s)
```

```python
import functools
import math

import jax
import jax.numpy as jnp
from jax import lax
from jax.experimental import pallas as pl
from jax.experimental.pallas import tpu as pltpu

F32 = jnp.float32
BF16 = jnp.bfloat16
HI = lax.Precision.HIGHEST

D_MODEL = 2048
CHUNK = 64
H_A, DK_A, DV_A = 8, 128, 128
H_B, DK_B, DV_B = 4, 128, 256
H_C, DK_C, DV_C = 8, 128, 128
CONV_C = 4
H_D, D_HD = 8, 64
DV_D = 2 * D_HD
D_FF = 5632
CONV_FFN = 3
EPS = 1e-6
NEG_BIG = -1e30

LANES = 128
SUBLANES = 8
VMEM_LIMIT = 56 * 1024 * 1024
MIX_CHUNK = 128

AB_Q, AB_F, AB_I, AB_G = 0, 8, 16, 24
AB_BQ, AB_BK = 32, 36
AB_BV, AB_BO = 20, 24
AB_GATES = 56
IN_AB_PAD = 57 * LANES
CD_Q, CD_K, CD_V, CD_Z = 0, 8, 16, 24
CD_DQ, CD_DK, CD_DV = 32, 40, 48
CD_GATES = 56
IN_CD_PAD = 57 * LANES


def _mm(a, b, prec=None):
    return lax.dot_general(a, b, (((1,), (0,)), ((), ())), precision=prec,
                           preferred_element_type=F32)


def _mm_nt(a, b, prec=None):
    return lax.dot_general(a, b, (((1,), (1,)), ((), ())), precision=prec,
                           preferred_element_type=F32)


def _mm_tn(a, b, prec=None):
    return lax.dot_general(a, b, (((0,), (0,)), ((), ())), precision=prec,
                           preferred_element_type=F32)


def _sigmoid(x):
    return 1.0 / (1.0 + jnp.exp(-x))


def _silu(x):
    return x * _sigmoid(x)


def _softplus(x):
    return jnp.maximum(x, 0.0) + jnp.log1p(jnp.exp(-jnp.abs(x)))


def _rms(x, g):
    return x * lax.rsqrt(jnp.mean(x * x, axis=-1, keepdims=True) + EPS) * g


def _params(*sem):
    return pltpu.CompilerParams(dimension_semantics=sem, vmem_limit_bytes=VMEM_LIMIT)


def _norm_matmul_kernel(x_ref, g_ref, w_ref, o_ref, h_ref):
    @pl.when(pl.program_id(1) == 0)
    def _():
        h_ref[...] = _rms(x_ref[...], g_ref[...]).astype(BF16)

    o_ref[...] = _mm(h_ref[...], w_ref[...])


def norm_matmul(x, g, w, tm, tn):
    n, k = x.shape
    m = w.shape[1]
    return pl.pallas_call(
        _norm_matmul_kernel,
        grid=(n // tm, m // tn),
        in_specs=[pl.BlockSpec((tm, k), lambda i, j: (i, 0)),
                  pl.BlockSpec((1, k), lambda i, j: (0, 0)),
                  pl.BlockSpec((k, tn), lambda i, j: (0, j))],
        out_specs=pl.BlockSpec((tm, tn), lambda i, j: (i, j)),
        out_shape=jax.ShapeDtypeStruct((n, m), F32),
        scratch_shapes=[pltpu.VMEM((tm, k), BF16)],
        compiler_params=_params("arbitrary", "arbitrary"),
        name="norm_matmul",
    )(x, g.reshape(1, k), w)


def _matmul_res_kernel(*refs, n_parts, final_norm):
    y_refs = refs[:n_parts]
    w_ref, x_ref, g_ref, o_ref = refs[n_parts:]
    k_part = y_refs[0].shape[1]
    acc = x_ref[...]
    for part, y_ref in enumerate(y_refs):
        acc = acc + _mm(y_ref[...], w_ref[part * k_part:(part + 1) * k_part, :])
    o_ref[...] = _rms(acc, g_ref[...]) if final_norm else acc


def matmul_residual(ys, w, x, g, tm, final_norm):
    n, k_part = ys[0].shape
    k, d = w.shape
    n_parts = len(ys)
    return pl.pallas_call(
        functools.partial(_matmul_res_kernel, n_parts=n_parts, final_norm=final_norm),
        grid=(n // tm,),
        in_specs=[pl.BlockSpec((tm, k_part), lambda i: (i, 0)) for _ in range(n_parts)]
        + [pl.BlockSpec((k, d), lambda i: (0, 0), pipeline_mode=pl.Buffered(1)),
           pl.BlockSpec((tm, d), lambda i: (i, 0)),
           pl.BlockSpec((1, d), lambda i: (0, 0))],
        out_specs=pl.BlockSpec((tm, d), lambda i: (i, 0)),
        out_shape=jax.ShapeDtypeStruct((n, d), F32),
        compiler_params=_params("arbitrary"),
        name="matmul_residual",
    )(*ys, w, x, g.reshape(1, d))


def _ffn_up_kernel(x_ref, g_ref, wa_ref, wu_ref, cw_ref, cb_ref, st_ref, act_ref, tail_ref,
                   h_ref, ext_ref, carry_ref, *, nseq, rows, tiles_per_seq):
    i = pl.program_id(0)
    j = pl.program_id(1)
    halo = CONV_FFN - 1

    @pl.when(j == 0)
    def _():
        h_ref[...] = _rms(x_ref[...], g_ref[...]).astype(BF16)

    if tiles_per_seq > 1:
        @pl.when(i == 0)
        def _():
            carry_ref[j] = jnp.zeros(carry_ref.shape[1:], F32)

    h = h_ref[...]
    a = _mm(h, wa_ref[...])
    u = _mm(h, wu_ref[...])
    cw = cw_ref[...]
    cb = cb_ref[...]
    for s in range(nseq):
        lo = s * rows
        ext_ref[s, SUBLANES:, :] = a[lo:lo + rows]
        prev = st_ref[s]
        if tiles_per_seq > 1:
            prev = jnp.where((i % tiles_per_seq) == 0, prev, carry_ref[j, SUBLANES - halo:, :])
        ext_ref[s, SUBLANES - halo:SUBLANES, :] = prev
        conv = cb
        for t in range(CONV_FFN):
            off = SUBLANES - halo + t
            conv = conv + ext_ref[s, off:off + rows, :] * cw[t:t + 1]
        act_ref[lo:lo + rows, :] = (_silu(conv) * u[lo:lo + rows]).astype(BF16)
        tail_ref[s] = ext_ref[s, SUBLANES + rows - halo:SUBLANES + rows, :]
        carry_ref[j] = ext_ref[s, rows:rows + SUBLANES, :]


def ffn_up(x, g, w_up, conv_w, conv_b, state, batch, seq, tm, tf):
    n, k = x.shape
    nff = D_FF // tf
    if tm >= seq:
        nseq, rows, tiles_per_seq = tm // seq, seq, 1
    else:
        nseq, rows, tiles_per_seq = 1, tm, seq // tm
    ntiles = n // tm
    halo = CONV_FFN - 1
    act, tail = pl.pallas_call(
        functools.partial(_ffn_up_kernel, nseq=nseq, rows=rows, tiles_per_seq=tiles_per_seq),
        grid=(ntiles, nff),
        in_specs=[pl.BlockSpec((tm, k), lambda i, j: (i, 0)),
                  pl.BlockSpec((1, k), lambda i, j: (0, 0)),
                  pl.BlockSpec((k, tf), lambda i, j: (0, j)),
                  pl.BlockSpec((k, tf), lambda i, j: (0, j + nff)),
                  pl.BlockSpec((CONV_FFN, tf), lambda i, j: (0, j)),
                  pl.BlockSpec((1, tf), lambda i, j: (0, j)),
                  pl.BlockSpec((nseq, halo, tf), lambda i, j: (i // tiles_per_seq, 0, j))],
        out_specs=[pl.BlockSpec((tm, tf), lambda i, j: (i, j)),
                   pl.BlockSpec((nseq, halo, tf), lambda i, j: (i, 0, j))],
        out_shape=[jax.ShapeDtypeStruct((n, D_FF), BF16),
                   jax.ShapeDtypeStruct((ntiles * nseq, halo, D_FF), F32)],
        scratch_shapes=[pltpu.VMEM((tm, k), BF16),
                        pltpu.VMEM((nseq, SUBLANES + rows, tf), F32),
                        pltpu.VMEM((nff, SUBLANES, tf), F32)],
        compiler_params=_params("arbitrary", "arbitrary"),
        name="ffn_up",
    )(x, g.reshape(1, k), w_up, w_up, conv_w, conv_b.reshape(1, D_FF), state)
    new_state = tail.reshape(batch, -1, halo, D_FF)[:, -1]
    return act, new_state


def _iota2(shape, dim):
    return lax.broadcasted_iota(jnp.int32, shape, dim)


def _tril(n):
    return (_iota2((n, n), 1) <= _iota2((n, n), 0)).astype(F32)


def _lane_col(x, lane):
    return jnp.sum(jnp.where(_iota2(x.shape, 1) == lane, x, 0.0), axis=-1, keepdims=True)


def _bf(x):
    return x.astype(BF16)


def _split3(x):
    h1 = _bf(x)
    r1 = x - h1.astype(F32)
    h2 = _bf(r1)
    h3 = _bf(r1 - h2.astype(F32))
    return h1, h2, h3


def _mm_sel(sel, x):
    n = x.shape[1]
    y = _mm(_bf(sel), jnp.concatenate(_split3(x), axis=1))
    return y[:, :n] + y[:, n:2 * n] + y[:, 2 * n:]


def _transpose_rows(x):
    r = x.shape[0]
    pad = -r % LANES
    if pad:
        x = jnp.concatenate([x, jnp.zeros((pad, x.shape[1]), x.dtype)], axis=0)
    return x.T


def _chunk_cumsum_rows(row, tri):
    r = row.shape[1]
    h1, h2, h3 = (p.astype(F32) for p in _split3(row))
    sub = _iota2((2 * SUBLANES, r), 0)
    parts = jnp.where(sub == 0, h1, jnp.where(sub == 1, h2, jnp.where(sub == 2, h3, 0.0)))
    y = _mm(_bf(parts), tri)
    return y[0:1] + y[1:2] + y[2:3]


def _row_to_col(row):
    n = row.shape[1]
    eye = _iota2((n, n), 0) == _iota2((n, n), 1)
    return jnp.sum(jnp.where(eye, row, 0.0), axis=-1, keepdims=True)


def _chunk_tri(tb, chunk):
    s = lax.broadcasted_iota(jnp.int32, (tb, tb), 0)
    t = lax.broadcasted_iota(jnp.int32, (tb, tb), 1)
    return ((s <= t) & (s // chunk == t // chunk)).astype(BF16)


def _hgrn_kernel(q_ref, f_ref, i_ref, g_ref, lb_ref, gn_ref, s0_ref, o_ref, s_out_ref, s_ref,
                 *, L, n_chunks, lb_index, heads):
    tb = pl.program_id(2)

    @pl.when(tb == 0)
    def _():
        s_ref[...] = s0_ref[0]

    lbp = lb_ref[...]
    e = jnp.exp(lbp - jnp.max(lbp, axis=0, keepdims=True))
    lb_all = jnp.sum(e[:lb_index + 1], axis=0, keepdims=True) / jnp.sum(e, axis=0, keepdims=True)

    sizes = [L >> l for l in range(int(math.log2(L)))]
    ti = _iota2((L, L), 0)
    si = _iota2((L, L), 1)
    tcol = _iota2((L, 1), 0)
    mats = [si <= ti]
    for sz in sizes:
        mats.append(si <= ((ti & ~(sz - 1)) + (sz // 2 - 1)))
    cum_mat = jnp.concatenate([m.astype(F32) for m in mats], axis=0)
    eye = ti == si
    gn = gn_ref[...]

    chunks = []
    for c in range(n_chunks):
        rows = pl.ds(c * L, L)
        for g in range(heads):
            lanes = slice(g * LANES, (g + 1) * LANES)
            lb = lb_all[:, lanes]
            q = q_ref[rows, lanes]
            f = lb + (1.0 - lb) * _sigmoid(f_ref[rows, lanes])
            k = 1.0 - f
            cums = _mm_sel(cum_mat, jnp.log(f))
            chunks.append(dict(
                rows=rows, lanes=lanes, g=g, q=q, k=k, cums=cums, b=cums[:L],
                vb=_bf(i_ref[rows, lanes]),
                scores=jnp.where(eye, jnp.sum(q * k, axis=-1, keepdims=True), 0.0)))
    for l, sz in enumerate(sizes):
        right = (tcol & (sz - 1)) >= sz // 2
        same = (ti & ~(sz - 1)) == (si & ~(sz - 1))
        for ch in chunks:
            w = jnp.exp(-jnp.abs(ch['b'] - ch['cums'][(l + 1) * L:(l + 2) * L]))
            qm = jnp.where(right, ch['q'] * w, 0.0)
            km = jnp.where(right, 0.0, ch['k'] * w)
            ch['scores'] = ch['scores'] + jnp.where(same, _mm_nt(_bf(qm), _bf(km)), 0.0)
    for ch in chunks:
        bl = ch['b'][L - 1:L]
        ch['update'] = _mm_tn(_bf(ch['k'] * jnp.exp(bl - ch['b'])), ch['vb'])
        ch['decay'] = _row_to_col(jnp.exp(bl))
    for ch in chunks:
        s = s_ref[ch['g']]
        o = _mm(jnp.concatenate([_bf(ch['q'] * jnp.exp(ch['b'])), _bf(ch['scores'])], axis=1),
                jnp.concatenate([_bf(s), ch['vb']], axis=0))
        s_ref[ch['g']] = ch['decay'] * s + ch['update']
        o_ref[ch['rows'], ch['lanes']] = (
            _rms(o, gn) * _silu(g_ref[ch['rows'], ch['lanes']])).astype(BF16)

    @pl.when(tb == pl.num_programs(2) - 1)
    def _():
        s_out_ref[0] = s_ref[...]


def hgrn2(z, lower_bounds, norm_g, s0, batch, seq, tb, lb_index, heads):
    L = min(MIX_CHUNK, seq)
    nt = seq // tb
    width = heads * LANES

    def col(c0):
        return pl.BlockSpec((tb, width), lambda b, h, t: (b * nt + t, c0 // heads + h))

    st_spec = pl.BlockSpec((1, heads, DK_A, DV_A), lambda b, h, t: (b, h, 0, 0))
    return pl.pallas_call(
        functools.partial(_hgrn_kernel, L=L, n_chunks=tb // L, lb_index=lb_index, heads=heads),
        grid=(batch, H_A // heads, nt),
        in_specs=[col(AB_Q), col(AB_F), col(AB_I), col(AB_G),
                  pl.BlockSpec((lower_bounds.shape[0], width), lambda b, h, t: (0, h)),
                  pl.BlockSpec((1, DV_A), lambda b, h, t: (0, 0)),
                  st_spec],
        out_specs=[pl.BlockSpec((tb, width), lambda b, h, t: (b * nt + t, h)), st_spec],
        out_shape=[jax.ShapeDtypeStruct((batch * seq, H_A * DV_A), BF16),
                   jax.ShapeDtypeStruct((batch, H_A, DK_A, DV_A), F32)],
        scratch_shapes=[pltpu.VMEM((heads, DK_A, DV_A), F32)],
        compiler_params=_params("arbitrary", "arbitrary", "arbitrary"),
        name="hgrn2",
    )(z, z, z, z, lower_bounds, norm_g.reshape(1, DV_A), s0)


def _mlstm_kernel(q_ref, k_ref, v_ref, og_ref, gt_ref, bias_ref, tri_ref, gn_ref, c0_ref, n0_ref,
                  m0_ref, o_ref, c_out_ref, n_out_ref, m_out_ref, c_ref, n_ref, m_ref, gtt_ref,
                  *, L, n_chunks, heads):
    h0 = pl.program_id(1) * heads
    tb = pl.program_id(2)

    @pl.when(tb == 0)
    def _():
        c_ref[...] = c0_ref[0]
        n_ref[...] = n0_ref[0]
        m_ref[...] = m0_ref[0]

    n_rows = L * n_chunks
    gtt_ref[...] = _transpose_rows(gt_ref[...] + bias_ref[...])
    ig_rows, b_rows = [], []
    for g in range(heads):
        ig_rows.append(gtt_ref[pl.ds(h0 + g, 1), :][:, :n_rows])
        f_rows = gtt_ref[pl.ds(H_B + h0 + g, 1), :][:, :n_rows]
        logf = jnp.minimum(f_rows, 0.0) - jnp.log1p(jnp.exp(-jnp.abs(f_rows)))
        b_rows.append(_chunk_cumsum_rows(logf, tri_ref[...]))

    causal = _iota2((L, L), 1) <= _iota2((L, L), 0)
    gn = gn_ref[...]
    scale = DK_B ** -0.5

    chunks = []
    for c in range(n_chunks):
        rows = pl.ds(c * L, L)
        for g in range(heads):
            q = q_ref[rows, g * DK_B:(g + 1) * DK_B] * scale
            k = k_ref[rows, g * DK_B:(g + 1) * DK_B]
            vb = _bf(v_ref[rows, g * DV_B:(g + 1) * DV_B])
            ig_row = ig_rows[g][:, c * L:(c + 1) * L]
            b_row = b_rows[g][:, c * L:(c + 1) * L]
            b_col = _row_to_col(b_row)
            d = b_col - b_row + ig_row
            d_max = jnp.max(jnp.where(causal, d, -jnp.inf), axis=-1, keepdims=True)
            p = jnp.exp(jnp.where(causal, d - d_max, -jnp.inf)) * _mm_nt(_bf(q), _bf(k))
            b_last = b_row[:, L - 1:L]
            wk_max = jnp.max(b_last - b_row + ig_row, axis=-1, keepdims=True)
            ks = jnp.exp(b_last - b_col + _row_to_col(ig_row) - wk_max) * k
            chunks.append(dict(
                rows=rows, g=g, q=q, b_col=b_col, d_max=d_max, b_last=b_last, wk_max=wk_max,
                p_sum=jnp.sum(p, axis=-1, keepdims=True), pv=_mm(_bf(p), vb),
                k_sum=jnp.sum(ks, axis=0, keepdims=True), kv=_mm_tn(_bf(ks), vb)))

    for ch in chunks:
        g = ch['g']
        q = ch['q']
        cs = c_ref[g]
        n = n_ref[g]
        m_prev = m_ref[g][:, 0:1]
        inter = ch['b_col'] + m_prev
        m_t = jnp.maximum(ch['d_max'], inter)
        w_intra = jnp.exp(ch['d_max'] - m_t)
        w_inter = jnp.exp(inter - m_t)
        num = w_intra * ch['pv'] + w_inter * _mm(_bf(q), _bf(cs))
        den = w_intra * ch['p_sum'] + w_inter * jnp.sum(q * n, axis=-1, keepdims=True)
        hid = num / jnp.maximum(jnp.abs(den), jnp.exp(-m_t))
        m_new = jnp.maximum(ch['b_last'] + m_prev, ch['wk_max'])
        w_new = jnp.exp(ch['wk_max'] - m_new)
        carry_w = jnp.exp(ch['b_last'] + m_prev - m_new)
        c_ref[g] = carry_w * cs + w_new * ch['kv']
        n_ref[g] = carry_w * n + w_new * ch['k_sum']
        m_ref[g] = jnp.broadcast_to(m_new, (1, LANES))
        lanes = slice(g * DV_B, (g + 1) * DV_B)
        o_ref[ch['rows'], lanes] = (
            _rms(hid, gn) * _sigmoid(og_ref[ch['rows'], lanes])).astype(BF16)

    @pl.when(tb == pl.num_programs(2) - 1)
    def _():
        c_out_ref[0] = c_ref[...]
        n_out_ref[0] = n_ref[...]
        m_out_ref[0] = m_ref[...]


def mlstm(z, i_bias, f_bias, norm_g, c0, n0, m0, batch, seq, tb, heads):
    L = min(MIX_CHUNK, seq)
    nt = seq // tb
    bias = jnp.zeros((1, LANES), F32).at[0, :H_B].set(i_bias).at[0, H_B:2 * H_B].set(f_bias)

    def col(c0_, width):
        return pl.BlockSpec((tb, heads * width), lambda b, h, t: (b * nt + t, c0_ // heads + h))

    def state(r, c):
        return pl.BlockSpec((1, heads, r, c), lambda b, h, t: (b, h, 0, 0))

    o, cs, n, m = pl.pallas_call(
        functools.partial(_mlstm_kernel, L=L, n_chunks=tb // L, heads=heads),
        grid=(batch, H_B // heads, nt),
        in_specs=[col(AB_BQ, DK_B), col(AB_BK, DK_B), col(AB_BV, DV_B), col(AB_BO, DV_B),
                  pl.BlockSpec((tb, LANES), lambda b, h, t: (b * nt + t, AB_GATES)),
                  pl.BlockSpec((1, LANES), lambda b, h, t: (0, 0)),
                  pl.BlockSpec((tb, tb), lambda b, h, t: (0, 0)),
                  pl.BlockSpec((1, DV_B), lambda b, h, t: (0, 0)),
                  state(DK_B, DV_B), state(1, DK_B), state(1, LANES)],
        out_specs=[pl.BlockSpec((tb, heads * DV_B), lambda b, h, t: (b * nt + t, h)),
                   state(DK_B, DV_B), state(1, DK_B), state(1, LANES)],
        out_shape=[jax.ShapeDtypeStruct((batch * seq, H_B * DV_B), BF16),
                   jax.ShapeDtypeStruct((batch, H_B, DK_B, DV_B), F32),
                   jax.ShapeDtypeStruct((batch, H_B, 1, DK_B), F32),
                   jax.ShapeDtypeStruct((batch, H_B, 1, LANES), F32)],
        scratch_shapes=[pltpu.VMEM((heads, DK_B, DV_B), F32), pltpu.VMEM((heads, 1, DK_B), F32),
                        pltpu.VMEM((heads, 1, LANES), F32),
                        pltpu.VMEM((LANES, tb + (-tb % LANES)), F32)],
        compiler_params=_params("arbitrary", "arbitrary", "arbitrary"),
        name="mlstm",
    )(z, z, z, z, z, bias, _chunk_tri(tb, L), norm_g.reshape(1, DV_B), c0,
      n0.reshape(batch, H_B, 1, DK_B),
      jnp.broadcast_to(m0.reshape(batch, H_B, 1, 1), (batch, H_B, 1, LANES)))
    return o, cs, n.reshape(batch, H_B, DK_B), m[:, :, 0, 0]


def _gdn_kernel(q_ref, k_ref, v_ref, z_ref, gt_ref, cw_ref, cs_ref, gp_ref, tri_ref, gn_ref, s0_ref,
                o_ref, s_out_ref, st_ref, ext_ref, cv_ref, lgt_ref, *, L, n_chunks, tb, heads):
    h0 = pl.program_id(1) * heads
    t = pl.program_id(2)
    halo = CONV_C - 1

    @pl.when(t == 0)
    def _():
        st_ref[...] = s0_ref[0]
        for p in range(3):
            ext_ref[p, SUBLANES - halo:SUBLANES, :] = cs_ref[p, 0]

    srcs = (q_ref, k_ref, v_ref)
    for p in range(3):
        ext_ref[p, SUBLANES:, :] = srcs[p][...]
        cw = cw_ref[p]
        acc = None
        for j in range(CONV_C):
            off = SUBLANES - halo + j
            term = ext_ref[p, off:off + tb, :] * cw[j:j + 1]
            acc = term if acc is None else acc + term
        cv_ref[p] = _silu(acc)
    ti = _iota2((L, L), 0)
    si = _iota2((L, L), 1)
    gp = gp_ref[...]
    gn = gn_ref[...]

    lgt_ref[...] = _transpose_rows(-jnp.exp(gp[0:1]) * _softplus(gt_ref[...] + gp[1:2]))
    b_rows = [_chunk_cumsum_rows(lgt_ref[pl.ds(H_C + h0 + g, 1), :][:, :tb], tri_ref[...])
              for g in range(heads)]

    def l2n(x):
        return x * lax.rsqrt(jnp.sum(x * x, axis=-1, keepdims=True) + EPS)

    chunks = []
    for c in range(n_chunks):
        rows = pl.ds(c * L, L)
        sig = _sigmoid(gt_ref[rows, :])
        for g in range(heads):
            lanes = slice(g * LANES, (g + 1) * LANES)
            q = l2n(cv_ref[0, rows, lanes]) * (DK_C ** -0.5)
            k = l2n(cv_ref[1, rows, lanes])
            v = cv_ref[2, rows, lanes]
            beta = _lane_col(sig, h0 + g)
            b_row = b_rows[g][:, c * L:(c + 1) * L]
            b_col = _row_to_col(b_row)
            dec_incl = jnp.exp(jnp.where(si <= ti, b_col - b_row, -jnp.inf))
            eb = jnp.exp(b_col)
            kb = _bf(k)
            qb = _bf(q)
            kq = _mm_nt(jnp.concatenate([kb, qb], axis=0), kb)
            b_last = b_row[:, L - 1:L]
            chunks.append(dict(
                rows=rows, lanes=lanes, g=g, qb=qb, eb=eb, decay=jnp.exp(b_last),
                pw=-(beta * kq[:L] * jnp.where(si < ti, dec_incl, 0.0)),
                attn=_bf(kq[L:] * dec_incl),
                x=jnp.concatenate([beta * v, (beta * eb) * k], axis=1),
                k_dec=_bf(k * jnp.exp(b_last - b_col))))

    eye = jnp.where(ti == si, 1.0, 0.0)
    n_steps = int(math.log2(L))
    for ch in chunks:
        ch['inv'] = eye + ch['pw']
        pwb = _bf(ch['pw'])
        ch['pw'] = _mm(pwb, pwb)
    for step in range(1, n_steps):
        for ch in chunks:
            inv, pw = ch['inv'], ch['pw']
            if step == n_steps - 1:
                ch['inv'] = inv + _mm(_bf(pw), _bf(inv))
            else:
                y = _mm(_bf(pw), jnp.concatenate([_bf(inv), _bf(pw)], axis=1))
                ch['inv'] = inv + y[:, :L]
                ch['pw'] = y[:, L:]
    for ch in chunks:
        ch['x'] = _mm(_bf(ch['inv']), _bf(ch['x']))

    for ch in chunks:
        w = ch['x'][:, :DV_C]
        u = ch['x'][:, DV_C:]
        st = st_ref[ch['g']]
        uq = _mm_nt(jnp.concatenate([_bf(u), ch['qb']], axis=0), _bf(st))
        db = _bf(w - uq[:L])
        o = ch['eb'] * uq[L:] + _mm(ch['attn'], db)
        st_ref[ch['g']] = ch['decay'] * st + _mm_tn(db, ch['k_dec'])
        o_ref[ch['rows'], ch['lanes']] = (
            _rms(o, gn) * _silu(z_ref[ch['rows'], ch['lanes']])).astype(BF16)

    for p in range(3):
        ext_ref[p, 0:SUBLANES, :] = ext_ref[p, tb:tb + SUBLANES, :]

    @pl.when(t == pl.num_programs(2) - 1)
    def _():
        s_out_ref[0] = st_ref[...]


def gdn(z, conv_w, conv_state, a_log, dt_bias, norm_g, s0, batch, seq, tb, heads):
    L = min(CHUNK, seq)
    nt = seq // tb
    halo = CONV_C - 1
    gate_params = (jnp.zeros((2, LANES), F32).at[0, H_C:2 * H_C].set(a_log)
                   .at[1, H_C:2 * H_C].set(dt_bias))
    cw = conv_w.reshape(CONV_C, 3, H_C * DK_C).transpose(1, 0, 2)
    cs = conv_state.reshape(batch, halo, 3, H_C * DK_C).transpose(2, 0, 1, 3)

    width = heads * LANES

    def col(c0):
        return pl.BlockSpec((tb, width), lambda b, h, t: (b * nt + t, c0 // heads + h))

    st_spec = pl.BlockSpec((1, heads, DV_C, DK_C), lambda b, h, t: (b, h, 0, 0))
    o, st = pl.pallas_call(
        functools.partial(_gdn_kernel, L=L, n_chunks=tb // L, tb=tb, heads=heads),
        grid=(batch, H_C // heads, nt),
        in_specs=[col(CD_Q), col(CD_K), col(CD_V), col(CD_Z),
                  pl.BlockSpec((tb, LANES), lambda b, h, t: (b * nt + t, CD_GATES)),
                  pl.BlockSpec((3, CONV_C, width), lambda b, h, t: (0, 0, h)),
                  pl.BlockSpec((3, 1, halo, width), lambda b, h, t: (0, b, 0, h)),
                  pl.BlockSpec((2, LANES), lambda b, h, t: (0, 0)),
                  pl.BlockSpec((tb, tb), lambda b, h, t: (0, 0)),
                  pl.BlockSpec((1, DV_C), lambda b, h, t: (0, 0)),
                  st_spec],
        out_specs=[pl.BlockSpec((tb, width), lambda b, h, t: (b * nt + t, h)), st_spec],
        out_shape=[jax.ShapeDtypeStruct((batch * seq, H_C * DV_C), BF16),
                   jax.ShapeDtypeStruct((batch, H_C, DV_C, DK_C), F32)],
        scratch_shapes=[pltpu.VMEM((heads, DV_C, DK_C), F32),
                        pltpu.VMEM((3, SUBLANES + tb, width), F32),
                        pltpu.VMEM((3, tb, width), F32),
                        pltpu.VMEM((LANES, tb + (-tb % LANES)), F32)],
        compiler_params=_params("arbitrary", "arbitrary", "arbitrary"),
        name="gdn",
    )(z, z, z, z, z, cw, cs, gate_params, _chunk_tri(tb, L), norm_g.reshape(1, DV_C),
      jnp.swapaxes(s0, 2, 3))
    return o, jnp.swapaxes(st, 2, 3)


def _attn_kernel(q_ref, k_ref, v_ref, lp_ref, gn_ref, o_ref, ka_ref, vt_ref, m_ref, acc_ref,
                 *, seq, kv_len, tq, tk, lam_init, heads):
    h0 = pl.program_id(1) * heads
    qi = pl.program_id(2)
    past = kv_len - seq
    shift = int(math.log2(CHUNK))
    n_kb = kv_len // tk
    n_streams = 2 * heads

    @pl.when(qi == 0)
    def _():
        lane = _iota2((kv_len, 2 * D_HD), 1)
        k_idx = _iota2((kv_len, 1), 0)
        chunk_f = (k_idx >> shift).astype(F32)
        rem_f = (k_idx & (CHUNK - 1)).astype(F32)

        def pos_lanes(base):
            return jnp.where(lane == base, chunk_f, jnp.where(lane == base + 1, rem_f, 0.0))

        for g in range(heads):
            k = k_ref[:, g * DV_D:(g + 1) * DV_D]
            ka_ref[2 * g] = _bf(jnp.where(lane < D_HD, k, pos_lanes(D_HD)))
            ka_ref[2 * g + 1] = _bf(jnp.where(lane >= D_HD, k, pos_lanes(0)))
            vt = _transpose_rows(v_ref[:, g * DV_D:(g + 1) * DV_D])
            for jb in range(n_kb):
                vt_ref[g, jb, :DV_D, :] = _bf(vt[:, jb * tk:(jb + 1) * tk])
                vt_ref[g, jb, DV_D:, :] = jnp.ones((LANES, tk), BF16)

    lp = lp_ref[...]
    lam = (jnp.exp(jnp.sum(lp[0:1] * lp[1:2], axis=-1, keepdims=True))
           - jnp.exp(jnp.sum(lp[2:3] * lp[3:4], axis=-1, keepdims=True)) + lam_init)

    lane_q = _iota2((tq, 2 * D_HD), 1)
    qa = []
    slopes = []
    for g in range(heads):
        hf = jnp.full((1, 1), h0 + g + 1, jnp.int32).astype(F32)
        slope = jnp.exp(hf * (-8.0 / H_D * math.log(2.0)))
        slopes.append(slope)
        q = q_ref[:, g * DV_D:(g + 1) * DV_D] * (D_HD ** -0.5)

        def slope_lanes(base, slope=slope):
            return jnp.where(lane_q == base, slope * CHUNK,
                             jnp.where(lane_q == base + 1, slope, 0.0))

        qa.append(_bf(jnp.where(lane_q < D_HD, q, slope_lanes(D_HD))))
        qa.append(_bf(jnp.where(lane_q >= D_HD, q, slope_lanes(0))))
    q_start = past + qi * tq
    q_pos = q_start + _iota2((1, tq), 1)
    q_chunk = q_pos >> shift

    m_ref[...] = jnp.full(m_ref.shape, NEG_BIG, F32)
    acc_ref[...] = jnp.zeros(acc_ref.shape, F32)

    last_allowed = (((q_start + tq - 1) >> shift) << shift) + (CHUNK - 1)
    n_blocks = jnp.minimum(last_allowed, kv_len - 1) // tk + 1
    n_before = jnp.minimum((q_start + 1) // tk, n_blocks)

    def scores(j):
        rows = pl.ds(pl.multiple_of(j * tk, 2 * SUBLANES), tk)
        return [_mm_nt(ka_ref[s, rows, :], qa[s]) for s in range(n_streams)]

    def accumulate(j, s_t, mask):
        m_old = [m_ref[s] for s in range(n_streams)]
        m_new = [jnp.maximum(m_old[s], jnp.max(s_t[s], axis=0, keepdims=True))
                 for s in range(n_streams)]
        p_t = [jnp.exp(s_t[s] - m_new[s]) for s in range(n_streams)]
        if mask is not None:
            p_t = [jnp.where(mask, p, 0.0) for p in p_t]
        for s in range(n_streams):
            acc_ref[s] = (jnp.exp(m_old[s] - m_new[s]) * acc_ref[s]
                          + _mm(vt_ref[s // 2, j], _bf(p_t[s])))
            m_ref[s] = m_new[s]

    def block_before(j, carry):
        accumulate(j, scores(j), None)
        return carry

    def block_masked(j, carry):
        k_pos = j * tk + _iota2((tk, 1), 0)
        allowed = (k_pos >> shift) <= q_chunk
        late = jnp.maximum(k_pos - q_pos, 0).astype(F32)
        s_t = [jnp.where(allowed, s - (2.0 * slopes[idx // 2]) * late, NEG_BIG)
               for idx, s in enumerate(scores(j))]
        accumulate(j, s_t, allowed)
        return carry

    lax.fori_loop(0, n_before, block_before, 0)
    lax.fori_loop(n_before, n_blocks, block_masked, 0)
    pad = -tq % LANES
    gn = gn_ref[...]
    for g in range(heads):
        a0 = acc_ref[2 * g]
        a1 = acc_ref[2 * g + 1]
        o_t = (a0[:DV_D] / a0[DV_D:DV_D + 1] - lam * (a1[:DV_D] / a1[DV_D:DV_D + 1]))
        if pad:
            o_t = jnp.concatenate([o_t, jnp.zeros((DV_D, pad), F32)], axis=1)
        o = o_t.T[:tq]
        o_ref[:, g * DV_D:(g + 1) * DV_D] = (_rms(o, gn) * (1.0 - lam_init)).astype(BF16)


def diff_attention(zq, q_col, kd, k_col, vd, v_col, lam_params, norm_g, batch, seq, kv_len,
                   tq, tk, lam_init, heads):
    nq = seq // tq
    width = heads * DV_D
    return pl.pallas_call(
        functools.partial(_attn_kernel, seq=seq, kv_len=kv_len, tq=tq, tk=tk, lam_init=lam_init,
                          heads=heads),
        grid=(batch, H_D // heads, nq),
        in_specs=[pl.BlockSpec((tq, width), lambda b, h, i: (b * nq + i, q_col // heads + h)),
                  pl.BlockSpec((kv_len, width), lambda b, h, i: (b, k_col // heads + h)),
                  pl.BlockSpec((kv_len, width), lambda b, h, i: (b, v_col // heads + h)),
                  pl.BlockSpec((4, D_HD), lambda b, h, i: (0, 0)),
                  pl.BlockSpec((1, DV_D), lambda b, h, i: (0, 0))],
        out_specs=pl.BlockSpec((tq, width), lambda b, h, i: (b * nq + i, h)),
        out_shape=jax.ShapeDtypeStruct((batch * seq, H_D * DV_D), BF16),
        scratch_shapes=[pltpu.VMEM((2 * heads, kv_len, DV_D), BF16),
                        pltpu.VMEM((heads, kv_len // tk, DV_D + LANES, tk), BF16),
                        pltpu.VMEM((2 * heads, 1, tq), F32),
                        pltpu.VMEM((2 * heads, DV_D + LANES, tq), F32)],
        compiler_params=_params("arbitrary", "arbitrary", "arbitrary"),
        name="diff_attention",
    )(zq, kd, vd, lam_params, norm_g.reshape(1, DV_D))


def _pad_cols(w, total):
    return jnp.pad(w, ((0, 0), (0, total - w.shape[1])))


def _prep_w_in_ab(w):
    sizes = (H_A * DK_A, H_A * DK_A, H_A * DV_A, H_A * DV_A, H_B * DK_B, H_B * DK_B, H_B * DV_B,
             H_B, H_B, H_B * DV_B)
    aq, af, ai, ag, bq, bk, bv, big, bfg, bo = jnp.split(w, np_cumsum(sizes), axis=1)
    return _pad_cols(jnp.concatenate([aq, af, ai, ag, bq, bk, bv, bo, big, bfg], axis=1),
                     IN_AB_PAD).astype(BF16)


def _prep_w_in_cd(w):
    sizes = (H_C * (2 * DK_C + DV_C), H_C, H_C, H_C * DV_C, H_D * 2 * D_HD, H_D * 2 * D_HD,
             H_D * DV_D)
    cqkv, cb, ca, cz, dq, dk, dv = jnp.split(w, np_cumsum(sizes), axis=1)
    return _pad_cols(jnp.concatenate([cqkv, cz, dq, dk, dv, cb, ca], axis=1),
                     IN_CD_PAD).astype(BF16)


def np_cumsum(sizes):
    out, acc = [], 0
    for s in sizes[:-1]:
        acc += s
        out.append(acc)
    return out


def _tiles(n_rows, seq):
    tm = min(512, n_rows)
    tb = min(512, seq)
    return tm, tb


def _heads_per_step(n_heads, n_chunks, items=8):
    return max(1, min(n_heads, items // n_chunks))


def _layer_ab(x, st, p, lower_bounds, batch, seq, layer):
    n = batch * seq
    tm, tb = _tiles(n, seq)
    z = norm_matmul(x, p['norm_mix'], p['w_in'], tm, IN_AB_PAD // 3)
    n_chunks = tb // min(MIX_CHUNK, seq)
    o_a, s_new = hgrn2(z, lower_bounds, p['hgrn_norm'], st['hgrn'], batch, seq, tb, layer // 2,
                       _heads_per_step(H_A, n_chunks))
    o_b, c_new, n_new, m_new = mlstm(z, p['i_bias'], p['f_bias'], p['mlstm_norm'], st['mlstm_c'],
                                     st['mlstm_n'], st['mlstm_m'], batch, seq, tb,
                                     _heads_per_step(H_B, n_chunks))
    x = matmul_residual((o_a, o_b), p['w_out'], x, p['norm_ffn'], tm, False)
    return x, (s_new, c_new, n_new, m_new)


def _layer_cd(x, st, p, batch, seq, layer):
    n = batch * seq
    tm, tb = _tiles(n, seq)
    z = norm_matmul(x, p['norm_mix'], p['w_in'], tm, IN_CD_PAD // 3)
    heads = _heads_per_step(H_C, tb // min(CHUNK, seq), items=16)
    o_c, s_new = gdn(z, p['conv_w'], st['gdn_conv'], p['a_log'], p['dt_bias'], p['gdn_norm'],
                     st['gdn'], batch, seq, tb, heads)
    z3 = z.reshape(batch, seq, IN_CD_PAD)
    conv_new = z3[:, seq - (CONV_C - 1):, :H_C * (2 * DK_C + DV_C)]
    k_new = z3[:, :, CD_DK * LANES:CD_DV * LANES]
    v_new = z3[:, :, CD_DV * LANES:CD_GATES * LANES]
    past = st['k_cache'].shape[1]
    lam_init = 0.8 - 0.6 * math.exp(-0.3 * layer)
    if past == 0:
        o_d = diff_attention(z, CD_DQ, z, CD_DK, z, CD_DV, p['lam'], p['diff_norm'], batch, seq,
                             seq, min(512, seq), min(512, seq), lam_init, 2)
    else:
        kv_len = past + seq
        kd = jnp.concatenate([st['k_cache'].reshape(batch, past, -1), k_new], axis=1)
        vd = jnp.concatenate([st['v_cache'].reshape(batch, past, -1), v_new], axis=1)
        o_d = diff_attention(z, CD_DQ, kd.reshape(batch * kv_len, -1), 0,
                             vd.reshape(batch * kv_len, -1), 0, p['lam'], p['diff_norm'], batch,
                             seq, kv_len, seq, kv_len, lam_init, 4)
    x = matmul_residual((o_c, o_d), p['w_out'], x, p['norm_ffn'], tm, False)
    return x, (s_new, conv_new, k_new.reshape(batch, seq, H_D, 2 * D_HD),
               v_new.reshape(batch, seq, H_D, DV_D))


def _ffn(x, state, p, batch, seq, final_g):
    n = batch * seq
    tm, _ = _tiles(n, seq)
    act, conv_new = ffn_up(x, p['norm_ffn'], p['ffn_up'], p['ffn_conv_w'], p['ffn_conv_b'], state,
                           batch, seq, tm, 512)
    g = p['norm_ffn'] if final_g is None else final_g
    x = matmul_residual((act,), p['ffn_down'], x, g, tm, final_g is not None)
    return x, conv_new


def _run_group(x, st, params, lower_bounds, norm_final):
    batch, seq, _ = x.shape
    x = x.reshape(batch * seq, D_MODEL)
    new_states = []
    for layer, (p, s) in enumerate(zip(params, st)):
        if layer % 2 == 0:
            x, new_mix = _layer_ab(x, s, p, lower_bounds, batch, seq, layer)
        else:
            x, new_mix = _layer_cd(x, s, p, batch, seq, layer)
        last = layer == len(params) - 1
        x, new_ffn = _ffn(x, s['ffn_conv'], p, batch, seq, norm_final if last else None)
        new_states.append(new_mix + (new_ffn,))
    return x.reshape(batch, seq, D_MODEL), new_states


def kernel(x_prompt, x_sample, state_hgrn_0, state_mlstm_c_0, state_mlstm_n_0, state_mlstm_m_0,
           state_ffn_conv_0, state_gdn_1, state_gdn_conv_1, cache_k_1, cache_v_1, state_ffn_conv_1,
           hgrn_lower_bounds, norm_mix_0, w_in_0, mlstm_i_bias_0, mlstm_f_bias_0, hgrn_norm_0,
           mlstm_norm_0, w_out_0, norm_ffn_0, ffn_up_0, ffn_conv_w_0, ffn_conv_b_0, ffn_down_0,
           norm_mix_1, w_in_1, gdn_conv_w_1, gdn_a_log_1, gdn_dt_bias_1, gdn_norm_1, diff_lambda_1,
           diff_norm_1, w_out_1, norm_ffn_1, ffn_up_1, ffn_conv_w_1, ffn_conv_b_1, ffn_down_1,
           norm_final):
    params = [
        dict(norm_mix=norm_mix_0, w_in=_prep_w_in_ab(w_in_0), i_bias=mlstm_i_bias_0,
             f_bias=mlstm_f_bias_0, hgrn_norm=hgrn_norm_0, mlstm_norm=mlstm_norm_0,
             w_out=w_out_0.astype(BF16), norm_ffn=norm_ffn_0, ffn_up=ffn_up_0.astype(BF16),
             ffn_conv_w=ffn_conv_w_0, ffn_conv_b=ffn_conv_b_0, ffn_down=ffn_down_0.astype(BF16)),
        dict(norm_mix=norm_mix_1, w_in=_prep_w_in_cd(w_in_1), conv_w=gdn_conv_w_1,
             a_log=gdn_a_log_1, dt_bias=gdn_dt_bias_1, gdn_norm=gdn_norm_1, lam=diff_lambda_1,
             diff_norm=diff_norm_1, w_out=w_out_1.astype(BF16), norm_ffn=norm_ffn_1,
             ffn_up=ffn_up_1.astype(BF16), ffn_conv_w=ffn_conv_w_1, ffn_conv_b=ffn_conv_b_1,
             ffn_down=ffn_down_1.astype(BF16)),
    ]
    bp = x_prompt.shape[0]
    st_prompt = [
        dict(hgrn=jnp.zeros((bp, H_A, DK_A, DV_A), F32), mlstm_c=jnp.zeros((bp, H_B, DK_B, DV_B), F32),
             mlstm_n=jnp.zeros((bp, H_B, DK_B), F32), mlstm_m=jnp.full((bp, H_B), NEG_BIG, F32),
             ffn_conv=jnp.zeros((bp, CONV_FFN - 1, D_FF), F32)),
        dict(gdn=jnp.zeros((bp, H_C, DK_C, DV_C), F32),
             gdn_conv=jnp.zeros((bp, CONV_C - 1, H_C * (2 * DK_C + DV_C)), F32),
             k_cache=jnp.zeros((bp, 0, H_D, 2 * D_HD), F32), v_cache=jnp.zeros((bp, 0, H_D, DV_D), F32),
             ffn_conv=jnp.zeros((bp, CONV_FFN - 1, D_FF), F32)),
    ]
    st_sample = [
        dict(hgrn=state_hgrn_0, mlstm_c=state_mlstm_c_0, mlstm_n=state_mlstm_n_0,
             mlstm_m=state_mlstm_m_0, ffn_conv=state_ffn_conv_0),
        dict(gdn=state_gdn_1, gdn_conv=state_gdn_conv_1, k_cache=cache_k_1, v_cache=cache_v_1,
             ffn_conv=state_ffn_conv_1),
    ]
    y_p, new_p = _run_group(x_prompt, st_prompt, params, hgrn_lower_bounds, norm_final)
    y_s, new_s = _run_group(x_sample, st_sample, params, hgrn_lower_bounds, norm_final)
    (hgrn_p, c_p, n_p, m_p, f0_p), (gdn_p, gc_p, k_p, v_p, f1_p) = new_p
    (hgrn_s, c_s, n_s, m_s, f0_s), (gdn_s, gc_s, k_s, v_s, f1_s) = new_s
    return (y_p, y_s, hgrn_p, hgrn_s, c_p, c_s, n_p, n_s, m_p, m_s, f0_p, f0_s, gdn_p, gdn_s,
            gc_p, gc_s, k_p, k_s, v_p, v_s, f1_p, f1_s)
```

```python
import functools
import math

import jax
import jax.numpy as jnp
from jax import lax
from jax.experimental import pallas as pl
from jax.experimental.pallas import tpu as pltpu

F32 = jnp.float32
BF16 = jnp.bfloat16
HI = lax.Precision.HIGHEST

D_MODEL = 2048
CHUNK = 64
H_A, DK_A, DV_A = 8, 128, 128
H_B, DK_B, DV_B = 4, 128, 256
H_C, DK_C, DV_C = 8, 128, 128
CONV_C = 4
H_D, D_HD = 8, 64
DV_D = 2 * D_HD
D_FF = 5632
CONV_FFN = 3
EPS = 1e-6
NEG_BIG = -1e30

LANES = 128
SUBLANES = 8
ONES_ROWS = 2 * SUBLANES
VMEM_LIMIT = 56 * 1024 * 1024
MIX_CHUNK = 128

AB_Q, AB_F, AB_I, AB_G = 0, 8, 16, 24
AB_BQ, AB_BK = 32, 36
AB_BV, AB_BO = 20, 24
AB_GATES = 56
IN_AB_PAD = 57 * LANES
CD_Q, CD_K, CD_V, CD_Z = 0, 8, 16, 24
CD_DQ, CD_DK, CD_DV = 32, 40, 48
CD_GATES = 56
IN_CD_PAD = 57 * LANES


def _mm(a, b, prec=None):
    return lax.dot_general(a, b, (((1,), (0,)), ((), ())), precision=prec,
                           preferred_element_type=F32)


def _mm_nt(a, b, prec=None):
    return lax.dot_general(a, b, (((1,), (1,)), ((), ())), precision=prec,
                           preferred_element_type=F32)


def _mm_tn(a, b, prec=None):
    return lax.dot_general(a, b, (((0,), (0,)), ((), ())), precision=prec,
                           preferred_element_type=F32)


def _sigmoid(x):
    return 1.0 / (1.0 + jnp.exp(-x))


def _silu(x):
    return x * _sigmoid(x)


def _softplus(x):
    return jnp.maximum(x, 0.0) + jnp.log1p(jnp.exp(-jnp.abs(x)))


def _rms(x, g):
    return x * lax.rsqrt(jnp.mean(x * x, axis=-1, keepdims=True) + EPS) * g


def _params(*sem):
    return pltpu.CompilerParams(dimension_semantics=sem, vmem_limit_bytes=VMEM_LIMIT)


def _norm_matmul_kernel(x_ref, g_ref, w_ref, o_ref, h_ref):
    @pl.when(pl.program_id(1) == 0)
    def _():
        h_ref[...] = _rms(x_ref[...], g_ref[...]).astype(BF16)

    o_ref[...] = _mm(h_ref[...], w_ref[...])


def norm_matmul(x, g, w, tm, tn):
    n, k = x.shape
    m = w.shape[1]
    return pl.pallas_call(
        _norm_matmul_kernel,
        grid=(n // tm, m // tn),
        in_specs=[pl.BlockSpec((tm, k), lambda i, j: (i, 0)),
                  pl.BlockSpec((1, k), lambda i, j: (0, 0)),
                  pl.BlockSpec((k, tn), lambda i, j: (0, j))],
        out_specs=pl.BlockSpec((tm, tn), lambda i, j: (i, j)),
        out_shape=jax.ShapeDtypeStruct((n, m), F32),
        scratch_shapes=[pltpu.VMEM((tm, k), BF16)],
        compiler_params=_params("arbitrary", "arbitrary"),
        name="norm_matmul",
    )(x, g.reshape(1, k), w)


def _matmul_res_kernel(*refs, n_parts, final_norm):
    y_refs = refs[:n_parts]
    w_ref, x_ref, g_ref, o_ref = refs[n_parts:]
    k_part = y_refs[0].shape[1]
    acc = x_ref[...]
    for part, y_ref in enumerate(y_refs):
        acc = acc + _mm(y_ref[...], w_ref[part * k_part:(part + 1) * k_part, :])
    o_ref[...] = _rms(acc, g_ref[...]) if final_norm else acc


def matmul_residual(ys, w, x, g, tm, final_norm):
    n, k_part = ys[0].shape
    k, d = w.shape
    n_parts = len(ys)
    return pl.pallas_call(
        functools.partial(_matmul_res_kernel, n_parts=n_parts, final_norm=final_norm),
        grid=(n // tm,),
        in_specs=[pl.BlockSpec((tm, k_part), lambda i: (i, 0)) for _ in range(n_parts)]
        + [pl.BlockSpec((k, d), lambda i: (0, 0), pipeline_mode=pl.Buffered(1)),
           pl.BlockSpec((tm, d), lambda i: (i, 0)),
           pl.BlockSpec((1, d), lambda i: (0, 0))],
        out_specs=pl.BlockSpec((tm, d), lambda i: (i, 0)),
        out_shape=jax.ShapeDtypeStruct((n, d), F32),
        compiler_params=_params("arbitrary"),
        name="matmul_residual",
    )(*ys, w, x, g.reshape(1, d))


def _ffn_up_kernel(x_ref, g_ref, wa_ref, wu_ref, cw_ref, cb_ref, st_ref, act_ref, tail_ref,
                   h_ref, ext_ref, carry_ref, *, nseq, rows, tiles_per_seq):
    i = pl.program_id(0)
    j = pl.program_id(1)
    halo = CONV_FFN - 1

    @pl.when(j == 0)
    def _():
        h_ref[...] = _rms(x_ref[...], g_ref[...]).astype(BF16)

    if tiles_per_seq > 1:
        @pl.when(i == 0)
        def _():
            carry_ref[j] = jnp.zeros(carry_ref.shape[1:], F32)

    h = h_ref[...]
    a = _mm(h, wa_ref[...])
    u = _mm(h, wu_ref[...])
    cw = cw_ref[...]
    cb = cb_ref[...]
    for s in range(nseq):
        lo = s * rows
        ext_ref[s, SUBLANES:, :] = a[lo:lo + rows]
        prev = st_ref[s]
        if tiles_per_seq > 1:
            prev = jnp.where((i % tiles_per_seq) == 0, prev, carry_ref[j, SUBLANES - halo:, :])
        ext_ref[s, SUBLANES - halo:SUBLANES, :] = prev
        conv = cb
        for t in range(CONV_FFN):
            off = SUBLANES - halo + t
            conv = conv + ext_ref[s, off:off + rows, :] * cw[t:t + 1]
        act_ref[lo:lo + rows, :] = (_silu(conv) * u[lo:lo + rows]).astype(BF16)
        tail_ref[s] = ext_ref[s, SUBLANES + rows - halo:SUBLANES + rows, :]
        carry_ref[j] = ext_ref[s, rows:rows + SUBLANES, :]


def ffn_up(x, g, w_up, conv_w, conv_b, state, batch, seq, tm, tf):
    n, k = x.shape
    nff = D_FF // tf
    if tm >= seq:
        nseq, rows, tiles_per_seq = tm // seq, seq, 1
    else:
        nseq, rows, tiles_per_seq = 1, tm, seq // tm
    ntiles = n // tm
    halo = CONV_FFN - 1
    act, tail = pl.pallas_call(
        functools.partial(_ffn_up_kernel, nseq=nseq, rows=rows, tiles_per_seq=tiles_per_seq),
        grid=(ntiles, nff),
        in_specs=[pl.BlockSpec((tm, k), lambda i, j: (i, 0)),
                  pl.BlockSpec((1, k), lambda i, j: (0, 0)),
                  pl.BlockSpec((k, tf), lambda i, j: (0, j)),
                  pl.BlockSpec((k, tf), lambda i, j: (0, j + nff)),
                  pl.BlockSpec((CONV_FFN, tf), lambda i, j: (0, j)),
                  pl.BlockSpec((1, tf), lambda i, j: (0, j)),
                  pl.BlockSpec((nseq, halo, tf), lambda i, j: (i // tiles_per_seq, 0, j))],
        out_specs=[pl.BlockSpec((tm, tf), lambda i, j: (i, j)),
                   pl.BlockSpec((nseq, halo, tf), lambda i, j: (i, 0, j))],
        out_shape=[jax.ShapeDtypeStruct((n, D_FF), BF16),
                   jax.ShapeDtypeStruct((ntiles * nseq, halo, D_FF), F32)],
        scratch_shapes=[pltpu.VMEM((tm, k), BF16),
                        pltpu.VMEM((nseq, SUBLANES + rows, tf), F32),
                        pltpu.VMEM((nff, SUBLANES, tf), F32)],
        compiler_params=_params("arbitrary", "arbitrary"),
        name="ffn_up",
    )(x, g.reshape(1, k), w_up, w_up, conv_w, conv_b.reshape(1, D_FF), state)
    new_state = tail.reshape(batch, -1, halo, D_FF)[:, -1]
    return act, new_state


def _iota2(shape, dim):
    return lax.broadcasted_iota(jnp.int32, shape, dim)


def _tril(n):
    return (_iota2((n, n), 1) <= _iota2((n, n), 0)).astype(F32)


def _lane_col(x, lane):
    return jnp.sum(jnp.where(_iota2(x.shape, 1) == lane, x, 0.0), axis=-1, keepdims=True)


def _bf(x):
    return x.astype(BF16)


def _split3(x):
    h1 = _bf(x)
    r1 = x - h1.astype(F32)
    h2 = _bf(r1)
    h3 = _bf(r1 - h2.astype(F32))
    return h1, h2, h3


def _mm_sel(sel, x):
    n = x.shape[1]
    y = _mm(_bf(sel), jnp.concatenate(_split3(x), axis=1))
    return y[:, :n] + y[:, n:2 * n] + y[:, 2 * n:]


def _transpose_rows(x):
    r = x.shape[0]
    pad = -r % LANES
    if pad:
        x = jnp.concatenate([x, jnp.zeros((pad, x.shape[1]), x.dtype)], axis=0)
    return x.T


def _chunk_cumsum_rows(row, tri):
    r = row.shape[1]
    h1, h2, h3 = (p.astype(F32) for p in _split3(row))
    sub = _iota2((2 * SUBLANES, r), 0)
    parts = jnp.where(sub == 0, h1, jnp.where(sub == 1, h2, jnp.where(sub == 2, h3, 0.0)))
    y = _mm(_bf(parts), tri)
    return y[0:1] + y[1:2] + y[2:3]


def _row_to_col(row):
    n = row.shape[1]
    eye = _iota2((n, n), 0) == _iota2((n, n), 1)
    return jnp.sum(jnp.where(eye, row, 0.0), axis=-1, keepdims=True)


def _chunk_tri(tb, chunk):
    s = lax.broadcasted_iota(jnp.int32, (tb, tb), 0)
    t = lax.broadcasted_iota(jnp.int32, (tb, tb), 1)
    return ((s <= t) & (s // chunk == t // chunk)).astype(BF16)


def _hgrn_kernel(q_ref, f_ref, i_ref, g_ref, lb_ref, gn_ref, s0_ref, o_ref, s_out_ref, s_ref,
                 *, L, n_chunks, lb_index, heads):
    tb = pl.program_id(2)

    @pl.when(tb == 0)
    def _():
        s_ref[...] = s0_ref[0]

    lbp = lb_ref[...]
    e = jnp.exp(lbp - jnp.max(lbp, axis=0, keepdims=True))
    lb_all = jnp.sum(e[:lb_index + 1], axis=0, keepdims=True) / jnp.sum(e, axis=0, keepdims=True)

    sizes = [L >> l for l in range(int(math.log2(L)))]
    ti = _iota2((L, L), 0)
    si = _iota2((L, L), 1)
    tcol = _iota2((L, 1), 0)
    mats = [si <= ti]
    for sz in sizes:
        mats.append(si <= ((ti & ~(sz - 1)) + (sz // 2 - 1)))
    cum_mat = jnp.concatenate([m.astype(F32) for m in mats], axis=0)
    eye = ti == si
    gn = gn_ref[...]

    chunks = []
    for c in range(n_chunks):
        rows = pl.ds(c * L, L)
        for g in range(heads):
            lanes = slice(g * LANES, (g + 1) * LANES)
            lb = lb_all[:, lanes]
            q = q_ref[rows, lanes]
            f = lb + (1.0 - lb) * _sigmoid(f_ref[rows, lanes])
            k = 1.0 - f
            cums = _mm_sel(cum_mat, jnp.log(f))
            chunks.append(dict(
                rows=rows, lanes=lanes, g=g, q=q, k=k, cums=cums, b=cums[:L],
                vb=_bf(i_ref[rows, lanes]),
                scores=jnp.where(eye, jnp.sum(q * k, axis=-1, keepdims=True), 0.0)))
    for l, sz in enumerate(sizes):
        right = (tcol & (sz - 1)) >= sz // 2
        same = (ti & ~(sz - 1)) == (si & ~(sz - 1))
        for ch in chunks:
            w = jnp.exp(-jnp.abs(ch['b'] - ch['cums'][(l + 1) * L:(l + 2) * L]))
            qm = jnp.where(right, ch['q'] * w, 0.0)
            km = jnp.where(right, 0.0, ch['k'] * w)
            ch['scores'] = ch['scores'] + jnp.where(same, _mm_nt(_bf(qm), _bf(km)), 0.0)
    for ch in chunks:
        bl = ch['b'][L - 1:L]
        ch['update'] = _mm_tn(_bf(ch['k'] * jnp.exp(bl - ch['b'])), ch['vb'])
        ch['decay'] = _row_to_col(jnp.exp(bl))
    for ch in chunks:
        s = s_ref[ch['g']]
        o = _mm(jnp.concatenate([_bf(ch['q'] * jnp.exp(ch['b'])), _bf(ch['scores'])], axis=1),
                jnp.concatenate([_bf(s), ch['vb']], axis=0))
        s_ref[ch['g']] = ch['decay'] * s + ch['update']
        o_ref[ch['rows'], ch['lanes']] = (
            _rms(o, gn) * _silu(g_ref[ch['rows'], ch['lanes']])).astype(BF16)

    @pl.when(tb == pl.num_programs(2) - 1)
    def _():
        s_out_ref[0] = s_ref[...]


def hgrn2(z, lower_bounds, norm_g, s0, batch, seq, tb, lb_index, heads):
    L = min(MIX_CHUNK, seq)
    nt = seq // tb
    width = heads * LANES

    def col(c0):
        return pl.BlockSpec((tb, width), lambda b, h, t: (b * nt + t, c0 // heads + h))

    st_spec = pl.BlockSpec((1, heads, DK_A, DV_A), lambda b, h, t: (b, h, 0, 0))
    return pl.pallas_call(
        functools.partial(_hgrn_kernel, L=L, n_chunks=tb // L, lb_index=lb_index, heads=heads),
        grid=(batch, H_A // heads, nt),
        in_specs=[col(AB_Q), col(AB_F), col(AB_I), col(AB_G),
                  pl.BlockSpec((lower_bounds.shape[0], width), lambda b, h, t: (0, h)),
                  pl.BlockSpec((1, DV_A), lambda b, h, t: (0, 0)),
                  st_spec],
        out_specs=[pl.BlockSpec((tb, width), lambda b, h, t: (b * nt + t, h)), st_spec],
        out_shape=[jax.ShapeDtypeStruct((batch * seq, H_A * DV_A), BF16),
                   jax.ShapeDtypeStruct((batch, H_A, DK_A, DV_A), F32)],
        scratch_shapes=[pltpu.VMEM((heads, DK_A, DV_A), F32)],
        compiler_params=_params("arbitrary", "arbitrary", "arbitrary"),
        name="hgrn2",
    )(z, z, z, z, lower_bounds, norm_g.reshape(1, DV_A), s0)


def _mlstm_kernel(q_ref, k_ref, v_ref, og_ref, gt_ref, bias_ref, tri_ref, gn_ref, c0_ref, n0_ref,
                  m0_ref, o_ref, c_out_ref, n_out_ref, m_out_ref, c_ref, n_ref, m_ref, gtt_ref,
                  *, L, n_chunks, heads):
    h0 = pl.program_id(1) * heads
    tb = pl.program_id(2)

    @pl.when(tb == 0)
    def _():
        c_ref[...] = c0_ref[0]
        n_ref[...] = n0_ref[0]
        m_ref[...] = m0_ref[0]

    n_rows = L * n_chunks
    gtt_ref[...] = _transpose_rows(gt_ref[...] + bias_ref[...])
    ig_rows, b_rows = [], []
    for g in range(heads):
        ig_rows.append(gtt_ref[pl.ds(h0 + g, 1), :][:, :n_rows])
        f_rows = gtt_ref[pl.ds(H_B + h0 + g, 1), :][:, :n_rows]
        logf = jnp.minimum(f_rows, 0.0) - jnp.log1p(jnp.exp(-jnp.abs(f_rows)))
        b_rows.append(_chunk_cumsum_rows(logf, tri_ref[...]))

    causal = _iota2((L, L), 1) <= _iota2((L, L), 0)
    gn = gn_ref[...]
    scale = DK_B ** -0.5

    chunks = []
    for c in range(n_chunks):
        rows = pl.ds(c * L, L)
        for g in range(heads):
            q = q_ref[rows, g * DK_B:(g + 1) * DK_B] * scale
            k = k_ref[rows, g * DK_B:(g + 1) * DK_B]
            vb = _bf(v_ref[rows, g * DV_B:(g + 1) * DV_B])
            ig_row = ig_rows[g][:, c * L:(c + 1) * L]
            b_row = b_rows[g][:, c * L:(c + 1) * L]
            b_col = _row_to_col(b_row)
            d = b_col - b_row + ig_row
            d_max = jnp.max(jnp.where(causal, d, -jnp.inf), axis=-1, keepdims=True)
            p = jnp.exp(jnp.where(causal, d - d_max, -jnp.inf)) * _mm_nt(_bf(q), _bf(k))
            b_last = b_row[:, L - 1:L]
            wk_max = jnp.max(b_last - b_row + ig_row, axis=-1, keepdims=True)
            ks = jnp.exp(b_last - b_col + _row_to_col(ig_row) - wk_max) * k
            chunks.append(dict(
                rows=rows, g=g, q=q, b_col=b_col, d_max=d_max, b_last=b_last, wk_max=wk_max,
                p_sum=jnp.sum(p, axis=-1, keepdims=True), pv=_mm(_bf(p), vb),
                k_sum=jnp.sum(ks, axis=0, keepdims=True), kv=_mm_tn(_bf(ks), vb)))

    for ch in chunks:
        g = ch['g']
        q = ch['q']
        cs = c_ref[g]
        n = n_ref[g]
        m_prev = m_ref[g][:, 0:1]
        inter = ch['b_col'] + m_prev
        m_t = jnp.maximum(ch['d_max'], inter)
        w_intra = jnp.exp(ch['d_max'] - m_t)
        w_inter = jnp.exp(inter - m_t)
        num = w_intra * ch['pv'] + w_inter * _mm(_bf(q), _bf(cs))
        den = w_intra * ch['p_sum'] + w_inter * jnp.sum(q * n, axis=-1, keepdims=True)
        hid = num / jnp.maximum(jnp.abs(den), jnp.exp(-m_t))
        m_new = jnp.maximum(ch['b_last'] + m_prev, ch['wk_max'])
        w_new = jnp.exp(ch['wk_max'] - m_new)
        carry_w = jnp.exp(ch['b_last'] + m_prev - m_new)
        c_ref[g] = carry_w * cs + w_new * ch['kv']
        n_ref[g] = carry_w * n + w_new * ch['k_sum']
        m_ref[g] = jnp.broadcast_to(m_new, (1, LANES))
        lanes = slice(g * DV_B, (g + 1) * DV_B)
        o_ref[ch['rows'], lanes] = (
            _rms(hid, gn) * _sigmoid(og_ref[ch['rows'], lanes])).astype(BF16)

    @pl.when(tb == pl.num_programs(2) - 1)
    def _():
        c_out_ref[0] = c_ref[...]
        n_out_ref[0] = n_ref[...]
        m_out_ref[0] = m_ref[...]


def mlstm(z, i_bias, f_bias, norm_g, c0, n0, m0, batch, seq, tb, heads):
    L = min(MIX_CHUNK, seq)
    nt = seq // tb
    bias = jnp.zeros((1, LANES), F32).at[0, :H_B].set(i_bias).at[0, H_B:2 * H_B].set(f_bias)

    def col(c0_, width):
        return pl.BlockSpec((tb, heads * width), lambda b, h, t: (b * nt + t, c0_ // heads + h))

    def state(r, c):
        return pl.BlockSpec((1, heads, r, c), lambda b, h, t: (b, h, 0, 0))

    o, cs, n, m = pl.pallas_call(
        functools.partial(_mlstm_kernel, L=L, n_chunks=tb // L, heads=heads),
        grid=(batch, H_B // heads, nt),
        in_specs=[col(AB_BQ, DK_B), col(AB_BK, DK_B), col(AB_BV, DV_B), col(AB_BO, DV_B),
                  pl.BlockSpec((tb, LANES), lambda b, h, t: (b * nt + t, AB_GATES)),
                  pl.BlockSpec((1, LANES), lambda b, h, t: (0, 0)),
                  pl.BlockSpec((tb, tb), lambda b, h, t: (0, 0)),
                  pl.BlockSpec((1, DV_B), lambda b, h, t: (0, 0)),
                  state(DK_B, DV_B), state(1, DK_B), state(1, LANES)],
        out_specs=[pl.BlockSpec((tb, heads * DV_B), lambda b, h, t: (b * nt + t, h)),
                   state(DK_B, DV_B), state(1, DK_B), state(1, LANES)],
        out_shape=[jax.ShapeDtypeStruct((batch * seq, H_B * DV_B), BF16),
                   jax.ShapeDtypeStruct((batch, H_B, DK_B, DV_B), F32),
                   jax.ShapeDtypeStruct((batch, H_B, 1, DK_B), F32),
                   jax.ShapeDtypeStruct((batch, H_B, 1, LANES), F32)],
        scratch_shapes=[pltpu.VMEM((heads, DK_B, DV_B), F32), pltpu.VMEM((heads, 1, DK_B), F32),
                        pltpu.VMEM((heads, 1, LANES), F32),
                        pltpu.VMEM((LANES, tb + (-tb % LANES)), F32)],
        compiler_params=_params("arbitrary", "arbitrary", "arbitrary"),
        name="mlstm",
    )(z, z, z, z, z, bias, _chunk_tri(tb, L), norm_g.reshape(1, DV_B), c0,
      n0.reshape(batch, H_B, 1, DK_B),
      jnp.broadcast_to(m0.reshape(batch, H_B, 1, 1), (batch, H_B, 1, LANES)))
    return o, cs, n.reshape(batch, H_B, DK_B), m[:, :, 0, 0]


def _gdn_kernel(q_ref, k_ref, v_ref, z_ref, gt_ref, cw_ref, cs_ref, gp_ref, tri_ref, gn_ref, s0_ref,
                o_ref, s_out_ref, st_ref, ext_ref, cv_ref, lgt_ref, *, L, n_chunks, tb, heads):
    h0 = pl.program_id(1) * heads
    t = pl.program_id(2)
    halo = CONV_C - 1

    @pl.when(t == 0)
    def _():
        st_ref[...] = s0_ref[0]
        for p in range(3):
            ext_ref[p, SUBLANES - halo:SUBLANES, :] = cs_ref[p, 0]

    srcs = (q_ref, k_ref, v_ref)
    for p in range(3):
        ext_ref[p, SUBLANES:, :] = srcs[p][...]
        cw = cw_ref[p]
        acc = None
        for j in range(CONV_C):
            off = SUBLANES - halo + j
            term = ext_ref[p, off:off + tb, :] * cw[j:j + 1]
            acc = term if acc is None else acc + term
        cv_ref[p] = _silu(acc)
    ti = _iota2((L, L), 0)
    si = _iota2((L, L), 1)
    gp = gp_ref[...]
    gn = gn_ref[...]

    lgt_ref[...] = _transpose_rows(-jnp.exp(gp[0:1]) * _softplus(gt_ref[...] + gp[1:2]))
    b_rows = [_chunk_cumsum_rows(lgt_ref[pl.ds(H_C + h0 + g, 1), :][:, :tb], tri_ref[...])
              for g in range(heads)]

    def l2n(x):
        return x * lax.rsqrt(jnp.sum(x * x, axis=-1, keepdims=True) + EPS)

    chunks = []
    for c in range(n_chunks):
        rows = pl.ds(c * L, L)
        sig = _sigmoid(gt_ref[rows, :])
        for g in range(heads):
            lanes = slice(g * LANES, (g + 1) * LANES)
            q = l2n(cv_ref[0, rows, lanes]) * (DK_C ** -0.5)
            k = l2n(cv_ref[1, rows, lanes])
            v = cv_ref[2, rows, lanes]
            beta = _lane_col(sig, h0 + g)
            b_row = b_rows[g][:, c * L:(c + 1) * L]
            b_col = _row_to_col(b_row)
            dec_incl = jnp.exp(jnp.where(si <= ti, b_col - b_row, -jnp.inf))
            eb = jnp.exp(b_col)
            kb = _bf(k)
            qb = _bf(q)
            kq = _mm_nt(jnp.concatenate([kb, qb], axis=0), kb)
            b_last = b_row[:, L - 1:L]
            chunks.append(dict(
                rows=rows, lanes=lanes, g=g, qb=qb, eb=eb, decay=jnp.exp(b_last),
                pw=-(beta * kq[:L] * jnp.where(si < ti, dec_incl, 0.0)),
                attn=_bf(kq[L:] * dec_incl),
                x=jnp.concatenate([beta * v, (beta * eb) * k], axis=1),
                k_dec=_bf(k * jnp.exp(b_last - b_col))))

    eye = jnp.where(ti == si, 1.0, 0.0)
    n_steps = int(math.log2(L))
    for ch in chunks:
        ch['inv'] = eye + ch['pw']
        pwb = _bf(ch['pw'])
        ch['pw'] = _mm(pwb, pwb)
    for step in range(1, n_steps):
        for ch in chunks:
            inv, pw = ch['inv'], ch['pw']
            if step == n_steps - 1:
                ch['inv'] = inv + _mm(_bf(pw), _bf(inv))
            else:
                y = _mm(_bf(pw), jnp.concatenate([_bf(inv), _bf(pw)], axis=1))
                ch['inv'] = inv + y[:, :L]
                ch['pw'] = y[:, L:]
    for ch in chunks:
        ch['x'] = _mm(_bf(ch['inv']), _bf(ch['x']))

    for ch in chunks:
        w = ch['x'][:, :DV_C]
        u = ch['x'][:, DV_C:]
        st = st_ref[ch['g']]
        uq = _mm_nt(jnp.concatenate([_bf(u), ch['qb']], axis=0), _bf(st))
        db = _bf(w - uq[:L])
        o = ch['eb'] * uq[L:] + _mm(ch['attn'], db)
        st_ref[ch['g']] = ch['decay'] * st + _mm_tn(db, ch['k_dec'])
        o_ref[ch['rows'], ch['lanes']] = (
            _rms(o, gn) * _silu(z_ref[ch['rows'], ch['lanes']])).astype(BF16)

    for p in range(3):
        ext_ref[p, 0:SUBLANES, :] = ext_ref[p, tb:tb + SUBLANES, :]

    @pl.when(t == pl.num_programs(2) - 1)
    def _():
        s_out_ref[0] = st_ref[...]


def gdn(z, conv_w, conv_state, a_log, dt_bias, norm_g, s0, batch, seq, tb, heads):
    L = min(CHUNK, seq)
    nt = seq // tb
    halo = CONV_C - 1
    gate_params = (jnp.zeros((2, LANES), F32).at[0, H_C:2 * H_C].set(a_log)
                   .at[1, H_C:2 * H_C].set(dt_bias))
    cw = conv_w.reshape(CONV_C, 3, H_C * DK_C).transpose(1, 0, 2)
    cs = conv_state.reshape(batch, halo, 3, H_C * DK_C).transpose(2, 0, 1, 3)

    width = heads * LANES

    def col(c0):
        return pl.BlockSpec((tb, width), lambda b, h, t: (b * nt + t, c0 // heads + h))

    st_spec = pl.BlockSpec((1, heads, DV_C, DK_C), lambda b, h, t: (b, h, 0, 0))
    o, st = pl.pallas_call(
        functools.partial(_gdn_kernel, L=L, n_chunks=tb // L, tb=tb, heads=heads),
        grid=(batch, H_C // heads, nt),
        in_specs=[col(CD_Q), col(CD_K), col(CD_V), col(CD_Z),
                  pl.BlockSpec((tb, LANES), lambda b, h, t: (b * nt + t, CD_GATES)),
                  pl.BlockSpec((3, CONV_C, width), lambda b, h, t: (0, 0, h)),
                  pl.BlockSpec((3, 1, halo, width), lambda b, h, t: (0, b, 0, h)),
                  pl.BlockSpec((2, LANES), lambda b, h, t: (0, 0)),
                  pl.BlockSpec((tb, tb), lambda b, h, t: (0, 0)),
                  pl.BlockSpec((1, DV_C), lambda b, h, t: (0, 0)),
                  st_spec],
        out_specs=[pl.BlockSpec((tb, width), lambda b, h, t: (b * nt + t, h)), st_spec],
        out_shape=[jax.ShapeDtypeStruct((batch * seq, H_C * DV_C), BF16),
                   jax.ShapeDtypeStruct((batch, H_C, DV_C, DK_C), F32)],
        scratch_shapes=[pltpu.VMEM((heads, DV_C, DK_C), F32),
                        pltpu.VMEM((3, SUBLANES + tb, width), F32),
                        pltpu.VMEM((3, tb, width), F32),
                        pltpu.VMEM((LANES, tb + (-tb % LANES)), F32)],
        compiler_params=_params("arbitrary", "arbitrary", "arbitrary"),
        name="gdn",
    )(z, z, z, z, z, cw, cs, gate_params, _chunk_tri(tb, L), norm_g.reshape(1, DV_C),
      jnp.swapaxes(s0, 2, 3))
    return o, jnp.swapaxes(st, 2, 3)


def _attn_kernel(q_ref, k_ref, v_ref, lp_ref, gn_ref, o_ref, ka_ref, vt_ref, m_ref, acc_ref,
                 *, seq, kv_len, tq, tk, lam_init, heads):
    h0 = pl.program_id(1) * heads
    qi = pl.program_id(2)
    past = kv_len - seq
    shift = int(math.log2(CHUNK))
    n_kb = kv_len // tk
    n_streams = 2 * heads

    @pl.when(qi == 0)
    def _():
        lane = _iota2((kv_len, 2 * D_HD), 1)
        k_idx = _iota2((kv_len, 1), 0)
        chunk_f = (k_idx >> shift).astype(F32)
        rem_f = (k_idx & (CHUNK - 1)).astype(F32)

        def pos_lanes(base):
            return jnp.where(lane == base, chunk_f, jnp.where(lane == base + 1, rem_f, 0.0))

        for g in range(heads):
            k = k_ref[:, g * DV_D:(g + 1) * DV_D]
            ka_ref[2 * g] = _bf(jnp.where(lane < D_HD, k, pos_lanes(D_HD)))
            ka_ref[2 * g + 1] = _bf(jnp.where(lane >= D_HD, k, pos_lanes(0)))
            vt = _transpose_rows(v_ref[:, g * DV_D:(g + 1) * DV_D])
            for jb in range(n_kb):
                vt_ref[g, jb, :DV_D, :] = _bf(vt[:, jb * tk:(jb + 1) * tk])
                vt_ref[g, jb, DV_D:, :] = jnp.ones((ONES_ROWS, tk), BF16)

    lp = lp_ref[...]
    lam = (jnp.exp(jnp.sum(lp[0:1] * lp[1:2], axis=-1, keepdims=True))
           - jnp.exp(jnp.sum(lp[2:3] * lp[3:4], axis=-1, keepdims=True)) + lam_init)

    lane_q = _iota2((tq, 2 * D_HD), 1)
    qa = []
    slopes = []
    for g in range(heads):
        hf = jnp.full((1, 1), h0 + g + 1, jnp.int32).astype(F32)
        slope = jnp.exp(hf * (-8.0 / H_D * math.log(2.0)))
        slopes.append(slope)
        q = q_ref[:, g * DV_D:(g + 1) * DV_D] * (D_HD ** -0.5)

        def slope_lanes(base, slope=slope):
            return jnp.where(lane_q == base, slope * CHUNK,
                             jnp.where(lane_q == base + 1, slope, 0.0))

        qa.append(_bf(jnp.where(lane_q < D_HD, q, slope_lanes(D_HD))))
        qa.append(_bf(jnp.where(lane_q >= D_HD, q, slope_lanes(0))))
    q_start = past + qi * tq
    q_pos = q_start + _iota2((1, tq), 1)
    q_chunk = q_pos >> shift

    m_ref[...] = jnp.full(m_ref.shape, NEG_BIG, F32)
    acc_ref[...] = jnp.zeros(acc_ref.shape, F32)

    last_allowed = (((q_start + tq - 1) >> shift) << shift) + (CHUNK - 1)
    n_blocks = jnp.minimum(last_allowed, kv_len - 1) // tk + 1
    n_before = jnp.minimum((q_start + 1) // tk, n_blocks)

    def scores(j):
        rows = pl.ds(pl.multiple_of(j * tk, 2 * SUBLANES), tk)
        return [_mm_nt(ka_ref[s, rows, :], qa[s]) for s in range(n_streams)]

    def accumulate(j, s_t, mask):
        m_old = [m_ref[s] for s in range(n_streams)]
        m_new = [jnp.maximum(m_old[s], jnp.max(s_t[s], axis=0, keepdims=True))
                 for s in range(n_streams)]
        p_t = [jnp.exp(s_t[s] - m_new[s]) for s in range(n_streams)]
        if mask is not None:
            p_t = [jnp.where(mask, p, 0.0) for p in p_t]
        for s in range(n_streams):
            acc_ref[s] = (jnp.exp(m_old[s] - m_new[s]) * acc_ref[s]
                          + _mm(vt_ref[s // 2, j], _bf(p_t[s])))
            m_ref[s] = m_new[s]

    def block_before(j, carry):
        accumulate(j, scores(j), None)
        return carry

    def block_masked(j, carry):
        k_pos = j * tk + _iota2((tk, 1), 0)
        allowed = (k_pos >> shift) <= q_chunk
        late = jnp.maximum(k_pos - q_pos, 0).astype(F32)
        s_t = [jnp.where(allowed, s - (2.0 * slopes[idx // 2]) * late, NEG_BIG)
               for idx, s in enumerate(scores(j))]
        accumulate(j, s_t, allowed)
        return carry

    lax.fori_loop(0, n_before, block_before, 0)
    lax.fori_loop(n_before, n_blocks, block_masked, 0)
    pad = -tq % LANES
    gn = gn_ref[...]
    for g in range(heads):
        a0 = acc_ref[2 * g]
        a1 = acc_ref[2 * g + 1]
        o_t = (a0[:DV_D] / a0[DV_D:DV_D + 1] - lam * (a1[:DV_D] / a1[DV_D:DV_D + 1]))
        if pad:
            o_t = jnp.concatenate([o_t, jnp.zeros((DV_D, pad), F32)], axis=1)
        o = o_t.T[:tq]
        o_ref[:, g * DV_D:(g + 1) * DV_D] = (_rms(o, gn) * (1.0 - lam_init)).astype(BF16)


def diff_attention(zq, q_col, kd, k_col, vd, v_col, lam_params, norm_g, batch, seq, kv_len,
                   tq, tk, lam_init, heads):
    nq = seq // tq
    width = heads * DV_D
    return pl.pallas_call(
        functools.partial(_attn_kernel, seq=seq, kv_len=kv_len, tq=tq, tk=tk, lam_init=lam_init,
                          heads=heads),
        grid=(batch, H_D // heads, nq),
        in_specs=[pl.BlockSpec((tq, width), lambda b, h, i: (b * nq + i, q_col // heads + h)),
                  pl.BlockSpec((kv_len, width), lambda b, h, i: (b, k_col // heads + h)),
                  pl.BlockSpec((kv_len, width), lambda b, h, i: (b, v_col // heads + h)),
                  pl.BlockSpec((4, D_HD), lambda b, h, i: (0, 0)),
                  pl.BlockSpec((1, DV_D), lambda b, h, i: (0, 0))],
        out_specs=pl.BlockSpec((tq, width), lambda b, h, i: (b * nq + i, h)),
        out_shape=jax.ShapeDtypeStruct((batch * seq, H_D * DV_D), BF16),
        scratch_shapes=[pltpu.VMEM((2 * heads, kv_len, DV_D), BF16),
                        pltpu.VMEM((heads, kv_len // tk, DV_D + ONES_ROWS, tk), BF16),
                        pltpu.VMEM((2 * heads, 1, tq), F32),
                        pltpu.VMEM((2 * heads, DV_D + ONES_ROWS, tq), F32)],
        compiler_params=_params("arbitrary", "arbitrary", "arbitrary"),
        name="diff_attention",
    )(zq, kd, vd, lam_params, norm_g.reshape(1, DV_D))


def _pad_cols(w, total):
    return jnp.pad(w, ((0, 0), (0, total - w.shape[1])))


def _prep_w_in_ab(w):
    sizes = (H_A * DK_A, H_A * DK_A, H_A * DV_A, H_A * DV_A, H_B * DK_B, H_B * DK_B, H_B * DV_B,
             H_B, H_B, H_B * DV_B)
    aq, af, ai, ag, bq, bk, bv, big, bfg, bo = jnp.split(w, np_cumsum(sizes), axis=1)
    return _pad_cols(jnp.concatenate([aq, af, ai, ag, bq, bk, bv, bo, big, bfg], axis=1),
                     IN_AB_PAD).astype(BF16)


def _prep_w_in_cd(w):
    sizes = (H_C * (2 * DK_C + DV_C), H_C, H_C, H_C * DV_C, H_D * 2 * D_HD, H_D * 2 * D_HD,
             H_D * DV_D)
    cqkv, cb, ca, cz, dq, dk, dv = jnp.split(w, np_cumsum(sizes), axis=1)
    return _pad_cols(jnp.concatenate([cqkv, cz, dq, dk, dv, cb, ca], axis=1),
                     IN_CD_PAD).astype(BF16)


def np_cumsum(sizes):
    out, acc = [], 0
    for s in sizes[:-1]:
        acc += s
        out.append(acc)
    return out


def _tiles(n_rows, seq):
    tm = min(512, n_rows)
    tb = min(512, seq)
    return tm, tb


def _heads_per_step(n_heads, n_chunks, items=8):
    return max(1, min(n_heads, items // n_chunks))


def _layer_ab(x, st, p, lower_bounds, batch, seq, layer):
    n = batch * seq
    tm, tb = _tiles(n, seq)
    z = norm_matmul(x, p['norm_mix'], p['w_in'], tm, IN_AB_PAD // 3)
    n_chunks = tb // min(MIX_CHUNK, seq)
    o_a, s_new = hgrn2(z, lower_bounds, p['hgrn_norm'], st['hgrn'], batch, seq, tb, layer // 2,
                       _heads_per_step(H_A, n_chunks))
    o_b, c_new, n_new, m_new = mlstm(z, p['i_bias'], p['f_bias'], p['mlstm_norm'], st['mlstm_c'],
                                     st['mlstm_n'], st['mlstm_m'], batch, seq, tb,
                                     _heads_per_step(H_B, n_chunks))
    x = matmul_residual((o_a, o_b), p['w_out'], x, p['norm_ffn'], tm, False)
    return x, (s_new, c_new, n_new, m_new)


def _layer_cd(x, st, p, batch, seq, layer):
    n = batch * seq
    tm, tb = _tiles(n, seq)
    z = norm_matmul(x, p['norm_mix'], p['w_in'], tm, IN_CD_PAD // 3)
    heads = _heads_per_step(H_C, tb // min(CHUNK, seq), items=32)
    o_c, s_new = gdn(z, p['conv_w'], st['gdn_conv'], p['a_log'], p['dt_bias'], p['gdn_norm'],
                     st['gdn'], batch, seq, tb, heads)
    z3 = z.reshape(batch, seq, IN_CD_PAD)
    conv_new = z3[:, seq - (CONV_C - 1):, :H_C * (2 * DK_C + DV_C)]
    k_new = z3[:, :, CD_DK * LANES:CD_DV * LANES]
    v_new = z3[:, :, CD_DV * LANES:CD_GATES * LANES]
    past = st['k_cache'].shape[1]
    lam_init = 0.8 - 0.6 * math.exp(-0.3 * layer)
    if past == 0:
        o_d = diff_attention(z, CD_DQ, z, CD_DK, z, CD_DV, p['lam'], p['diff_norm'], batch, seq,
                             seq, min(512, seq), min(512, seq), lam_init, 2)
    else:
        kv_len = past + seq
        kd = jnp.concatenate([st['k_cache'].reshape(batch, past, -1), k_new], axis=1)
        vd = jnp.concatenate([st['v_cache'].reshape(batch, past, -1), v_new], axis=1)
        o_d = diff_attention(z, CD_DQ, kd.reshape(batch * kv_len, -1), 0,
                             vd.reshape(batch * kv_len, -1), 0, p['lam'], p['diff_norm'], batch,
                             seq, kv_len, seq, kv_len, lam_init, 4)
    x = matmul_residual((o_c, o_d), p['w_out'], x, p['norm_ffn'], tm, False)
    return x, (s_new, conv_new, k_new.reshape(batch, seq, H_D, 2 * D_HD),
               v_new.reshape(batch, seq, H_D, DV_D))


def _ffn(x, state, p, batch, seq, final_g):
    n = batch * seq
    tm, _ = _tiles(n, seq)
    act, conv_new = ffn_up(x, p['norm_ffn'], p['ffn_up'], p['ffn_conv_w'], p['ffn_conv_b'], state,
                           batch, seq, min(2 * tm, n), 512)
    g = p['norm_ffn'] if final_g is None else final_g
    x = matmul_residual((act,), p['ffn_down'], x, g, tm, final_g is not None)
    return x, conv_new


def _run_group(x, st, params, lower_bounds, norm_final):
    batch, seq, _ = x.shape
    x = x.reshape(batch * seq, D_MODEL)
    new_states = []
    for layer, (p, s) in enumerate(zip(params, st)):
        if layer % 2 == 0:
            x, new_mix = _layer_ab(x, s, p, lower_bounds, batch, seq, layer)
        else:
            x, new_mix = _layer_cd(x, s, p, batch, seq, layer)
        last = layer == len(params) - 1
        x, new_ffn = _ffn(x, s['ffn_conv'], p, batch, seq, norm_final if last else None)
        new_states.append(new_mix + (new_ffn,))
    return x.reshape(batch, seq, D_MODEL), new_states


def kernel(x_prompt, x_sample, state_hgrn_0, state_mlstm_c_0, state_mlstm_n_0, state_mlstm_m_0,
           state_ffn_conv_0, state_gdn_1, state_gdn_conv_1, cache_k_1, cache_v_1, state_ffn_conv_1,
           hgrn_lower_bounds, norm_mix_0, w_in_0, mlstm_i_bias_0, mlstm_f_bias_0, hgrn_norm_0,
           mlstm_norm_0, w_out_0, norm_ffn_0, ffn_up_0, ffn_conv_w_0, ffn_conv_b_0, ffn_down_0,
           norm_mix_1, w_in_1, gdn_conv_w_1, gdn_a_log_1, gdn_dt_bias_1, gdn_norm_1, diff_lambda_1,
           diff_norm_1, w_out_1, norm_ffn_1, ffn_up_1, ffn_conv_w_1, ffn_conv_b_1, ffn_down_1,
           norm_final):
    params = [
        dict(norm_mix=norm_mix_0, w_in=_prep_w_in_ab(w_in_0), i_bias=mlstm_i_bias_0,
             f_bias=mlstm_f_bias_0, hgrn_norm=hgrn_norm_0, mlstm_norm=mlstm_norm_0,
             w_out=w_out_0.astype(BF16), norm_ffn=norm_ffn_0, ffn_up=ffn_up_0.astype(BF16),
             ffn_conv_w=ffn_conv_w_0, ffn_conv_b=ffn_conv_b_0, ffn_down=ffn_down_0.astype(BF16)),
        dict(norm_mix=norm_mix_1, w_in=_prep_w_in_cd(w_in_1), conv_w=gdn_conv_w_1,
             a_log=gdn_a_log_1, dt_bias=gdn_dt_bias_1, gdn_norm=gdn_norm_1, lam=diff_lambda_1,
             diff_norm=diff_norm_1, w_out=w_out_1.astype(BF16), norm_ffn=norm_ffn_1,
             ffn_up=ffn_up_1.astype(BF16), ffn_conv_w=ffn_conv_w_1, ffn_conv_b=ffn_conv_b_1,
             ffn_down=ffn_down_1.astype(BF16)),
    ]
    bp = x_prompt.shape[0]
    st_prompt = [
        dict(hgrn=jnp.zeros((bp, H_A, DK_A, DV_A), F32), mlstm_c=jnp.zeros((bp, H_B, DK_B, DV_B), F32),
             mlstm_n=jnp.zeros((bp, H_B, DK_B), F32), mlstm_m=jnp.full((bp, H_B), NEG_BIG, F32),
             ffn_conv=jnp.zeros((bp, CONV_FFN - 1, D_FF), F32)),
        dict(gdn=jnp.zeros((bp, H_C, DK_C, DV_C), F32),
             gdn_conv=jnp.zeros((bp, CONV_C - 1, H_C * (2 * DK_C + DV_C)), F32),
             k_cache=jnp.zeros((bp, 0, H_D, 2 * D_HD), F32), v_cache=jnp.zeros((bp, 0, H_D, DV_D), F32),
             ffn_conv=jnp.zeros((bp, CONV_FFN - 1, D_FF), F32)),
    ]
    st_sample = [
        dict(hgrn=state_hgrn_0, mlstm_c=state_mlstm_c_0, mlstm_n=state_mlstm_n_0,
             mlstm_m=state_mlstm_m_0, ffn_conv=state_ffn_conv_0),
        dict(gdn=state_gdn_1, gdn_conv=state_gdn_conv_1, k_cache=cache_k_1, v_cache=cache_v_1,
             ffn_conv=state_ffn_conv_1),
    ]
    y_p, new_p = _run_group(x_prompt, st_prompt, params, hgrn_lower_bounds, norm_final)
    y_s, new_s = _run_group(x_sample, st_sample, params, hgrn_lower_bounds, norm_final)
    (hgrn_p, c_p, n_p, m_p, f0_p), (gdn_p, gc_p, k_p, v_p, f1_p) = new_p
    (hgrn_s, c_s, n_s, m_s, f0_s), (gdn_s, gc_s, k_s, v_s, f1_s) = new_s
    return (y_p, y_s, hgrn_p, hgrn_s, c_p, c_s, n_p, n_s, m_p, m_s, f0_p, f0_s, gdn_p, gdn_s,
            gc_p, gc_s, k_p, k_s, v_p, v_s, f1_p, f1_s)
```

```python
import functools
import math

import jax
import jax.numpy as jnp
from jax import lax
from jax.experimental import pallas as pl
from jax.experimental.pallas import tpu as pltpu

F32 = jnp.float32
BF16 = jnp.bfloat16
HI = lax.Precision.HIGHEST

D_MODEL = 2048
CHUNK = 64
H_A, DK_A, DV_A = 8, 128, 128
H_B, DK_B, DV_B = 4, 128, 256
H_C, DK_C, DV_C = 8, 128, 128
CONV_C = 4
H_D, D_HD = 8, 64
DV_D = 2 * D_HD
D_FF = 5632
CONV_FFN = 3
EPS = 1e-6
NEG_BIG = -1e30

LANES = 128
SUBLANES = 8
ONES_ROWS = 2 * SUBLANES
VMEM_LIMIT = 56 * 1024 * 1024
MIX_CHUNK = 128

AB_Q, AB_F, AB_I, AB_G = 0, 8, 16, 24
AB_BQ, AB_BK = 32, 36
AB_BV, AB_BO = 20, 24
AB_GATES = 56
IN_AB_PAD = 57 * LANES
CD_Q, CD_K, CD_V, CD_Z = 0, 8, 16, 24
CD_DQ, CD_DK, CD_DV = 32, 40, 48
CD_GATES = 56
IN_CD_PAD = 57 * LANES


def _mm(a, b, prec=None):
    return lax.dot_general(a, b, (((1,), (0,)), ((), ())), precision=prec,
                           preferred_element_type=F32)


def _mm_nt(a, b, prec=None):
    return lax.dot_general(a, b, (((1,), (1,)), ((), ())), precision=prec,
                           preferred_element_type=F32)


def _mm_tn(a, b, prec=None):
    return lax.dot_general(a, b, (((0,), (0,)), ((), ())), precision=prec,
                           preferred_element_type=F32)


def _sigmoid(x):
    return 1.0 / (1.0 + jnp.exp(-x))


def _silu(x):
    return x * _sigmoid(x)


def _softplus(x):
    return jnp.maximum(x, 0.0) + jnp.log1p(jnp.exp(-jnp.abs(x)))


def _rms(x, g):
    return x * lax.rsqrt(jnp.mean(x * x, axis=-1, keepdims=True) + EPS) * g


def _params(*sem):
    return pltpu.CompilerParams(dimension_semantics=sem, vmem_limit_bytes=VMEM_LIMIT)


def _norm_matmul_kernel(x_ref, g_ref, w_ref, o_ref, h_ref):
    @pl.when(pl.program_id(1) == 0)
    def _():
        h_ref[...] = _rms(x_ref[...], g_ref[...]).astype(BF16)

    o_ref[...] = _mm(h_ref[...], w_ref[...])


def norm_matmul(x, g, w, tm, tn):
    n, k = x.shape
    m = w.shape[1]
    return pl.pallas_call(
        _norm_matmul_kernel,
        grid=(n // tm, m // tn),
        in_specs=[pl.BlockSpec((tm, k), lambda i, j: (i, 0)),
                  pl.BlockSpec((1, k), lambda i, j: (0, 0)),
                  pl.BlockSpec((k, tn), lambda i, j: (0, j))],
        out_specs=pl.BlockSpec((tm, tn), lambda i, j: (i, j)),
        out_shape=jax.ShapeDtypeStruct((n, m), F32),
        scratch_shapes=[pltpu.VMEM((tm, k), BF16)],
        compiler_params=_params("arbitrary", "arbitrary"),
        name="norm_matmul",
    )(x, g.reshape(1, k), w)


def _matmul_res_kernel(*refs, n_parts, final_norm):
    y_refs = refs[:n_parts]
    w_ref, x_ref, g_ref, o_ref = refs[n_parts:]
    k_part = y_refs[0].shape[1]
    acc = x_ref[...]
    for part, y_ref in enumerate(y_refs):
        acc = acc + _mm(y_ref[...], w_ref[part * k_part:(part + 1) * k_part, :])
    o_ref[...] = _rms(acc, g_ref[...]) if final_norm else acc


def matmul_residual(ys, w, x, g, tm, final_norm):
    n, k_part = ys[0].shape
    k, d = w.shape
    n_parts = len(ys)
    return pl.pallas_call(
        functools.partial(_matmul_res_kernel, n_parts=n_parts, final_norm=final_norm),
        grid=(n // tm,),
        in_specs=[pl.BlockSpec((tm, k_part), lambda i: (i, 0)) for _ in range(n_parts)]
        + [pl.BlockSpec((k, d), lambda i: (0, 0), pipeline_mode=pl.Buffered(1)),
           pl.BlockSpec((tm, d), lambda i: (i, 0)),
           pl.BlockSpec((1, d), lambda i: (0, 0))],
        out_specs=pl.BlockSpec((tm, d), lambda i: (i, 0)),
        out_shape=jax.ShapeDtypeStruct((n, d), F32),
        compiler_params=_params("arbitrary"),
        name="matmul_residual",
    )(*ys, w, x, g.reshape(1, d))


def _ffn_up_kernel(x_ref, g_ref, wa_ref, wu_ref, cw_ref, cb_ref, st_ref, act_ref, tail_ref,
                   h_ref, ext_ref, carry_ref, *, nseq, rows, tiles_per_seq):
    i = pl.program_id(0)
    j = pl.program_id(1)
    halo = CONV_FFN - 1

    @pl.when(j == 0)
    def _():
        h_ref[...] = _rms(x_ref[...], g_ref[...]).astype(BF16)

    if tiles_per_seq > 1:
        @pl.when(i == 0)
        def _():
            carry_ref[j] = jnp.zeros(carry_ref.shape[1:], F32)

    h = h_ref[...]
    a = _mm(h, wa_ref[...])
    u = _mm(h, wu_ref[...])
    cw = cw_ref[...]
    cb = cb_ref[...]
    for s in range(nseq):
        lo = s * rows
        ext_ref[s, SUBLANES:, :] = a[lo:lo + rows]
        prev = st_ref[s]
        if tiles_per_seq > 1:
            prev = jnp.where((i % tiles_per_seq) == 0, prev, carry_ref[j, SUBLANES - halo:, :])
        ext_ref[s, SUBLANES - halo:SUBLANES, :] = prev
        conv = cb
        for t in range(CONV_FFN):
            off = SUBLANES - halo + t
            conv = conv + ext_ref[s, off:off + rows, :] * cw[t:t + 1]
        act_ref[lo:lo + rows, :] = (_silu(conv) * u[lo:lo + rows]).astype(BF16)
        tail_ref[s] = ext_ref[s, SUBLANES + rows - halo:SUBLANES + rows, :]
        carry_ref[j] = ext_ref[s, rows:rows + SUBLANES, :]


def ffn_up(x, g, w_up, conv_w, conv_b, state, batch, seq, tm, tf):
    n, k = x.shape
    nff = D_FF // tf
    if tm >= seq:
        nseq, rows, tiles_per_seq = tm // seq, seq, 1
    else:
        nseq, rows, tiles_per_seq = 1, tm, seq // tm
    ntiles = n // tm
    halo = CONV_FFN - 1
    act, tail = pl.pallas_call(
        functools.partial(_ffn_up_kernel, nseq=nseq, rows=rows, tiles_per_seq=tiles_per_seq),
        grid=(ntiles, nff),
        in_specs=[pl.BlockSpec((tm, k), lambda i, j: (i, 0)),
                  pl.BlockSpec((1, k), lambda i, j: (0, 0)),
                  pl.BlockSpec((k, tf), lambda i, j: (0, j)),
                  pl.BlockSpec((k, tf), lambda i, j: (0, j + nff)),
                  pl.BlockSpec((CONV_FFN, tf), lambda i, j: (0, j)),
                  pl.BlockSpec((1, tf), lambda i, j: (0, j)),
                  pl.BlockSpec((nseq, halo, tf), lambda i, j: (i // tiles_per_seq, 0, j))],
        out_specs=[pl.BlockSpec((tm, tf), lambda i, j: (i, j)),
                   pl.BlockSpec((nseq, halo, tf), lambda i, j: (i, 0, j))],
        out_shape=[jax.ShapeDtypeStruct((n, D_FF), BF16),
                   jax.ShapeDtypeStruct((ntiles * nseq, halo, D_FF), F32)],
        scratch_shapes=[pltpu.VMEM((tm, k), BF16),
                        pltpu.VMEM((nseq, SUBLANES + rows, tf), F32),
                        pltpu.VMEM((nff, SUBLANES, tf), F32)],
        compiler_params=_params("arbitrary", "arbitrary"),
        name="ffn_up",
    )(x, g.reshape(1, k), w_up, w_up, conv_w, conv_b.reshape(1, D_FF), state)
    new_state = tail.reshape(batch, -1, halo, D_FF)[:, -1]
    return act, new_state


def _iota2(shape, dim):
    return lax.broadcasted_iota(jnp.int32, shape, dim)


def _tril(n):
    return (_iota2((n, n), 1) <= _iota2((n, n), 0)).astype(F32)


def _lane_col(x, lane):
    return jnp.sum(jnp.where(_iota2(x.shape, 1) == lane, x, 0.0), axis=-1, keepdims=True)


def _bf(x):
    return x.astype(BF16)


def _split3(x):
    h1 = _bf(x)
    r1 = x - h1.astype(F32)
    h2 = _bf(r1)
    h3 = _bf(r1 - h2.astype(F32))
    return h1, h2, h3


def _mm_sel(sel, x):
    n = x.shape[1]
    y = _mm(_bf(sel), jnp.concatenate(_split3(x), axis=1))
    return y[:, :n] + y[:, n:2 * n] + y[:, 2 * n:]


def _transpose_rows(x):
    r = x.shape[0]
    pad = -r % LANES
    if pad:
        x = jnp.concatenate([x, jnp.zeros((pad, x.shape[1]), x.dtype)], axis=0)
    return x.T


def _chunk_cumsum_rows(row, tri):
    r = row.shape[1]
    h1, h2, h3 = (p.astype(F32) for p in _split3(row))
    sub = _iota2((2 * SUBLANES, r), 0)
    parts = jnp.where(sub == 0, h1, jnp.where(sub == 1, h2, jnp.where(sub == 2, h3, 0.0)))
    y = _mm(_bf(parts), tri)
    return y[0:1] + y[1:2] + y[2:3]


def _row_to_col(row):
    n = row.shape[1]
    eye = _iota2((n, n), 0) == _iota2((n, n), 1)
    return jnp.sum(jnp.where(eye, row, 0.0), axis=-1, keepdims=True)


def _chunk_tri(tb, chunk):
    s = lax.broadcasted_iota(jnp.int32, (tb, tb), 0)
    t = lax.broadcasted_iota(jnp.int32, (tb, tb), 1)
    return ((s <= t) & (s // chunk == t // chunk)).astype(BF16)


def _hgrn_kernel(q_ref, f_ref, i_ref, g_ref, lb_ref, gn_ref, s0_ref, o_ref, s_out_ref, s_ref,
                 *, L, n_chunks, lb_index, heads):
    tb = pl.program_id(2)

    @pl.when(tb == 0)
    def _():
        s_ref[...] = s0_ref[0]

    lbp = lb_ref[...]
    e = jnp.exp(lbp - jnp.max(lbp, axis=0, keepdims=True))
    lb_all = jnp.sum(e[:lb_index + 1], axis=0, keepdims=True) / jnp.sum(e, axis=0, keepdims=True)

    sizes = [L >> l for l in range(int(math.log2(L)))]
    ti = _iota2((L, L), 0)
    si = _iota2((L, L), 1)
    tcol = _iota2((L, 1), 0)
    mats = [si <= ti]
    for sz in sizes:
        mats.append(si <= ((ti & ~(sz - 1)) + (sz // 2 - 1)))
    cum_mat = jnp.concatenate([m.astype(F32) for m in mats], axis=0)
    eye = ti == si
    gn = gn_ref[...]

    chunks = []
    for c in range(n_chunks):
        rows = pl.ds(c * L, L)
        for g in range(heads):
            lanes = slice(g * LANES, (g + 1) * LANES)
            lb = lb_all[:, lanes]
            q = q_ref[rows, lanes]
            f = lb + (1.0 - lb) * _sigmoid(f_ref[rows, lanes])
            k = 1.0 - f
            cums = _mm_sel(cum_mat, jnp.log(f))
            chunks.append(dict(
                rows=rows, lanes=lanes, g=g, q=q, k=k, cums=cums, b=cums[:L],
                vb=_bf(i_ref[rows, lanes]),
                scores=jnp.where(eye, jnp.sum(q * k, axis=-1, keepdims=True), 0.0)))
    for l, sz in enumerate(sizes):
        right = (tcol & (sz - 1)) >= sz // 2
        same = (ti & ~(sz - 1)) == (si & ~(sz - 1))
        for ch in chunks:
            w = jnp.exp(-jnp.abs(ch['b'] - ch['cums'][(l + 1) * L:(l + 2) * L]))
            qm = jnp.where(right, ch['q'] * w, 0.0)
            km = jnp.where(right, 0.0, ch['k'] * w)
            ch['scores'] = ch['scores'] + jnp.where(same, _mm_nt(_bf(qm), _bf(km)), 0.0)
    for ch in chunks:
        bl = ch['b'][L - 1:L]
        ch['update'] = _mm_tn(_bf(ch['k'] * jnp.exp(bl - ch['b'])), ch['vb'])
        ch['decay'] = _row_to_col(jnp.exp(bl))
    for ch in chunks:
        s = s_ref[ch['g']]
        o = _mm(jnp.concatenate([_bf(ch['q'] * jnp.exp(ch['b'])), _bf(ch['scores'])], axis=1),
                jnp.concatenate([_bf(s), ch['vb']], axis=0))
        s_ref[ch['g']] = ch['decay'] * s + ch['update']
        o_ref[ch['rows'], ch['lanes']] = (
            _rms(o, gn) * _silu(g_ref[ch['rows'], ch['lanes']])).astype(BF16)

    @pl.when(tb == pl.num_programs(2) - 1)
    def _():
        s_out_ref[0] = s_ref[...]


def hgrn2(z, lower_bounds, norm_g, s0, batch, seq, tb, lb_index, heads):
    L = min(MIX_CHUNK, seq)
    nt = seq // tb
    width = heads * LANES

    def col(c0):
        return pl.BlockSpec((tb, width), lambda b, h, t: (b * nt + t, c0 // heads + h))

    st_spec = pl.BlockSpec((1, heads, DK_A, DV_A), lambda b, h, t: (b, h, 0, 0))
    return pl.pallas_call(
        functools.partial(_hgrn_kernel, L=L, n_chunks=tb // L, lb_index=lb_index, heads=heads),
        grid=(batch, H_A // heads, nt),
        in_specs=[col(AB_Q), col(AB_F), col(AB_I), col(AB_G),
                  pl.BlockSpec((lower_bounds.shape[0], width), lambda b, h, t: (0, h)),
                  pl.BlockSpec((1, DV_A), lambda b, h, t: (0, 0)),
                  st_spec],
        out_specs=[pl.BlockSpec((tb, width), lambda b, h, t: (b * nt + t, h)), st_spec],
        out_shape=[jax.ShapeDtypeStruct((batch * seq, H_A * DV_A), BF16),
                   jax.ShapeDtypeStruct((batch, H_A, DK_A, DV_A), F32)],
        scratch_shapes=[pltpu.VMEM((heads, DK_A, DV_A), F32)],
        compiler_params=_params("arbitrary", "arbitrary", "arbitrary"),
        name="hgrn2",
    )(z, z, z, z, lower_bounds, norm_g.reshape(1, DV_A), s0)


def _mlstm_kernel(q_ref, k_ref, v_ref, og_ref, gt_ref, bias_ref, tri_ref, gn_ref, c0_ref, n0_ref,
                  m0_ref, o_ref, c_out_ref, n_out_ref, m_out_ref, c_ref, n_ref, m_ref, gtt_ref,
                  *, L, n_chunks, heads):
    h0 = pl.program_id(1) * heads
    tb = pl.program_id(2)

    @pl.when(tb == 0)
    def _():
        c_ref[...] = c0_ref[0]
        n_ref[...] = n0_ref[0]
        m_ref[...] = m0_ref[0]

    n_rows = L * n_chunks
    gtt_ref[...] = _transpose_rows(gt_ref[...] + bias_ref[...])
    ig_rows, b_rows = [], []
    for g in range(heads):
        ig_rows.append(gtt_ref[pl.ds(h0 + g, 1), :][:, :n_rows])
        f_rows = gtt_ref[pl.ds(H_B + h0 + g, 1), :][:, :n_rows]
        logf = jnp.minimum(f_rows, 0.0) - jnp.log1p(jnp.exp(-jnp.abs(f_rows)))
        b_rows.append(_chunk_cumsum_rows(logf, tri_ref[...]))

    causal = _iota2((L, L), 1) <= _iota2((L, L), 0)
    gn = gn_ref[...]
    scale = DK_B ** -0.5

    chunks = []
    for c in range(n_chunks):
        rows = pl.ds(c * L, L)
        for g in range(heads):
            q = q_ref[rows, g * DK_B:(g + 1) * DK_B] * scale
            k = k_ref[rows, g * DK_B:(g + 1) * DK_B]
            vb = _bf(v_ref[rows, g * DV_B:(g + 1) * DV_B])
            ig_row = ig_rows[g][:, c * L:(c + 1) * L]
            b_row = b_rows[g][:, c * L:(c + 1) * L]
            b_col = _row_to_col(b_row)
            d = b_col - b_row + ig_row
            d_max = jnp.max(jnp.where(causal, d, -jnp.inf), axis=-1, keepdims=True)
            p = jnp.exp(jnp.where(causal, d - d_max, -jnp.inf)) * _mm_nt(_bf(q), _bf(k))
            b_last = b_row[:, L - 1:L]
            wk_max = jnp.max(b_last - b_row + ig_row, axis=-1, keepdims=True)
            ks = jnp.exp(b_last - b_col + _row_to_col(ig_row) - wk_max) * k
            chunks.append(dict(
                rows=rows, g=g, q=q, b_col=b_col, d_max=d_max, b_last=b_last, wk_max=wk_max,
                p_sum=jnp.sum(p, axis=-1, keepdims=True), pv=_mm(_bf(p), vb),
                k_sum=jnp.sum(ks, axis=0, keepdims=True), kv=_mm_tn(_bf(ks), vb)))

    for ch in chunks:
        g = ch['g']
        q = ch['q']
        cs = c_ref[g]
        n = n_ref[g]
        m_prev = m_ref[g][:, 0:1]
        inter = ch['b_col'] + m_prev
        m_t = jnp.maximum(ch['d_max'], inter)
        w_intra = jnp.exp(ch['d_max'] - m_t)
        w_inter = jnp.exp(inter - m_t)
        num = w_intra * ch['pv'] + w_inter * _mm(_bf(q), _bf(cs))
        den = w_intra * ch['p_sum'] + w_inter * jnp.sum(q * n, axis=-1, keepdims=True)
        hid = num / jnp.maximum(jnp.abs(den), jnp.exp(-m_t))
        m_new = jnp.maximum(ch['b_last'] + m_prev, ch['wk_max'])
        w_new = jnp.exp(ch['wk_max'] - m_new)
        carry_w = jnp.exp(ch['b_last'] + m_prev - m_new)
        c_ref[g] = carry_w * cs + w_new * ch['kv']
        n_ref[g] = carry_w * n + w_new * ch['k_sum']
        m_ref[g] = jnp.broadcast_to(m_new, (1, LANES))
        lanes = slice(g * DV_B, (g + 1) * DV_B)
        o_ref[ch['rows'], lanes] = (
            _rms(hid, gn) * _sigmoid(og_ref[ch['rows'], lanes])).astype(BF16)

    @pl.when(tb == pl.num_programs(2) - 1)
    def _():
        c_out_ref[0] = c_ref[...]
        n_out_ref[0] = n_ref[...]
        m_out_ref[0] = m_ref[...]


def mlstm(z, i_bias, f_bias, norm_g, c0, n0, m0, batch, seq, tb, heads):
    L = min(MIX_CHUNK, seq)
    nt = seq // tb
    bias = jnp.zeros((1, LANES), F32).at[0, :H_B].set(i_bias).at[0, H_B:2 * H_B].set(f_bias)

    def col(c0_, width):
        return pl.BlockSpec((tb, heads * width), lambda b, h, t: (b * nt + t, c0_ // heads + h))

    def state(r, c):
        return pl.BlockSpec((1, heads, r, c), lambda b, h, t: (b, h, 0, 0))

    o, cs, n, m = pl.pallas_call(
        functools.partial(_mlstm_kernel, L=L, n_chunks=tb // L, heads=heads),
        grid=(batch, H_B // heads, nt),
        in_specs=[col(AB_BQ, DK_B), col(AB_BK, DK_B), col(AB_BV, DV_B), col(AB_BO, DV_B),
                  pl.BlockSpec((tb, LANES), lambda b, h, t: (b * nt + t, AB_GATES)),
                  pl.BlockSpec((1, LANES), lambda b, h, t: (0, 0)),
                  pl.BlockSpec((tb, tb), lambda b, h, t: (0, 0)),
                  pl.BlockSpec((1, DV_B), lambda b, h, t: (0, 0)),
                  state(DK_B, DV_B), state(1, DK_B), state(1, LANES)],
        out_specs=[pl.BlockSpec((tb, heads * DV_B), lambda b, h, t: (b * nt + t, h)),
                   state(DK_B, DV_B), state(1, DK_B), state(1, LANES)],
        out_shape=[jax.ShapeDtypeStruct((batch * seq, H_B * DV_B), BF16),
                   jax.ShapeDtypeStruct((batch, H_B, DK_B, DV_B), F32),
                   jax.ShapeDtypeStruct((batch, H_B, 1, DK_B), F32),
                   jax.ShapeDtypeStruct((batch, H_B, 1, LANES), F32)],
        scratch_shapes=[pltpu.VMEM((heads, DK_B, DV_B), F32), pltpu.VMEM((heads, 1, DK_B), F32),
                        pltpu.VMEM((heads, 1, LANES), F32),
                        pltpu.VMEM((LANES, tb + (-tb % LANES)), F32)],
        compiler_params=_params("arbitrary", "arbitrary", "arbitrary"),
        name="mlstm",
    )(z, z, z, z, z, bias, _chunk_tri(tb, L), norm_g.reshape(1, DV_B), c0,
      n0.reshape(batch, H_B, 1, DK_B),
      jnp.broadcast_to(m0.reshape(batch, H_B, 1, 1), (batch, H_B, 1, LANES)))
    return o, cs, n.reshape(batch, H_B, DK_B), m[:, :, 0, 0]


def _gdn_kernel(q_ref, k_ref, v_ref, z_ref, gt_ref, cw_ref, cs_ref, gp_ref, tri_ref, gn_ref, s0_ref,
                o_ref, s_out_ref, st_ref, ext_ref, cv_ref, lgt_ref, *, L, n_chunks, tb, heads):
    h0 = pl.program_id(1) * heads
    t = pl.program_id(2)
    halo = CONV_C - 1

    @pl.when(t == 0)
    def _():
        st_ref[...] = s0_ref[0]
        for p in range(3):
            ext_ref[p, SUBLANES - halo:SUBLANES, :] = cs_ref[p, 0]

    srcs = (q_ref, k_ref, v_ref)
    for p in range(3):
        ext_ref[p, SUBLANES:, :] = srcs[p][...]
        cw = cw_ref[p]
        acc = None
        for j in range(CONV_C):
            off = SUBLANES - halo + j
            term = ext_ref[p, off:off + tb, :] * cw[j:j + 1]
            acc = term if acc is None else acc + term
        cv_ref[p] = _silu(acc)
    gp = gp_ref[...]
    gn = gn_ref[...]

    lgt_ref[...] = _transpose_rows(-jnp.exp(gp[0:1]) * _softplus(gt_ref[...] + gp[1:2]))
    b_rows = [_chunk_cumsum_rows(lgt_ref[pl.ds(H_C + h0 + g, 1), :][:, :tb], tri_ref[...])
              for g in range(heads)]

    def l2n(x):
        return x * lax.rsqrt(jnp.sum(x * x, axis=-1, keepdims=True) + EPS)

    grp = 2 if n_chunks % 2 == 0 else 1
    R = grp * L
    ti = _iota2((R, R), 0)
    si = _iota2((R, R), 1)
    shift = int(math.log2(L))
    same = (ti >> shift) == (si >> shift)
    incl = (si <= ti) & same
    strict = (si < ti) & same

    groups = []
    for gi in range(n_chunks // grp):
        rows = pl.ds(gi * R, R)
        sig = _sigmoid(gt_ref[rows, :])
        for g in range(heads):
            lanes = slice(g * LANES, (g + 1) * LANES)
            q = l2n(cv_ref[0, rows, lanes]) * (DK_C ** -0.5)
            k = l2n(cv_ref[1, rows, lanes])
            v = cv_ref[2, rows, lanes]
            beta = _lane_col(sig, h0 + g)
            b_row = b_rows[g][:, gi * R:(gi + 1) * R]
            b_col = _row_to_col(b_row)
            dec_incl = jnp.exp(jnp.where(incl, b_col - b_row, -jnp.inf))
            eb = jnp.exp(b_col)
            kb = _bf(k)
            qb = _bf(q)
            kq = _mm_nt(jnp.concatenate([kb, qb], axis=0), kb)
            subs = []
            for s in range(grp):
                sub = slice(s * L, (s + 1) * L)
                b_last = b_row[:, (s + 1) * L - 1:(s + 1) * L]
                subs.append(dict(sub=sub, decay=jnp.exp(b_last),
                                 k_dec=_bf(k[sub] * jnp.exp(b_last - b_col[sub]))))
            groups.append(dict(
                rows=rows, lanes=lanes, g=g, qb=qb, eb=eb, subs=subs,
                pw=-(beta * kq[:R] * jnp.where(strict, dec_incl, 0.0)),
                attn=_bf(kq[R:] * dec_incl),
                x=jnp.concatenate([beta * v, (beta * eb) * k], axis=1)))

    eye = jnp.where(ti == si, 1.0, 0.0)
    n_steps = shift
    for gr in groups:
        gr['inv'] = eye + gr['pw']
        pwb = _bf(gr['pw'])
        gr['pw'] = _mm(pwb, pwb)
    for step in range(1, n_steps):
        for gr in groups:
            inv, pw = gr['inv'], gr['pw']
            if step == n_steps - 1:
                gr['inv'] = inv + _mm(_bf(pw), _bf(inv))
            else:
                y = _mm(_bf(pw), jnp.concatenate([_bf(inv), _bf(pw)], axis=1))
                gr['inv'] = inv + y[:, :R]
                gr['pw'] = y[:, R:]
    for gr in groups:
        gr['x'] = _mm(_bf(gr['inv']), _bf(gr['x']))

    for gr in groups:
        deltas, qs = [], []
        for sb in gr['subs']:
            sub = sb['sub']
            w = gr['x'][sub, :DV_C]
            u = gr['x'][sub, DV_C:]
            st = st_ref[gr['g']]
            uq = _mm_nt(jnp.concatenate([_bf(u), gr['qb'][sub]], axis=0), _bf(st))
            db = _bf(w - uq[:L])
            st_ref[gr['g']] = sb['decay'] * st + _mm_tn(db, sb['k_dec'])
            deltas.append(db)
            qs.append(uq[L:])
        o = (gr['eb'] * jnp.concatenate(qs, axis=0)
             + _mm(gr['attn'], jnp.concatenate(deltas, axis=0)))
        o_ref[gr['rows'], gr['lanes']] = (
            _rms(o, gn) * _silu(z_ref[gr['rows'], gr['lanes']])).astype(BF16)

    for p in range(3):
        ext_ref[p, 0:SUBLANES, :] = ext_ref[p, tb:tb + SUBLANES, :]

    @pl.when(t == pl.num_programs(2) - 1)
    def _():
        s_out_ref[0] = st_ref[...]


def gdn(z, conv_w, conv_state, a_log, dt_bias, norm_g, s0, batch, seq, tb, heads):
    L = min(CHUNK, seq)
    nt = seq // tb
    halo = CONV_C - 1
    gate_params = (jnp.zeros((2, LANES), F32).at[0, H_C:2 * H_C].set(a_log)
                   .at[1, H_C:2 * H_C].set(dt_bias))
    cw = conv_w.reshape(CONV_C, 3, H_C * DK_C).transpose(1, 0, 2)
    cs = conv_state.reshape(batch, halo, 3, H_C * DK_C).transpose(2, 0, 1, 3)

    width = heads * LANES

    def col(c0):
        return pl.BlockSpec((tb, width), lambda b, h, t: (b * nt + t, c0 // heads + h))

    st_spec = pl.BlockSpec((1, heads, DV_C, DK_C), lambda b, h, t: (b, h, 0, 0))
    o, st = pl.pallas_call(
        functools.partial(_gdn_kernel, L=L, n_chunks=tb // L, tb=tb, heads=heads),
        grid=(batch, H_C // heads, nt),
        in_specs=[col(CD_Q), col(CD_K), col(CD_V), col(CD_Z),
                  pl.BlockSpec((tb, LANES), lambda b, h, t: (b * nt + t, CD_GATES)),
                  pl.BlockSpec((3, CONV_C, width), lambda b, h, t: (0, 0, h)),
                  pl.BlockSpec((3, 1, halo, width), lambda b, h, t: (0, b, 0, h)),
                  pl.BlockSpec((2, LANES), lambda b, h, t: (0, 0)),
                  pl.BlockSpec((tb, tb), lambda b, h, t: (0, 0)),
                  pl.BlockSpec((1, DV_C), lambda b, h, t: (0, 0)),
                  st_spec],
        out_specs=[pl.BlockSpec((tb, width), lambda b, h, t: (b * nt + t, h)), st_spec],
        out_shape=[jax.ShapeDtypeStruct((batch * seq, H_C * DV_C), BF16),
                   jax.ShapeDtypeStruct((batch, H_C, DV_C, DK_C), F32)],
        scratch_shapes=[pltpu.VMEM((heads, DV_C, DK_C), F32),
                        pltpu.VMEM((3, SUBLANES + tb, width), F32),
                        pltpu.VMEM((3, tb, width), F32),
                        pltpu.VMEM((LANES, tb + (-tb % LANES)), F32)],
        compiler_params=_params("arbitrary", "arbitrary", "arbitrary"),
        name="gdn",
    )(z, z, z, z, z, cw, cs, gate_params, _chunk_tri(tb, L), norm_g.reshape(1, DV_C),
      jnp.swapaxes(s0, 2, 3))
    return o, jnp.swapaxes(st, 2, 3)


def _attn_kernel(q_ref, k_ref, v_ref, lp_ref, gn_ref, o_ref, ka_ref, vt_ref, m_ref, acc_ref,
                 *, seq, kv_len, tq, tk, lam_init, heads):
    h0 = pl.program_id(1) * heads
    qi = pl.program_id(2)
    past = kv_len - seq
    shift = int(math.log2(CHUNK))
    n_kb = kv_len // tk
    n_streams = 2 * heads

    @pl.when(qi == 0)
    def _():
        lane = _iota2((kv_len, 2 * D_HD), 1)
        k_idx = _iota2((kv_len, 1), 0)
        chunk_f = (k_idx >> shift).astype(F32)
        rem_f = (k_idx & (CHUNK - 1)).astype(F32)

        def pos_lanes(base):
            return jnp.where(lane == base, chunk_f, jnp.where(lane == base + 1, rem_f, 0.0))

        for g in range(heads):
            k = k_ref[:, g * DV_D:(g + 1) * DV_D]
            ka_ref[2 * g] = _bf(jnp.where(lane < D_HD, k, pos_lanes(D_HD)))
            ka_ref[2 * g + 1] = _bf(jnp.where(lane >= D_HD, k, pos_lanes(0)))
            vt = _transpose_rows(v_ref[:, g * DV_D:(g + 1) * DV_D])
            for jb in range(n_kb):
                vt_ref[g, jb, :DV_D, :] = _bf(vt[:, jb * tk:(jb + 1) * tk])
                vt_ref[g, jb, DV_D:, :] = jnp.ones((ONES_ROWS, tk), BF16)

    lp = lp_ref[...]
    lam = (jnp.exp(jnp.sum(lp[0:1] * lp[1:2], axis=-1, keepdims=True))
           - jnp.exp(jnp.sum(lp[2:3] * lp[3:4], axis=-1, keepdims=True)) + lam_init)

    lane_q = _iota2((tq, 2 * D_HD), 1)
    qa = []
    slopes = []
    for g in range(heads):
        hf = jnp.full((1, 1), h0 + g + 1, jnp.int32).astype(F32)
        slope = jnp.exp(hf * (-8.0 / H_D * math.log(2.0)))
        slopes.append(slope)
        q = q_ref[:, g * DV_D:(g + 1) * DV_D] * (D_HD ** -0.5)

        def slope_lanes(base, slope=slope):
            return jnp.where(lane_q == base, slope * CHUNK,
                             jnp.where(lane_q == base + 1, slope, 0.0))

        qa.append(_bf(jnp.where(lane_q < D_HD, q, slope_lanes(D_HD))))
        qa.append(_bf(jnp.where(lane_q >= D_HD, q, slope_lanes(0))))
    q_start = past + qi * tq
    q_pos = q_start + _iota2((1, tq), 1)
    q_chunk = q_pos >> shift

    m_ref[...] = jnp.full(m_ref.shape, NEG_BIG, F32)
    acc_ref[...] = jnp.zeros(acc_ref.shape, F32)

    last_allowed = (((q_start + tq - 1) >> shift) << shift) + (CHUNK - 1)
    n_blocks = jnp.minimum(last_allowed, kv_len - 1) // tk + 1
    n_before = jnp.minimum((q_start + 1) // tk, n_blocks)

    def scores(j):
        rows = pl.ds(pl.multiple_of(j * tk, 2 * SUBLANES), tk)
        return [_mm_nt(ka_ref[s, rows, :], qa[s]) for s in range(n_streams)]

    def accumulate(j, s_t, mask):
        m_old = [m_ref[s] for s in range(n_streams)]
        m_new = [jnp.maximum(m_old[s], jnp.max(s_t[s], axis=0, keepdims=True))
                 for s in range(n_streams)]
        p_t = [jnp.exp(s_t[s] - m_new[s]) for s in range(n_streams)]
        if mask is not None:
            p_t = [jnp.where(mask, p, 0.0) for p in p_t]
        for s in range(n_streams):
            acc_ref[s] = (jnp.exp(m_old[s] - m_new[s]) * acc_ref[s]
                          + _mm(vt_ref[s // 2, j], _bf(p_t[s])))
            m_ref[s] = m_new[s]

    def block_before(j, carry):
        accumulate(j, scores(j), None)
        return carry

    def block_masked(j, carry):
        k_pos = j * tk + _iota2((tk, 1), 0)
        allowed = (k_pos >> shift) <= q_chunk
        late = jnp.maximum(k_pos - q_pos, 0).astype(F32)
        s_t = [jnp.where(allowed, s - (2.0 * slopes[idx // 2]) * late, NEG_BIG)
               for idx, s in enumerate(scores(j))]
        accumulate(j, s_t, allowed)
        return carry

    lax.fori_loop(0, n_before, block_before, 0)
    lax.fori_loop(n_before, n_blocks, block_masked, 0)
    pad = -tq % LANES
    gn = gn_ref[...]
    for g in range(heads):
        a0 = acc_ref[2 * g]
        a1 = acc_ref[2 * g + 1]
        o_t = (a0[:DV_D] / a0[DV_D:DV_D + 1] - lam * (a1[:DV_D] / a1[DV_D:DV_D + 1]))
        if pad:
            o_t = jnp.concatenate([o_t, jnp.zeros((DV_D, pad), F32)], axis=1)
        o = o_t.T[:tq]
        o_ref[:, g * DV_D:(g + 1) * DV_D] = (_rms(o, gn) * (1.0 - lam_init)).astype(BF16)


def diff_attention(zq, q_col, kd, k_col, vd, v_col, lam_params, norm_g, batch, seq, kv_len,
                   tq, tk, lam_init, heads):
    nq = seq // tq
    width = heads * DV_D
    return pl.pallas_call(
        functools.partial(_attn_kernel, seq=seq, kv_len=kv_len, tq=tq, tk=tk, lam_init=lam_init,
                          heads=heads),
        grid=(batch, H_D // heads, nq),
        in_specs=[pl.BlockSpec((tq, width), lambda b, h, i: (b * nq + i, q_col // heads + h)),
                  pl.BlockSpec((kv_len, width), lambda b, h, i: (b, k_col // heads + h)),
                  pl.BlockSpec((kv_len, width), lambda b, h, i: (b, v_col // heads + h)),
                  pl.BlockSpec((4, D_HD), lambda b, h, i: (0, 0)),
                  pl.BlockSpec((1, DV_D), lambda b, h, i: (0, 0))],
        out_specs=pl.BlockSpec((tq, width), lambda b, h, i: (b * nq + i, h)),
        out_shape=jax.ShapeDtypeStruct((batch * seq, H_D * DV_D), BF16),
        scratch_shapes=[pltpu.VMEM((2 * heads, kv_len, DV_D), BF16),
                        pltpu.VMEM((heads, kv_len // tk, DV_D + ONES_ROWS, tk), BF16),
                        pltpu.VMEM((2 * heads, 1, tq), F32),
                        pltpu.VMEM((2 * heads, DV_D + ONES_ROWS, tq), F32)],
        compiler_params=_params("arbitrary", "arbitrary", "arbitrary"),
        name="diff_attention",
    )(zq, kd, vd, lam_params, norm_g.reshape(1, DV_D))


def _pad_cols(w, total):
    return jnp.pad(w, ((0, 0), (0, total - w.shape[1])))


def _prep_w_in_ab(w):
    sizes = (H_A * DK_A, H_A * DK_A, H_A * DV_A, H_A * DV_A, H_B * DK_B, H_B * DK_B, H_B * DV_B,
             H_B, H_B, H_B * DV_B)
    aq, af, ai, ag, bq, bk, bv, big, bfg, bo = jnp.split(w, np_cumsum(sizes), axis=1)
    return _pad_cols(jnp.concatenate([aq, af, ai, ag, bq, bk, bv, bo, big, bfg], axis=1),
                     IN_AB_PAD).astype(BF16)


def _prep_w_in_cd(w):
    sizes = (H_C * (2 * DK_C + DV_C), H_C, H_C, H_C * DV_C, H_D * 2 * D_HD, H_D * 2 * D_HD,
             H_D * DV_D)
    cqkv, cb, ca, cz, dq, dk, dv = jnp.split(w, np_cumsum(sizes), axis=1)
    return _pad_cols(jnp.concatenate([cqkv, cz, dq, dk, dv, cb, ca], axis=1),
                     IN_CD_PAD).astype(BF16)


def np_cumsum(sizes):
    out, acc = [], 0
    for s in sizes[:-1]:
        acc += s
        out.append(acc)
    return out


def _tiles(n_rows, seq):
    tm = min(512, n_rows)
    tb = min(512, seq)
    return tm, tb


def _heads_per_step(n_heads, n_chunks, items=8):
    return max(1, min(n_heads, items // n_chunks))


def _layer_ab(x, st, p, lower_bounds, batch, seq, layer):
    n = batch * seq
    tm, tb = _tiles(n, seq)
    z = norm_matmul(x, p['norm_mix'], p['w_in'], tm, IN_AB_PAD // 3)
    n_chunks = tb // min(MIX_CHUNK, seq)
    o_a, s_new = hgrn2(z, lower_bounds, p['hgrn_norm'], st['hgrn'], batch, seq, tb, layer // 2,
                       _heads_per_step(H_A, n_chunks))
    o_b, c_new, n_new, m_new = mlstm(z, p['i_bias'], p['f_bias'], p['mlstm_norm'], st['mlstm_c'],
                                     st['mlstm_n'], st['mlstm_m'], batch, seq, tb,
                                     _heads_per_step(H_B, n_chunks))
    x = matmul_residual((o_a, o_b), p['w_out'], x, p['norm_ffn'], tm, False)
    return x, (s_new, c_new, n_new, m_new)


def _layer_cd(x, st, p, batch, seq, layer):
    n = batch * seq
    tm, tb = _tiles(n, seq)
    z = norm_matmul(x, p['norm_mix'], p['w_in'], tm, IN_CD_PAD // 3)
    tb_gdn = min(256, seq)
    heads = _heads_per_step(H_C, tb_gdn // min(CHUNK, seq), items=32)
    o_c, s_new = gdn(z, p['conv_w'], st['gdn_conv'], p['a_log'], p['dt_bias'], p['gdn_norm'],
                     st['gdn'], batch, seq, tb_gdn, heads)
    z3 = z.reshape(batch, seq, IN_CD_PAD)
    conv_new = z3[:, seq - (CONV_C - 1):, :H_C * (2 * DK_C + DV_C)]
    k_new = z3[:, :, CD_DK * LANES:CD_DV * LANES]
    v_new = z3[:, :, CD_DV * LANES:CD_GATES * LANES]
    past = st['k_cache'].shape[1]
    lam_init = 0.8 - 0.6 * math.exp(-0.3 * layer)
    if past == 0:
        o_d = diff_attention(z, CD_DQ, z, CD_DK, z, CD_DV, p['lam'], p['diff_norm'], batch, seq,
                             seq, min(512, seq), min(512, seq), lam_init, 2)
    else:
        kv_len = past + seq
        kd = jnp.concatenate([st['k_cache'].reshape(batch, past, -1), k_new], axis=1)
        vd = jnp.concatenate([st['v_cache'].reshape(batch, past, -1), v_new], axis=1)
        o_d = diff_attention(z, CD_DQ, kd.reshape(batch * kv_len, -1), 0,
                             vd.reshape(batch * kv_len, -1), 0, p['lam'], p['diff_norm'], batch,
                             seq, kv_len, seq, kv_len, lam_init, 4)
    x = matmul_residual((o_c, o_d), p['w_out'], x, p['norm_ffn'], tm, False)
    return x, (s_new, conv_new, k_new.reshape(batch, seq, H_D, 2 * D_HD),
               v_new.reshape(batch, seq, H_D, DV_D))


def _ffn(x, state, p, batch, seq, final_g):
    n = batch * seq
    tm, _ = _tiles(n, seq)
    act, conv_new = ffn_up(x, p['norm_ffn'], p['ffn_up'], p['ffn_conv_w'], p['ffn_conv_b'], state,
                           batch, seq, min(2 * tm, n), 512)
    g = p['norm_ffn'] if final_g is None else final_g
    x = matmul_residual((act,), p['ffn_down'], x, g, tm, final_g is not None)
    return x, conv_new


def _run_group(x, st, params, lower_bounds, norm_final):
    batch, seq, _ = x.shape
    x = x.reshape(batch * seq, D_MODEL)
    new_states = []
    for layer, (p, s) in enumerate(zip(params, st)):
        if layer % 2 == 0:
            x, new_mix = _layer_ab(x, s, p, lower_bounds, batch, seq, layer)
        else:
            x, new_mix = _layer_cd(x, s, p, batch, seq, layer)
        last = layer == len(params) - 1
        x, new_ffn = _ffn(x, s['ffn_conv'], p, batch, seq, norm_final if last else None)
        new_states.append(new_mix + (new_ffn,))
    return x.reshape(batch, seq, D_MODEL), new_states


def kernel(x_prompt, x_sample, state_hgrn_0, state_mlstm_c_0, state_mlstm_n_0, state_mlstm_m_0,
           state_ffn_conv_0, state_gdn_1, state_gdn_conv_1, cache_k_1, cache_v_1, state_ffn_conv_1,
           hgrn_lower_bounds, norm_mix_0, w_in_0, mlstm_i_bias_0, mlstm_f_bias_0, hgrn_norm_0,
           mlstm_norm_0, w_out_0, norm_ffn_0, ffn_up_0, ffn_conv_w_0, ffn_conv_b_0, ffn_down_0,
           norm_mix_1, w_in_1, gdn_conv_w_1, gdn_a_log_1, gdn_dt_bias_1, gdn_norm_1, diff_lambda_1,
           diff_norm_1, w_out_1, norm_ffn_1, ffn_up_1, ffn_conv_w_1, ffn_conv_b_1, ffn_down_1,
           norm_final):
    params = [
        dict(norm_mix=norm_mix_0, w_in=_prep_w_in_ab(w_in_0), i_bias=mlstm_i_bias_0,
             f_bias=mlstm_f_bias_0, hgrn_norm=hgrn_norm_0, mlstm_norm=mlstm_norm_0,
             w_out=w_out_0.astype(BF16), norm_ffn=norm_ffn_0, ffn_up=ffn_up_0.astype(BF16),
             ffn_conv_w=ffn_conv_w_0, ffn_conv_b=ffn_conv_b_0, ffn_down=ffn_down_0.astype(BF16)),
        dict(norm_mix=norm_mix_1, w_in=_prep_w_in_cd(w_in_1), conv_w=gdn_conv_w_1,
             a_log=gdn_a_log_1, dt_bias=gdn_dt_bias_1, gdn_norm=gdn_norm_1, lam=diff_lambda_1,
             diff_norm=diff_norm_1, w_out=w_out_1.astype(BF16), norm_ffn=norm_ffn_1,
             ffn_up=ffn_up_1.astype(BF16), ffn_conv_w=ffn_conv_w_1, ffn_conv_b=ffn_conv_b_1,
             ffn_down=ffn_down_1.astype(BF16)),
    ]
    bp = x_prompt.shape[0]
    st_prompt = [
        dict(hgrn=jnp.zeros((bp, H_A, DK_A, DV_A), F32), mlstm_c=jnp.zeros((bp, H_B, DK_B, DV_B), F32),
             mlstm_n=jnp.zeros((bp, H_B, DK_B), F32), mlstm_m=jnp.full((bp, H_B), NEG_BIG, F32),
             ffn_conv=jnp.zeros((bp, CONV_FFN - 1, D_FF), F32)),
        dict(gdn=jnp.zeros((bp, H_C, DK_C, DV_C), F32),
             gdn_conv=jnp.zeros((bp, CONV_C - 1, H_C * (2 * DK_C + DV_C)), F32),
             k_cache=jnp.zeros((bp, 0, H_D, 2 * D_HD), F32), v_cache=jnp.zeros((bp, 0, H_D, DV_D), F32),
             ffn_conv=jnp.zeros((bp, CONV_FFN - 1, D_FF), F32)),
    ]
    st_sample = [
        dict(hgrn=state_hgrn_0, mlstm_c=state_mlstm_c_0, mlstm_n=state_mlstm_n_0,
             mlstm_m=state_mlstm_m_0, ffn_conv=state_ffn_conv_0),
        dict(gdn=state_gdn_1, gdn_conv=state_gdn_conv_1, k_cache=cache_k_1, v_cache=cache_v_1,
             ffn_conv=state_ffn_conv_1),
    ]
    y_p, new_p = _run_group(x_prompt, st_prompt, params, hgrn_lower_bounds, norm_final)
    y_s, new_s = _run_group(x_sample, st_sample, params, hgrn_lower_bounds, norm_final)
    (hgrn_p, c_p, n_p, m_p, f0_p), (gdn_p, gc_p, k_p, v_p, f1_p) = new_p
    (hgrn_s, c_s, n_s, m_s, f0_s), (gdn_s, gc_s, k_s, v_s, f1_s) = new_s
    return (y_p, y_s, hgrn_p, hgrn_s, c_p, c_s, n_p, n_s, m_p, m_s, f0_p, f0_s, gdn_p, gdn_s,
            gc_p, gc_s, k_p, k_s, v_p, v_s, f1_p, f1_s)
```

```python
import functools
import math

import jax
import jax.numpy as jnp
from jax import lax
from jax.experimental import pallas as pl
from jax.experimental.pallas import tpu as pltpu

F32 = jnp.float32
BF16 = jnp.bfloat16
HI = lax.Precision.HIGHEST

D_MODEL = 2048
CHUNK = 64
H_A, DK_A, DV_A = 8, 128, 128
H_B, DK_B, DV_B = 4, 128, 256
H_C, DK_C, DV_C = 8, 128, 128
CONV_C = 4
H_D, D_HD = 8, 64
DV_D = 2 * D_HD
D_FF = 5632
CONV_FFN = 3
EPS = 1e-6
NEG_BIG = -1e30

LANES = 128
SUBLANES = 8
ONES_ROWS = 2 * SUBLANES
VMEM_LIMIT = 56 * 1024 * 1024
MIX_CHUNK = 128

AB_Q, AB_F, AB_I, AB_G = 0, 8, 16, 24
AB_BQ, AB_BK = 32, 36
AB_BV, AB_BO = 20, 24
AB_GATES = 56
IN_AB_PAD = 57 * LANES
CD_Q, CD_K, CD_V, CD_Z = 0, 8, 16, 24
CD_DQ, CD_DK, CD_DV = 32, 40, 48
CD_GATES = 56
IN_CD_PAD = 57 * LANES


def _mm(a, b, prec=None):
    return lax.dot_general(a, b, (((1,), (0,)), ((), ())), precision=prec,
                           preferred_element_type=F32)


def _mm_nt(a, b, prec=None):
    return lax.dot_general(a, b, (((1,), (1,)), ((), ())), precision=prec,
                           preferred_element_type=F32)


def _mm_tn(a, b, prec=None):
    return lax.dot_general(a, b, (((0,), (0,)), ((), ())), precision=prec,
                           preferred_element_type=F32)


def _sigmoid(x):
    return 1.0 / (1.0 + jnp.exp(-x))


def _silu(x):
    return x * _sigmoid(x)


def _softplus(x):
    return jnp.maximum(x, 0.0) + jnp.log1p(jnp.exp(-jnp.abs(x)))


def _rms(x, g):
    return x * lax.rsqrt(jnp.mean(x * x, axis=-1, keepdims=True) + EPS) * g


def _params(*sem):
    return pltpu.CompilerParams(dimension_semantics=sem, vmem_limit_bytes=VMEM_LIMIT)


def _norm_matmul_kernel(x_ref, g_ref, w_ref, o_ref, h_ref):
    @pl.when(pl.program_id(1) == 0)
    def _():
        h_ref[...] = _rms(x_ref[...], g_ref[...]).astype(BF16)

    o_ref[...] = _mm(h_ref[...], w_ref[...])


def norm_matmul(x, g, w, tm, tn):
    n, k = x.shape
    m = w.shape[1]
    return pl.pallas_call(
        _norm_matmul_kernel,
        grid=(n // tm, m // tn),
        in_specs=[pl.BlockSpec((tm, k), lambda i, j: (i, 0)),
                  pl.BlockSpec((1, k), lambda i, j: (0, 0)),
                  pl.BlockSpec((k, tn), lambda i, j: (0, j))],
        out_specs=pl.BlockSpec((tm, tn), lambda i, j: (i, j)),
        out_shape=jax.ShapeDtypeStruct((n, m), F32),
        scratch_shapes=[pltpu.VMEM((tm, k), BF16)],
        compiler_params=_params("arbitrary", "arbitrary"),
        name="norm_matmul",
    )(x, g.reshape(1, k), w)


def _matmul_res_kernel(*refs, n_parts, final_norm):
    y_refs = refs[:n_parts]
    w_ref, x_ref, g_ref, o_ref = refs[n_parts:]
    k_part = y_refs[0].shape[1]
    acc = x_ref[...]
    for part, y_ref in enumerate(y_refs):
        acc = acc + _mm(y_ref[...], w_ref[part * k_part:(part + 1) * k_part, :])
    o_ref[...] = _rms(acc, g_ref[...]) if final_norm else acc


def matmul_residual(ys, w, x, g, tm, final_norm):
    n, k_part = ys[0].shape
    k, d = w.shape
    n_parts = len(ys)
    return pl.pallas_call(
        functools.partial(_matmul_res_kernel, n_parts=n_parts, final_norm=final_norm),
        grid=(n // tm,),
        in_specs=[pl.BlockSpec((tm, k_part), lambda i: (i, 0)) for _ in range(n_parts)]
        + [pl.BlockSpec((k, d), lambda i: (0, 0), pipeline_mode=pl.Buffered(1)),
           pl.BlockSpec((tm, d), lambda i: (i, 0)),
           pl.BlockSpec((1, d), lambda i: (0, 0))],
        out_specs=pl.BlockSpec((tm, d), lambda i: (i, 0)),
        out_shape=jax.ShapeDtypeStruct((n, d), F32),
        compiler_params=_params("arbitrary"),
        name="matmul_residual",
    )(*ys, w, x, g.reshape(1, d))


def _ffn_up_kernel(x_ref, g_ref, wa_ref, wu_ref, cw_ref, cb_ref, st_ref, act_ref, tail_ref,
                   h_ref, ext_ref, carry_ref, *, nseq, rows, tiles_per_seq):
    i = pl.program_id(0)
    j = pl.program_id(1)
    halo = CONV_FFN - 1

    @pl.when(j == 0)
    def _():
        h_ref[...] = _rms(x_ref[...], g_ref[...]).astype(BF16)

    if tiles_per_seq > 1:
        @pl.when(i == 0)
        def _():
            carry_ref[j] = jnp.zeros(carry_ref.shape[1:], F32)

    h = h_ref[...]
    a = _mm(h, _bf(wa_ref[...]))
    u = _mm(h, _bf(wu_ref[...]))
    cw = cw_ref[...]
    cb = cb_ref[...]
    for s in range(nseq):
        lo = s * rows
        ext_ref[s, SUBLANES:, :] = a[lo:lo + rows]
        prev = st_ref[s]
        if tiles_per_seq > 1:
            prev = jnp.where((i % tiles_per_seq) == 0, prev, carry_ref[j, SUBLANES - halo:, :])
        ext_ref[s, SUBLANES - halo:SUBLANES, :] = prev
        conv = cb
        for t in range(CONV_FFN):
            off = SUBLANES - halo + t
            conv = conv + ext_ref[s, off:off + rows, :] * cw[t:t + 1]
        act_ref[lo:lo + rows, :] = (_silu(conv) * u[lo:lo + rows]).astype(BF16)
        tail_ref[s] = ext_ref[s, SUBLANES + rows - halo:SUBLANES + rows, :]
        carry_ref[j] = ext_ref[s, rows:rows + SUBLANES, :]


def ffn_up(x, g, w_up, conv_w, conv_b, state, batch, seq, tm, tf):
    n, k = x.shape
    nff = D_FF // tf
    if tm >= seq:
        nseq, rows, tiles_per_seq = tm // seq, seq, 1
    else:
        nseq, rows, tiles_per_seq = 1, tm, seq // tm
    ntiles = n // tm
    halo = CONV_FFN - 1
    act, tail = pl.pallas_call(
        functools.partial(_ffn_up_kernel, nseq=nseq, rows=rows, tiles_per_seq=tiles_per_seq),
        grid=(ntiles, nff),
        in_specs=[pl.BlockSpec((tm, k), lambda i, j: (i, 0)),
                  pl.BlockSpec((1, k), lambda i, j: (0, 0)),
                  pl.BlockSpec((k, tf), lambda i, j: (0, j)),
                  pl.BlockSpec((k, tf), lambda i, j: (0, j + nff)),
                  pl.BlockSpec((CONV_FFN, tf), lambda i, j: (0, j)),
                  pl.BlockSpec((1, tf), lambda i, j: (0, j)),
                  pl.BlockSpec((nseq, halo, tf), lambda i, j: (i // tiles_per_seq, 0, j))],
        out_specs=[pl.BlockSpec((tm, tf), lambda i, j: (i, j)),
                   pl.BlockSpec((nseq, halo, tf), lambda i, j: (i, 0, j))],
        out_shape=[jax.ShapeDtypeStruct((n, D_FF), BF16),
                   jax.ShapeDtypeStruct((ntiles * nseq, halo, D_FF), F32)],
        scratch_shapes=[pltpu.VMEM((tm, k), BF16),
                        pltpu.VMEM((nseq, SUBLANES + rows, tf), F32),
                        pltpu.VMEM((nff, SUBLANES, tf), F32)],
        compiler_params=_params("arbitrary", "arbitrary"),
        name="ffn_up",
    )(x, g.reshape(1, k), w_up, w_up, conv_w, conv_b.reshape(1, D_FF), state)
    new_state = tail.reshape(batch, -1, halo, D_FF)[:, -1]
    return act, new_state


def _iota2(shape, dim):
    return lax.broadcasted_iota(jnp.int32, shape, dim)


def _tril(n):
    return (_iota2((n, n), 1) <= _iota2((n, n), 0)).astype(F32)


def _lane_col(x, lane):
    return jnp.sum(jnp.where(_iota2(x.shape, 1) == lane, x, 0.0), axis=-1, keepdims=True)


def _bf(x):
    return x.astype(BF16)


def _split3(x):
    h1 = _bf(x)
    r1 = x - h1.astype(F32)
    h2 = _bf(r1)
    h3 = _bf(r1 - h2.astype(F32))
    return h1, h2, h3


def _mm_sel(sel, x):
    n = x.shape[1]
    y = _mm(_bf(sel), jnp.concatenate(_split3(x), axis=1))
    return y[:, :n] + y[:, n:2 * n] + y[:, 2 * n:]


def _transpose_rows(x):
    r = x.shape[0]
    pad = -r % LANES
    if pad:
        x = jnp.concatenate([x, jnp.zeros((pad, x.shape[1]), x.dtype)], axis=0)
    return x.T


def _chunk_cumsum_rows(row, tri):
    r = row.shape[1]
    h1, h2, h3 = (p.astype(F32) for p in _split3(row))
    sub = _iota2((2 * SUBLANES, r), 0)
    parts = jnp.where(sub == 0, h1, jnp.where(sub == 1, h2, jnp.where(sub == 2, h3, 0.0)))
    y = _mm(_bf(parts), tri)
    return y[0:1] + y[1:2] + y[2:3]


def _row_to_col(row):
    n = row.shape[1]
    eye = _iota2((n, n), 0) == _iota2((n, n), 1)
    return jnp.sum(jnp.where(eye, row, 0.0), axis=-1, keepdims=True)


def _chunk_tri(tb, chunk):
    s = lax.broadcasted_iota(jnp.int32, (tb, tb), 0)
    t = lax.broadcasted_iota(jnp.int32, (tb, tb), 1)
    return ((s <= t) & (s // chunk == t // chunk)).astype(BF16)


def _hgrn_kernel(q_ref, f_ref, i_ref, g_ref, lb_ref, gn_ref, s0_ref, o_ref, s_out_ref, s_ref,
                 *, L, n_chunks, lb_index, heads):
    tb = pl.program_id(2)

    @pl.when(tb == 0)
    def _():
        s_ref[...] = s0_ref[0]

    lbp = lb_ref[...]
    e = jnp.exp(lbp - jnp.max(lbp, axis=0, keepdims=True))
    lb_all = jnp.sum(e[:lb_index + 1], axis=0, keepdims=True) / jnp.sum(e, axis=0, keepdims=True)

    sizes = [L >> l for l in range(int(math.log2(L)))]
    ti = _iota2((L, L), 0)
    si = _iota2((L, L), 1)
    tcol = _iota2((L, 1), 0)
    mats = [si <= ti]
    for sz in sizes:
        mats.append(si <= ((ti & ~(sz - 1)) + (sz // 2 - 1)))
    cum_mat = jnp.concatenate([m.astype(F32) for m in mats], axis=0)
    eye = ti == si
    gn = gn_ref[...]

    chunks = []
    for c in range(n_chunks):
        rows = pl.ds(c * L, L)
        for g in range(heads):
            lanes = slice(g * LANES, (g + 1) * LANES)
            lb = lb_all[:, lanes]
            q = q_ref[rows, lanes]
            f = lb + (1.0 - lb) * _sigmoid(f_ref[rows, lanes])
            k = 1.0 - f
            cums = _mm_sel(cum_mat, jnp.log(f))
            chunks.append(dict(
                rows=rows, lanes=lanes, g=g, q=q, k=k, cums=cums, b=cums[:L],
                vb=_bf(i_ref[rows, lanes]),
                scores=jnp.where(eye, jnp.sum(q * k, axis=-1, keepdims=True), 0.0)))
    for l, sz in enumerate(sizes):
        right = (tcol & (sz - 1)) >= sz // 2
        same = (ti & ~(sz - 1)) == (si & ~(sz - 1))
        for ch in chunks:
            w = jnp.exp(-jnp.abs(ch['b'] - ch['cums'][(l + 1) * L:(l + 2) * L]))
            qm = jnp.where(right, ch['q'] * w, 0.0)
            km = jnp.where(right, 0.0, ch['k'] * w)
            ch['scores'] = ch['scores'] + jnp.where(same, _mm_nt(_bf(qm), _bf(km)), 0.0)
    for ch in chunks:
        bl = ch['b'][L - 1:L]
        ch['update'] = _mm_tn(_bf(ch['k'] * jnp.exp(bl - ch['b'])), ch['vb'])
        ch['decay'] = _row_to_col(jnp.exp(bl))
    for ch in chunks:
        s = s_ref[ch['g']]
        o = _mm(jnp.concatenate([_bf(ch['q'] * jnp.exp(ch['b'])), _bf(ch['scores'])], axis=1),
                jnp.concatenate([_bf(s), ch['vb']], axis=0))
        s_ref[ch['g']] = ch['decay'] * s + ch['update']
        o_ref[ch['rows'], ch['lanes']] = (
            _rms(o, gn) * _silu(g_ref[ch['rows'], ch['lanes']])).astype(BF16)

    @pl.when(tb == pl.num_programs(2) - 1)
    def _():
        s_out_ref[0] = s_ref[...]


def hgrn2(z, lower_bounds, norm_g, s0, batch, seq, tb, lb_index, heads):
    L = min(MIX_CHUNK, seq)
    nt = seq // tb
    width = heads * LANES

    def col(c0):
        return pl.BlockSpec((tb, width), lambda b, h, t: (b * nt + t, c0 // heads + h))

    st_spec = pl.BlockSpec((1, heads, DK_A, DV_A), lambda b, h, t: (b, h, 0, 0))
    return pl.pallas_call(
        functools.partial(_hgrn_kernel, L=L, n_chunks=tb // L, lb_index=lb_index, heads=heads),
        grid=(batch, H_A // heads, nt),
        in_specs=[col(AB_Q), col(AB_F), col(AB_I), col(AB_G),
                  pl.BlockSpec((lower_bounds.shape[0], width), lambda b, h, t: (0, h)),
                  pl.BlockSpec((1, DV_A), lambda b, h, t: (0, 0)),
                  st_spec],
        out_specs=[pl.BlockSpec((tb, width), lambda b, h, t: (b * nt + t, h)), st_spec],
        out_shape=[jax.ShapeDtypeStruct((batch * seq, H_A * DV_A), BF16),
                   jax.ShapeDtypeStruct((batch, H_A, DK_A, DV_A), F32)],
        scratch_shapes=[pltpu.VMEM((heads, DK_A, DV_A), F32)],
        compiler_params=_params("arbitrary", "arbitrary", "arbitrary"),
        name="hgrn2",
    )(z, z, z, z, lower_bounds, norm_g.reshape(1, DV_A), s0)


def _mlstm_kernel(q_ref, k_ref, v_ref, og_ref, gt_ref, bias_ref, tri_ref, gn_ref, c0_ref, n0_ref,
                  m0_ref, o_ref, c_out_ref, n_out_ref, m_out_ref, c_ref, n_ref, m_ref, gtt_ref,
                  *, L, n_chunks, heads):
    h0 = pl.program_id(1) * heads
    tb = pl.program_id(2)

    @pl.when(tb == 0)
    def _():
        c_ref[...] = c0_ref[0]
        n_ref[...] = n0_ref[0]
        m_ref[...] = m0_ref[0]

    n_rows = L * n_chunks
    gtt_ref[...] = _transpose_rows(gt_ref[...] + bias_ref[...])
    ig_rows, b_rows = [], []
    for g in range(heads):
        ig_rows.append(gtt_ref[pl.ds(h0 + g, 1), :][:, :n_rows])
        f_rows = gtt_ref[pl.ds(H_B + h0 + g, 1), :][:, :n_rows]
        logf = jnp.minimum(f_rows, 0.0) - jnp.log1p(jnp.exp(-jnp.abs(f_rows)))
        b_rows.append(_chunk_cumsum_rows(logf, tri_ref[...]))

    causal = _iota2((L, L), 1) <= _iota2((L, L), 0)
    gn = gn_ref[...]
    scale = DK_B ** -0.5

    chunks = []
    for c in range(n_chunks):
        rows = pl.ds(c * L, L)
        for g in range(heads):
            q = q_ref[rows, g * DK_B:(g + 1) * DK_B] * scale
            k = k_ref[rows, g * DK_B:(g + 1) * DK_B]
            vb = _bf(v_ref[rows, g * DV_B:(g + 1) * DV_B])
            ig_row = ig_rows[g][:, c * L:(c + 1) * L]
            b_row = b_rows[g][:, c * L:(c + 1) * L]
            b_col = _row_to_col(b_row)
            d = b_col - b_row + ig_row
            d_max = jnp.max(jnp.where(causal, d, -jnp.inf), axis=-1, keepdims=True)
            p = jnp.exp(jnp.where(causal, d - d_max, -jnp.inf)) * _mm_nt(_bf(q), _bf(k))
            b_last = b_row[:, L - 1:L]
            wk_max = jnp.max(b_last - b_row + ig_row, axis=-1, keepdims=True)
            ks = jnp.exp(b_last - b_col + _row_to_col(ig_row) - wk_max) * k
            chunks.append(dict(
                rows=rows, g=g, q=q, b_col=b_col, d_max=d_max, b_last=b_last, wk_max=wk_max,
                p_sum=jnp.sum(p, axis=-1, keepdims=True), pv=_mm(_bf(p), vb),
                k_sum=jnp.sum(ks, axis=0, keepdims=True), kv=_mm_tn(_bf(ks), vb)))

    for ch in chunks:
        g = ch['g']
        q = ch['q']
        cs = c_ref[g]
        n = n_ref[g]
        m_prev = m_ref[g][:, 0:1]
        inter = ch['b_col'] + m_prev
        m_t = jnp.maximum(ch['d_max'], inter)
        w_intra = jnp.exp(ch['d_max'] - m_t)
        w_inter = jnp.exp(inter - m_t)
        num = w_intra * ch['pv'] + w_inter * _mm(_bf(q), _bf(cs))
        den = w_intra * ch['p_sum'] + w_inter * jnp.sum(q * n, axis=-1, keepdims=True)
        hid = num / jnp.maximum(jnp.abs(den), jnp.exp(-m_t))
        m_new = jnp.maximum(ch['b_last'] + m_prev, ch['wk_max'])
        w_new = jnp.exp(ch['wk_max'] - m_new)
        carry_w = jnp.exp(ch['b_last'] + m_prev - m_new)
        c_ref[g] = carry_w * cs + w_new * ch['kv']
        n_ref[g] = carry_w * n + w_new * ch['k_sum']
        m_ref[g] = jnp.broadcast_to(m_new, (1, LANES))
        lanes = slice(g * DV_B, (g + 1) * DV_B)
        o_ref[ch['rows'], lanes] = (
            _rms(hid, gn) * _sigmoid(og_ref[ch['rows'], lanes])).astype(BF16)

    @pl.when(tb == pl.num_programs(2) - 1)
    def _():
        c_out_ref[0] = c_ref[...]
        n_out_ref[0] = n_ref[...]
        m_out_ref[0] = m_ref[...]


def mlstm(z, i_bias, f_bias, norm_g, c0, n0, m0, batch, seq, tb, heads):
    L = min(MIX_CHUNK, seq)
    nt = seq // tb
    bias = jnp.zeros((1, LANES), F32).at[0, :H_B].set(i_bias).at[0, H_B:2 * H_B].set(f_bias)

    def col(c0_, width):
        return pl.BlockSpec((tb, heads * width), lambda b, h, t: (b * nt + t, c0_ // heads + h))

    def state(r, c):
        return pl.BlockSpec((1, heads, r, c), lambda b, h, t: (b, h, 0, 0))

    o, cs, n, m = pl.pallas_call(
        functools.partial(_mlstm_kernel, L=L, n_chunks=tb // L, heads=heads),
        grid=(batch, H_B // heads, nt),
        in_specs=[col(AB_BQ, DK_B), col(AB_BK, DK_B), col(AB_BV, DV_B), col(AB_BO, DV_B),
                  pl.BlockSpec((tb, LANES), lambda b, h, t: (b * nt + t, AB_GATES)),
                  pl.BlockSpec((1, LANES), lambda b, h, t: (0, 0)),
                  pl.BlockSpec((tb, tb), lambda b, h, t: (0, 0)),
                  pl.BlockSpec((1, DV_B), lambda b, h, t: (0, 0)),
                  state(DK_B, DV_B), state(1, DK_B), state(1, LANES)],
        out_specs=[pl.BlockSpec((tb, heads * DV_B), lambda b, h, t: (b * nt + t, h)),
                   state(DK_B, DV_B), state(1, DK_B), state(1, LANES)],
        out_shape=[jax.ShapeDtypeStruct((batch * seq, H_B * DV_B), BF16),
                   jax.ShapeDtypeStruct((batch, H_B, DK_B, DV_B), F32),
                   jax.ShapeDtypeStruct((batch, H_B, 1, DK_B), F32),
                   jax.ShapeDtypeStruct((batch, H_B, 1, LANES), F32)],
        scratch_shapes=[pltpu.VMEM((heads, DK_B, DV_B), F32), pltpu.VMEM((heads, 1, DK_B), F32),
                        pltpu.VMEM((heads, 1, LANES), F32),
                        pltpu.VMEM((LANES, tb + (-tb % LANES)), F32)],
        compiler_params=_params("arbitrary", "arbitrary", "arbitrary"),
        name="mlstm",
    )(z, z, z, z, z, bias, _chunk_tri(tb, L), norm_g.reshape(1, DV_B), c0,
      n0.reshape(batch, H_B, 1, DK_B),
      jnp.broadcast_to(m0.reshape(batch, H_B, 1, 1), (batch, H_B, 1, LANES)))
    return o, cs, n.reshape(batch, H_B, DK_B), m[:, :, 0, 0]


def _gdn_kernel(q_ref, k_ref, v_ref, z_ref, gt_ref, cw_ref, cs_ref, gp_ref, tri_ref, gn_ref, s0_ref,
                o_ref, s_out_ref, st_ref, ext_ref, cv_ref, lgt_ref, *, L, n_chunks, tb, heads):
    h0 = pl.program_id(1) * heads
    t = pl.program_id(2)
    halo = CONV_C - 1

    @pl.when(t == 0)
    def _():
        st_ref[...] = s0_ref[0]
        for p in range(3):
            ext_ref[p, SUBLANES - halo:SUBLANES, :] = cs_ref[p, 0]

    srcs = (q_ref, k_ref, v_ref)
    for p in range(3):
        ext_ref[p, SUBLANES:, :] = srcs[p][...]
        cw = cw_ref[p]
        acc = None
        for j in range(CONV_C):
            off = SUBLANES - halo + j
            term = ext_ref[p, off:off + tb, :] * cw[j:j + 1]
            acc = term if acc is None else acc + term
        cv_ref[p] = _silu(acc)
    gp = gp_ref[...]
    gn = gn_ref[...]

    lgt_ref[...] = _transpose_rows(-jnp.exp(gp[0:1]) * _softplus(gt_ref[...] + gp[1:2]))
    b_rows = [_chunk_cumsum_rows(lgt_ref[pl.ds(H_C + h0 + g, 1), :][:, :tb], tri_ref[...])
              for g in range(heads)]

    def l2n(x):
        return x * lax.rsqrt(jnp.sum(x * x, axis=-1, keepdims=True) + EPS)

    grp = 2 if n_chunks % 2 == 0 else 1
    R = grp * L
    ti = _iota2((R, R), 0)
    si = _iota2((R, R), 1)
    shift = int(math.log2(L))
    same = (ti >> shift) == (si >> shift)
    incl = (si <= ti) & same
    strict = (si < ti) & same

    groups = []
    for gi in range(n_chunks // grp):
        rows = pl.ds(gi * R, R)
        sig = _sigmoid(gt_ref[rows, :])
        for g in range(heads):
            lanes = slice(g * LANES, (g + 1) * LANES)
            q = l2n(cv_ref[0, rows, lanes]) * (DK_C ** -0.5)
            k = l2n(cv_ref[1, rows, lanes])
            v = cv_ref[2, rows, lanes]
            beta = _lane_col(sig, h0 + g)
            b_row = b_rows[g][:, gi * R:(gi + 1) * R]
            b_col = _row_to_col(b_row)
            dec_incl = jnp.exp(jnp.where(incl, b_col - b_row, -jnp.inf))
            eb = jnp.exp(b_col)
            kb = _bf(k)
            qb = _bf(q)
            kq = _mm_nt(jnp.concatenate([kb, qb], axis=0), kb)
            subs = []
            for s in range(grp):
                sub = slice(s * L, (s + 1) * L)
                b_last = b_row[:, (s + 1) * L - 1:(s + 1) * L]
                subs.append(dict(sub=sub, decay=jnp.exp(b_last),
                                 k_dec=_bf(k[sub] * jnp.exp(b_last - b_col[sub]))))
            groups.append(dict(
                rows=rows, lanes=lanes, g=g, qb=qb, eb=eb, subs=subs,
                pw=-(beta * kq[:R] * jnp.where(strict, dec_incl, 0.0)),
                attn=_bf(kq[R:] * dec_incl),
                x=jnp.concatenate([beta * v, (beta * eb) * k], axis=1)))

    eye = jnp.where(ti == si, 1.0, 0.0)
    n_steps = shift
    for gr in groups:
        gr['inv'] = eye + gr['pw']
        pwb = _bf(gr['pw'])
        gr['pw'] = _mm(pwb, pwb)
    for step in range(1, n_steps):
        for gr in groups:
            inv, pw = gr['inv'], gr['pw']
            if step == n_steps - 1:
                gr['inv'] = inv + _mm(_bf(pw), _bf(inv))
            else:
                y = _mm(_bf(pw), jnp.concatenate([_bf(inv), _bf(pw)], axis=1))
                gr['inv'] = inv + y[:, :R]
                gr['pw'] = y[:, R:]
    for gr in groups:
        gr['x'] = _mm(_bf(gr['inv']), _bf(gr['x']))

    for gr in groups:
        deltas, qs = [], []
        for sb in gr['subs']:
            sub = sb['sub']
            w = gr['x'][sub, :DV_C]
            u = gr['x'][sub, DV_C:]
            st = st_ref[gr['g']]
            uq = _mm_nt(jnp.concatenate([_bf(u), gr['qb'][sub]], axis=0), _bf(st))
            db = _bf(w - uq[:L])
            st_ref[gr['g']] = sb['decay'] * st + _mm_tn(db, sb['k_dec'])
            deltas.append(db)
            qs.append(uq[L:])
        o = (gr['eb'] * jnp.concatenate(qs, axis=0)
             + _mm(gr['attn'], jnp.concatenate(deltas, axis=0)))
        o_ref[gr['rows'], gr['lanes']] = (
            _rms(o, gn) * _silu(z_ref[gr['rows'], gr['lanes']])).astype(BF16)

    for p in range(3):
        ext_ref[p, 0:SUBLANES, :] = ext_ref[p, tb:tb + SUBLANES, :]

    @pl.when(t == pl.num_programs(2) - 1)
    def _():
        s_out_ref[0] = st_ref[...]


def gdn(z, conv_w, conv_state, a_log, dt_bias, norm_g, s0, batch, seq, tb, heads):
    L = min(CHUNK, seq)
    nt = seq // tb
    halo = CONV_C - 1
    gate_params = (jnp.zeros((2, LANES), F32).at[0, H_C:2 * H_C].set(a_log)
                   .at[1, H_C:2 * H_C].set(dt_bias))
    cw = conv_w.reshape(CONV_C, 3, H_C * DK_C).transpose(1, 0, 2)
    cs = conv_state.reshape(batch, halo, 3, H_C * DK_C).transpose(2, 0, 1, 3)

    width = heads * LANES

    def col(c0):
        return pl.BlockSpec((tb, width), lambda b, h, t: (b * nt + t, c0 // heads + h))

    st_spec = pl.BlockSpec((1, heads, DV_C, DK_C), lambda b, h, t: (b, h, 0, 0))
    o, st = pl.pallas_call(
        functools.partial(_gdn_kernel, L=L, n_chunks=tb // L, tb=tb, heads=heads),
        grid=(batch, H_C // heads, nt),
        in_specs=[col(CD_Q), col(CD_K), col(CD_V), col(CD_Z),
                  pl.BlockSpec((tb, LANES), lambda b, h, t: (b * nt + t, CD_GATES)),
                  pl.BlockSpec((3, CONV_C, width), lambda b, h, t: (0, 0, h)),
                  pl.BlockSpec((3, 1, halo, width), lambda b, h, t: (0, b, 0, h)),
                  pl.BlockSpec((2, LANES), lambda b, h, t: (0, 0)),
                  pl.BlockSpec((tb, tb), lambda b, h, t: (0, 0)),
                  pl.BlockSpec((1, DV_C), lambda b, h, t: (0, 0)),
                  st_spec],
        out_specs=[pl.BlockSpec((tb, width), lambda b, h, t: (b * nt + t, h)), st_spec],
        out_shape=[jax.ShapeDtypeStruct((batch * seq, H_C * DV_C), BF16),
                   jax.ShapeDtypeStruct((batch, H_C, DV_C, DK_C), F32)],
        scratch_shapes=[pltpu.VMEM((heads, DV_C, DK_C), F32),
                        pltpu.VMEM((3, SUBLANES + tb, width), F32),
                        pltpu.VMEM((3, tb, width), F32),
                        pltpu.VMEM((LANES, tb + (-tb % LANES)), F32)],
        compiler_params=_params("arbitrary", "arbitrary", "arbitrary"),
        name="gdn",
    )(z, z, z, z, z, cw, cs, gate_params, _chunk_tri(tb, L), norm_g.reshape(1, DV_C),
      jnp.swapaxes(s0, 2, 3))
    return o, jnp.swapaxes(st, 2, 3)


def _attn_kernel(q_ref, k_ref, v_ref, lp_ref, gn_ref, o_ref, ka_ref, vt_ref, m_ref, acc_ref,
                 *, seq, kv_len, tq, tk, lam_init, heads):
    h0 = pl.program_id(1) * heads
    qi = pl.program_id(2)
    past = kv_len - seq
    shift = int(math.log2(CHUNK))
    n_kb = kv_len // tk
    n_streams = 2 * heads

    @pl.when(qi == 0)
    def _():
        lane = _iota2((kv_len, 2 * D_HD), 1)
        k_idx = _iota2((kv_len, 1), 0)
        chunk_f = (k_idx >> shift).astype(F32)
        rem_f = (k_idx & (CHUNK - 1)).astype(F32)

        def pos_lanes(base):
            return jnp.where(lane == base, chunk_f, jnp.where(lane == base + 1, rem_f, 0.0))

        for g in range(heads):
            k = k_ref[:, g * DV_D:(g + 1) * DV_D]
            ka_ref[2 * g] = _bf(jnp.where(lane < D_HD, k, pos_lanes(D_HD)))
            ka_ref[2 * g + 1] = _bf(jnp.where(lane >= D_HD, k, pos_lanes(0)))
            vt = _transpose_rows(v_ref[:, g * DV_D:(g + 1) * DV_D])
            for jb in range(n_kb):
                vt_ref[g, jb, :DV_D, :] = _bf(vt[:, jb * tk:(jb + 1) * tk])
                vt_ref[g, jb, DV_D:, :] = jnp.ones((ONES_ROWS, tk), BF16)

    lp = lp_ref[...]
    lam = (jnp.exp(jnp.sum(lp[0:1] * lp[1:2], axis=-1, keepdims=True))
           - jnp.exp(jnp.sum(lp[2:3] * lp[3:4], axis=-1, keepdims=True)) + lam_init)

    lane_q = _iota2((tq, 2 * D_HD), 1)
    qa = []
    slopes = []
    for g in range(heads):
        hf = jnp.full((1, 1), h0 + g + 1, jnp.int32).astype(F32)
        slope = jnp.exp(hf * (-8.0 / H_D * math.log(2.0)))
        slopes.append(slope)
        q = q_ref[:, g * DV_D:(g + 1) * DV_D] * (D_HD ** -0.5)

        def slope_lanes(base, slope=slope):
            return jnp.where(lane_q == base, slope * CHUNK,
                             jnp.where(lane_q == base + 1, slope, 0.0))

        qa.append(_bf(jnp.where(lane_q < D_HD, q, slope_lanes(D_HD))))
        qa.append(_bf(jnp.where(lane_q >= D_HD, q, slope_lanes(0))))
    q_start = past + qi * tq
    q_pos = q_start + _iota2((1, tq), 1)
    q_chunk = q_pos >> shift

    m_ref[...] = jnp.full(m_ref.shape, NEG_BIG, F32)
    acc_ref[...] = jnp.zeros(acc_ref.shape, F32)

    last_allowed = (((q_start + tq - 1) >> shift) << shift) + (CHUNK - 1)
    n_blocks = jnp.minimum(last_allowed, kv_len - 1) // tk + 1
    n_before = jnp.minimum((q_start + 1) // tk, n_blocks)

    def scores(j):
        rows = pl.ds(pl.multiple_of(j * tk, 2 * SUBLANES), tk)
        return [_mm_nt(ka_ref[s, rows, :], qa[s]) for s in range(n_streams)]

    def accumulate(j, s_t, mask):
        m_old = [m_ref[s] for s in range(n_streams)]
        m_new = [jnp.maximum(m_old[s], jnp.max(s_t[s], axis=0, keepdims=True))
                 for s in range(n_streams)]
        p_t = [jnp.exp(s_t[s] - m_new[s]) for s in range(n_streams)]
        if mask is not None:
            p_t = [jnp.where(mask, p, 0.0) for p in p_t]
        for s in range(n_streams):
            acc_ref[s] = (jnp.exp(m_old[s] - m_new[s]) * acc_ref[s]
                          + _mm(vt_ref[s // 2, j], _bf(p_t[s])))
            m_ref[s] = m_new[s]

    def block_before(j, carry):
        accumulate(j, scores(j), None)
        return carry

    def block_masked(j, carry):
        k_pos = j * tk + _iota2((tk, 1), 0)
        allowed = (k_pos >> shift) <= q_chunk
        late = jnp.maximum(k_pos - q_pos, 0).astype(F32)
        s_t = [jnp.where(allowed, s - (2.0 * slopes[idx // 2]) * late, NEG_BIG)
               for idx, s in enumerate(scores(j))]
        accumulate(j, s_t, allowed)
        return carry

    lax.fori_loop(0, n_before, block_before, 0)
    lax.fori_loop(n_before, n_blocks, block_masked, 0)
    pad = -tq % LANES
    gn = gn_ref[...]
    for g in range(heads):
        a0 = acc_ref[2 * g]
        a1 = acc_ref[2 * g + 1]
        o_t = (a0[:DV_D] / a0[DV_D:DV_D + 1] - lam * (a1[:DV_D] / a1[DV_D:DV_D + 1]))
        if pad:
            o_t = jnp.concatenate([o_t, jnp.zeros((DV_D, pad), F32)], axis=1)
        o = o_t.T[:tq]
        o_ref[:, g * DV_D:(g + 1) * DV_D] = (_rms(o, gn) * (1.0 - lam_init)).astype(BF16)


def diff_attention(zq, q_col, kd, k_col, vd, v_col, lam_params, norm_g, batch, seq, kv_len,
                   tq, tk, lam_init, heads):
    nq = seq // tq
    width = heads * DV_D
    return pl.pallas_call(
        functools.partial(_attn_kernel, seq=seq, kv_len=kv_len, tq=tq, tk=tk, lam_init=lam_init,
                          heads=heads),
        grid=(batch, H_D // heads, nq),
        in_specs=[pl.BlockSpec((tq, width), lambda b, h, i: (b * nq + i, q_col // heads + h)),
                  pl.BlockSpec((kv_len, width), lambda b, h, i: (b, k_col // heads + h)),
                  pl.BlockSpec((kv_len, width), lambda b, h, i: (b, v_col // heads + h)),
                  pl.BlockSpec((4, D_HD), lambda b, h, i: (0, 0)),
                  pl.BlockSpec((1, DV_D), lambda b, h, i: (0, 0))],
        out_specs=pl.BlockSpec((tq, width), lambda b, h, i: (b * nq + i, h)),
        out_shape=jax.ShapeDtypeStruct((batch * seq, H_D * DV_D), BF16),
        scratch_shapes=[pltpu.VMEM((2 * heads, kv_len, DV_D), BF16),
                        pltpu.VMEM((heads, kv_len // tk, DV_D + ONES_ROWS, tk), BF16),
                        pltpu.VMEM((2 * heads, 1, tq), F32),
                        pltpu.VMEM((2 * heads, DV_D + ONES_ROWS, tq), F32)],
        compiler_params=_params("arbitrary", "arbitrary", "arbitrary"),
        name="diff_attention",
    )(zq, kd, vd, lam_params, norm_g.reshape(1, DV_D))


def _pad_cols(w, total):
    return jnp.pad(w, ((0, 0), (0, total - w.shape[1])))


def _prep_w_in_ab(w):
    sizes = (H_A * DK_A, H_A * DK_A, H_A * DV_A, H_A * DV_A, H_B * DK_B, H_B * DK_B, H_B * DV_B,
             H_B, H_B, H_B * DV_B)
    aq, af, ai, ag, bq, bk, bv, big, bfg, bo = jnp.split(w, np_cumsum(sizes), axis=1)
    return _pad_cols(jnp.concatenate([aq, af, ai, ag, bq, bk, bv, bo, big, bfg], axis=1),
                     IN_AB_PAD).astype(BF16)


def _prep_w_in_cd(w):
    sizes = (H_C * (2 * DK_C + DV_C), H_C, H_C, H_C * DV_C, H_D * 2 * D_HD, H_D * 2 * D_HD,
             H_D * DV_D)
    cqkv, cb, ca, cz, dq, dk, dv = jnp.split(w, np_cumsum(sizes), axis=1)
    return _pad_cols(jnp.concatenate([cqkv, cz, dq, dk, dv, cb, ca], axis=1),
                     IN_CD_PAD).astype(BF16)


def np_cumsum(sizes):
    out, acc = [], 0
    for s in sizes[:-1]:
        acc += s
        out.append(acc)
    return out


def _tiles(n_rows, seq):
    tm = min(512, n_rows)
    tb = min(512, seq)
    return tm, tb


def _heads_per_step(n_heads, n_chunks, items=8):
    return max(1, min(n_heads, items // n_chunks))


def _layer_ab(x, st, p, lower_bounds, batch, seq, layer):
    n = batch * seq
    tm, tb = _tiles(n, seq)
    z = norm_matmul(x, p['norm_mix'], p['w_in'], tm, IN_AB_PAD // 3)
    n_chunks = tb // min(MIX_CHUNK, seq)
    o_a, s_new = hgrn2(z, lower_bounds, p['hgrn_norm'], st['hgrn'], batch, seq, tb, layer // 2,
                       _heads_per_step(H_A, n_chunks))
    o_b, c_new, n_new, m_new = mlstm(z, p['i_bias'], p['f_bias'], p['mlstm_norm'], st['mlstm_c'],
                                     st['mlstm_n'], st['mlstm_m'], batch, seq, tb,
                                     _heads_per_step(H_B, n_chunks))
    x = matmul_residual((o_a, o_b), p['w_out'], x, p['norm_ffn'], tm, False)
    return x, (s_new, c_new, n_new, m_new)


def _layer_cd(x, st, p, batch, seq, layer):
    n = batch * seq
    tm, tb = _tiles(n, seq)
    z = norm_matmul(x, p['norm_mix'], p['w_in'], tm, IN_CD_PAD // 3)
    tb_gdn = min(256, seq)
    heads = _heads_per_step(H_C, tb_gdn // min(CHUNK, seq), items=32)
    o_c, s_new = gdn(z, p['conv_w'], st['gdn_conv'], p['a_log'], p['dt_bias'], p['gdn_norm'],
                     st['gdn'], batch, seq, tb_gdn, heads)
    z3 = z.reshape(batch, seq, IN_CD_PAD)
    conv_new = z3[:, seq - (CONV_C - 1):, :H_C * (2 * DK_C + DV_C)]
    k_new = z3[:, :, CD_DK * LANES:CD_DV * LANES]
    v_new = z3[:, :, CD_DV * LANES:CD_GATES * LANES]
    past = st['k_cache'].shape[1]
    lam_init = 0.8 - 0.6 * math.exp(-0.3 * layer)
    if past == 0:
        o_d = diff_attention(z, CD_DQ, z, CD_DK, z, CD_DV, p['lam'], p['diff_norm'], batch, seq,
                             seq, min(512, seq), min(512, seq), lam_init, 2)
    else:
        kv_len = past + seq
        kd = jnp.concatenate([st['k_cache'].reshape(batch, past, -1), k_new], axis=1)
        vd = jnp.concatenate([st['v_cache'].reshape(batch, past, -1), v_new], axis=1)
        o_d = diff_attention(z, CD_DQ, kd.reshape(batch * kv_len, -1), 0,
                             vd.reshape(batch * kv_len, -1), 0, p['lam'], p['diff_norm'], batch,
                             seq, kv_len, seq, kv_len, lam_init, 4)
    x = matmul_residual((o_c, o_d), p['w_out'], x, p['norm_ffn'], tm, False)
    return x, (s_new, conv_new, k_new.reshape(batch, seq, H_D, 2 * D_HD),
               v_new.reshape(batch, seq, H_D, DV_D))


def _ffn(x, state, p, batch, seq, final_g):
    n = batch * seq
    tm, _ = _tiles(n, seq)
    act, conv_new = ffn_up(x, p['norm_ffn'], p['ffn_up'], p['ffn_conv_w'], p['ffn_conv_b'], state,
                           batch, seq, min(2 * tm, n), 512)
    g = p['norm_ffn'] if final_g is None else final_g
    x = matmul_residual((act,), p['ffn_down'], x, g, tm, final_g is not None)
    return x, conv_new


def _run_group(x, st, params, lower_bounds, norm_final):
    batch, seq, _ = x.shape
    x = x.reshape(batch * seq, D_MODEL)
    new_states = []
    for layer, (p, s) in enumerate(zip(params, st)):
        if layer % 2 == 0:
            x, new_mix = _layer_ab(x, s, p, lower_bounds, batch, seq, layer)
        else:
            x, new_mix = _layer_cd(x, s, p, batch, seq, layer)
        last = layer == len(params) - 1
        x, new_ffn = _ffn(x, s['ffn_conv'], p, batch, seq, norm_final if last else None)
        new_states.append(new_mix + (new_ffn,))
    return x.reshape(batch, seq, D_MODEL), new_states


def kernel(x_prompt, x_sample, state_hgrn_0, state_mlstm_c_0, state_mlstm_n_0, state_mlstm_m_0,
           state_ffn_conv_0, state_gdn_1, state_gdn_conv_1, cache_k_1, cache_v_1, state_ffn_conv_1,
           hgrn_lower_bounds, norm_mix_0, w_in_0, mlstm_i_bias_0, mlstm_f_bias_0, hgrn_norm_0,
           mlstm_norm_0, w_out_0, norm_ffn_0, ffn_up_0, ffn_conv_w_0, ffn_conv_b_0, ffn_down_0,
           norm_mix_1, w_in_1, gdn_conv_w_1, gdn_a_log_1, gdn_dt_bias_1, gdn_norm_1, diff_lambda_1,
           diff_norm_1, w_out_1, norm_ffn_1, ffn_up_1, ffn_conv_w_1, ffn_conv_b_1, ffn_down_1,
           norm_final):
    params = [
        dict(norm_mix=norm_mix_0, w_in=_prep_w_in_ab(w_in_0), i_bias=mlstm_i_bias_0,
             f_bias=mlstm_f_bias_0, hgrn_norm=hgrn_norm_0, mlstm_norm=mlstm_norm_0,
             w_out=w_out_0.astype(BF16), norm_ffn=norm_ffn_0, ffn_up=ffn_up_0,
             ffn_conv_w=ffn_conv_w_0, ffn_conv_b=ffn_conv_b_0, ffn_down=ffn_down_0.astype(BF16)),
        dict(norm_mix=norm_mix_1, w_in=_prep_w_in_cd(w_in_1), conv_w=gdn_conv_w_1,
             a_log=gdn_a_log_1, dt_bias=gdn_dt_bias_1, gdn_norm=gdn_norm_1, lam=diff_lambda_1,
             diff_norm=diff_norm_1, w_out=w_out_1.astype(BF16), norm_ffn=norm_ffn_1,
             ffn_up=ffn_up_1, ffn_conv_w=ffn_conv_w_1, ffn_conv_b=ffn_conv_b_1,
             ffn_down=ffn_down_1.astype(BF16)),
    ]
    bp = x_prompt.shape[0]
    st_prompt = [
        dict(hgrn=jnp.zeros((bp, H_A, DK_A, DV_A), F32), mlstm_c=jnp.zeros((bp, H_B, DK_B, DV_B), F32),
             mlstm_n=jnp.zeros((bp, H_B, DK_B), F32), mlstm_m=jnp.full((bp, H_B), NEG_BIG, F32),
             ffn_conv=jnp.zeros((bp, CONV_FFN - 1, D_FF), F32)),
        dict(gdn=jnp.zeros((bp, H_C, DK_C, DV_C), F32),
             gdn_conv=jnp.zeros((bp, CONV_C - 1, H_C * (2 * DK_C + DV_C)), F32),
             k_cache=jnp.zeros((bp, 0, H_D, 2 * D_HD), F32), v_cache=jnp.zeros((bp, 0, H_D, DV_D), F32),
             ffn_conv=jnp.zeros((bp, CONV_FFN - 1, D_FF), F32)),
    ]
    st_sample = [
        dict(hgrn=state_hgrn_0, mlstm_c=state_mlstm_c_0, mlstm_n=state_mlstm_n_0,
             mlstm_m=state_mlstm_m_0, ffn_conv=state_ffn_conv_0),
        dict(gdn=state_gdn_1, gdn_conv=state_gdn_conv_1, k_cache=cache_k_1, v_cache=cache_v_1,
             ffn_conv=state_ffn_conv_1),
    ]
    y_p, new_p = _run_group(x_prompt, st_prompt, params, hgrn_lower_bounds, norm_final)
    y_s, new_s = _run_group(x_sample, st_sample, params, hgrn_lower_bounds, norm_final)
    (hgrn_p, c_p, n_p, m_p, f0_p), (gdn_p, gc_p, k_p, v_p, f1_p) = new_p
    (hgrn_s, c_s, n_s, m_s, f0_s), (gdn_s, gc_s, k_s, v_s, f1_s) = new_s
    return (y_p, y_s, hgrn_p, hgrn_s, c_p, c_s, n_p, n_s, m_p, m_s, f0_p, f0_s, gdn_p, gdn_s,
            gc_p, gc_s, k_p, k_s, v_p, v_s, f1_p, f1_s)
```
